```python
import functools
import jax, jax.numpy as jnp
from jax import lax
import numpy as np

D_MODEL = 1024
BATCH = 32
SEQ = 256
DEPTH = 4
DEC_BATCH = 2
DEC_SEQ = 4096
PAST_LEN = 256

GRID_W = 64
HEAD_DIM = 64
Q_BLOCK = 128
EPS = 1e-6
ROPE_THETA = 10000.0
MIX_WIDTH = D_MODEL
N_EVEN = (DEPTH + 1) // 2
N_ODD = DEPTH // 2

POOL_WINDOWS = (2, 4, 8, 16)
POOL_GROUPS = len(POOL_WINDOWS)
POOL_WIDTH = MIX_WIDTH // 4
POOL_GW = POOL_WIDTH // POOL_GROUPS
B_HEADS = (MIX_WIDTH - POOL_WIDTH) // HEAD_DIM
B_KV = B_HEADS // 3
EVEN_SIZES = (POOL_WIDTH, B_HEADS * HEAD_DIM, B_KV * HEAD_DIM, B_KV * HEAD_DIM)
C_HEADS = MIX_WIDTH // 2 // HEAD_DIM
C_KV = C_HEADS // 4
C_WINDOW = 128
D_HEADS = MIX_WIDTH // 2 // HEAD_DIM
NA_ROWS = 8
NA_COLS = 16
ODD_SIZES = (C_HEADS * HEAD_DIM, C_KV * HEAD_DIM, C_KV * HEAD_DIM,
             D_HEADS * HEAD_DIM, D_HEADS * HEAD_DIM, D_HEADS * HEAD_DIM)
D_FF = 2816
CONV_W = 3

kernel_name = 'hybrid_diffusion_prefix_trunk_step'


def split_cols(u, sizes):
    offs, acc = [], 0
    for s in sizes[:-1]:
        acc += s
        offs.append(acc)
    return jnp.split(u, offs, axis=-1)


def rmsnorm(x, g):
    x32 = x.astype(jnp.float32)
    y = x32 * lax.rsqrt(jnp.mean(x32 * x32, axis=-1, keepdims=True) + EPS)
    return (y * g.astype(jnp.float32)).astype(x.dtype)


def modulation(cvec, w_mod, b_mod):
    m = jax.nn.silu(cvec) @ w_mod + b_mod
    return tuple(t[:, None, :] for t in jnp.split(m, 6, axis=-1))


def axial_rope(n_tokens):
    t = jnp.arange(n_tokens)
    row = (t // GRID_W).astype(jnp.float32)
    col = (t % GRID_W).astype(jnp.float32)
    axis_dim = HEAD_DIM // 2
    inv_freq = ROPE_THETA ** (-jnp.arange(0, axis_dim, 2, dtype=jnp.float32) / axis_dim)
    ang = jnp.concatenate([row[:, None] * inv_freq, col[:, None] * inv_freq], axis=-1)
    return jnp.cos(ang)[None, :, None, :], jnp.sin(ang)[None, :, None, :]


def apply_rope(x, cos, sin):
    x32 = x.astype(jnp.float32)
    x1, x2 = x32[..., 0::2], x32[..., 1::2]
    out = jnp.stack([x1 * cos - x2 * sin, x1 * sin + x2 * cos], axis=-1).reshape(x.shape)
    return out.astype(x.dtype)


def softmax_logits(s, sink):
    if sink is None:
        return jax.nn.softmax(s, axis=-1)
    m = jnp.maximum(jnp.max(s, axis=-1, keepdims=True), sink)
    e = jnp.exp(s - m)
    return e / (jnp.sum(e, axis=-1, keepdims=True) + jnp.exp(sink - m))


def attend_blocked(q, k, v, sink=None):
    Bn, Sq, H, dh = q.shape
    KV = k.shape[2]
    G = H // KV
    scale = dh ** -0.5
    qb = jnp.moveaxis(q.reshape(Bn, Sq // Q_BLOCK, Q_BLOCK, KV, G, dh), 1, 0)
    sink_b = None if sink is None else sink.astype(jnp.float32).reshape(KV, G)[None, :, :, None, None]

    def one_block(qblk):
        s = jnp.einsum('bqkgd,bskd->bkgqs', qblk, k, preferred_element_type=jnp.float32) * scale
        p = softmax_logits(s, sink_b).astype(v.dtype)
        return jnp.einsum('bkgqs,bskd->bqkgd', p, v)

    o = lax.map(one_block, qb)
    return jnp.moveaxis(o, 0, 1).reshape(Bn, Sq, H, dh)


def window_attention(q, k, v, ctx_k, ctx_v, sink):
    Bn, N, H, dh = q.shape
    KV = k.shape[2]
    G = H // KV
    nb = N // Q_BLOCK
    scale = dh ** -0.5

    def band(x):
        xb = jnp.pad(x.reshape(Bn, nb, Q_BLOCK, KV, dh), ((0, 0), (1, 1), (0, 0), (0, 0), (0, 0)))
        return jnp.concatenate([xb[:, :-2], xb[:, 1:-1], xb[:, 2:]], axis=2)

    kb, vb = band(k), band(v)
    qb = q.reshape(Bn, nb, Q_BLOCK, KV, G, dh)
    q_pos = jnp.arange(N).reshape(nb, Q_BLOCK)
    k_pos = (jnp.arange(nb)[:, None] - 1) * Q_BLOCK + jnp.arange(3 * Q_BLOCK)[None, :]
    valid = ((jnp.abs(q_pos[:, :, None] - k_pos[:, None, :]) <= C_WINDOW)
             & (k_pos >= 0)[:, None, :] & (k_pos < N)[:, None, :])
    s_loc = jnp.einsum('bnqkgd,bnskd->bnkgqs', qb, kb, preferred_element_type=jnp.float32) * scale
    s_loc = jnp.where(valid[None, :, None, None], s_loc, -jnp.inf)
    s_ctx = jnp.einsum('bnqkgd,bskd->bnkgqs', qb, ctx_k, preferred_element_type=jnp.float32) * scale
    sink_b = sink.astype(jnp.float32).reshape(KV, G)[None, None, :, :, None, None]
    p = softmax_logits(jnp.concatenate([s_loc, s_ctx], axis=-1), sink_b).astype(v.dtype)
    n_loc = 3 * Q_BLOCK
    o = (jnp.einsum('bnkgqs,bnskd->bnqkgd', p[..., :n_loc], vb)
         + jnp.einsum('bnkgqs,bskd->bnqkgd', p[..., n_loc:], ctx_v))
    return o.reshape(Bn, N, H, dh)


def neighbourhood_attention(q, k, v, ctx_k, ctx_v, rpb):
    Bn, N, H, dh = q.shape
    rows = N // GRID_W
    kh = min(NA_ROWS, rows)
    kw = NA_COLS
    scale = dh ** -0.5
    qg = q.reshape(Bn, rows, GRID_W, H, dh)
    kg = k.reshape(Bn, rows, GRID_W, H, dh)
    vg = v.reshape(Bn, rows, GRID_W, H, dh)
    col = jnp.arange(GRID_W)
    col_start = jnp.clip(col - kw // 2, 0, GRID_W - kw)
    col_idx = col_start[:, None] + jnp.arange(kw)[None, :]
    ix = (col_idx - col[:, None] + NA_COLS - 1)[:, None, :]
    n_loc = kh * kw

    def one_row(r):
        row_start = jnp.clip(r - kh // 2, 0, rows - kh)
        iy = (row_start + jnp.arange(kh) - r + NA_ROWS - 1)[None, :, None]
        bias = rpb[:, iy, ix].reshape(H, GRID_W, n_loc)

        def gather(x):
            xr = lax.dynamic_slice_in_dim(x, row_start, kh, axis=1)
            xn = xr[:, :, col_idx]
            return jnp.swapaxes(xn, 1, 2).reshape(Bn, GRID_W, n_loc, H, dh)

        kn, vn = gather(kg), gather(vg)
        qr = lax.dynamic_index_in_dim(qg, r, axis=1, keepdims=False)
        s_loc = jnp.einsum('bwhd,bwnhd->bhwn', qr, kn, preferred_element_type=jnp.float32) * scale + bias
        s_ctx = jnp.einsum('bwhd,bshd->bhws', qr, ctx_k, preferred_element_type=jnp.float32) * scale
        p = jax.nn.softmax(jnp.concatenate([s_loc, s_ctx], axis=-1), axis=-1).astype(v.dtype)
        return (jnp.einsum('bhwn,bwnhd->bwhd', p[..., :n_loc], vn)
                + jnp.einsum('bhws,bshd->bwhd', p[..., n_loc:], ctx_v))

    out = lax.map(one_row, jnp.arange(rows))
    return jnp.moveaxis(out, 0, 1).reshape(Bn, N, H, dh)


def pool_mixer(a, w_pool, pool_scale):
    Bn, N, _ = a.shape
    a32 = a.reshape(Bn, N, POOL_GROUPS, POOL_GW).astype(jnp.float32)
    csum = jnp.pad(jnp.cumsum(a32, axis=1), ((0, 0), (1, 0), (0, 0), (0, 0)))
    t = jnp.arange(N)[:, None]
    w = jnp.array(POOL_WINDOWS)[None, :]
    lo = jnp.clip(t - w // 2, 0, N)
    hi = jnp.clip(t - w // 2 + w, 0, N)
    grp = jnp.arange(POOL_GROUPS)[None, :]
    win_sum = csum[:, hi, grp] - csum[:, lo, grp]
    mean = win_sum / (hi - lo).astype(jnp.float32)[None, :, :, None]
    pooled = (mean - a32).astype(a.dtype)
    y = jnp.einsum('bngc,gcd->bngd', pooled, w_pool)
    return y.reshape(Bn, N, POOL_WIDTH) * pool_scale


def even_mixer(h, w_in, w_out, w_pool, pool_scale, g_q, g_k, rope=None, ctx=None):
    Bn, N, _ = h.shape
    a, q, k, v = split_cols(h @ w_in, EVEN_SIZES)
    q = rmsnorm(q.reshape(Bn, N, B_HEADS, HEAD_DIM), g_q)
    k = rmsnorm(k.reshape(Bn, N, B_KV, HEAD_DIM), g_k)
    v = v.reshape(Bn, N, B_KV, HEAD_DIM)
    ya = pool_mixer(a, w_pool, pool_scale)
    if ctx is None:
        yb = attend_blocked(q, k, v)
    else:
        cos, sin = rope
        ctx_k, ctx_v = ctx
        yb = attend_blocked(apply_rope(q, cos, sin),
                            jnp.concatenate([apply_rope(k, cos, sin), ctx_k], axis=1),
                            jnp.concatenate([v, ctx_v], axis=1))
    y = jnp.concatenate([ya, yb.reshape(Bn, N, -1)], axis=-1) @ w_out
    return y, (k, v)


def odd_mixer(h, w_in, w_out, sink, rpb, rope=None, ctx=None):
    Bn, N, _ = h.shape
    qc, kc, vc, qd, kd, vd = split_cols(h @ w_in, ODD_SIZES)
    qc = qc.reshape(Bn, N, C_HEADS, HEAD_DIM)
    kc = kc.reshape(Bn, N, C_KV, HEAD_DIM)
    vc = vc.reshape(Bn, N, C_KV, HEAD_DIM)
    qd = qd.reshape(Bn, N, D_HEADS, HEAD_DIM)
    kd = kd.reshape(Bn, N, D_HEADS, HEAD_DIM)
    vd = vd.reshape(Bn, N, D_HEADS, HEAD_DIM)
    if ctx is None:
        yc = attend_blocked(qc, kc, vc, sink)
        yd = attend_blocked(qd, kd, vd)
    else:
        cos, sin = rope
        ck_c, cv_c, ck_d, cv_d = ctx
        yc = window_attention(apply_rope(qc, cos, sin), apply_rope(kc, cos, sin), vc, ck_c, cv_c, sink)
        yd = neighbourhood_attention(qd, kd, vd, ck_d, cv_d, rpb)
    y = jnp.concatenate([yc.reshape(Bn, N, -1), yd.reshape(Bn, N, -1)], axis=-1) @ w_out
    return y, (kc, vc, kd, vd)


def conv_ffn(h, w_in, w_conv, w_out):
    N = h.shape[1]
    u = h @ w_in
    pad = CONV_W // 2
    up = jnp.pad(u, ((0, 0), (pad, pad), (0, 0)))
    u = sum(up[:, j:j + N] * w_conv[j] for j in range(CONV_W))
    gate, val = jnp.split(u, 2, axis=-1)
    return (jax.nn.silu(gate) * val) @ w_out


def trunk_layer(x, mod, g_mix_pre, g_mix_post, g_ffn_pre, g_ffn_post, w_ffn_in, w_ffn_conv, w_ffn_out, mixer):
    shift1, scale1, gate1, shift2, scale2, gate2 = mod
    h = rmsnorm(x, g_mix_pre) * (1 + scale1) + shift1
    y, ctx_state = mixer(h)
    x = x + gate1 * rmsnorm(y, g_mix_post)
    h = rmsnorm(x, g_ffn_pre) * (1 + scale2) + shift2
    x = x + gate2 * rmsnorm(conv_ffn(h, w_ffn_in, w_ffn_conv, w_ffn_out), g_ffn_post)
    return x, ctx_state


def setup_inputs(seed: int = 0) -> dict:
    key = jax.random.key(seed)
    ks = jax.random.split(key, 28)

    def nrm(k, shape, scale=1.0):
        return jax.random.normal(k, shape, jnp.float32) * scale

    even_in = sum(EVEN_SIZES)
    odd_in = sum(ODD_SIZES)
    return {
        'x_prompt': nrm(ks[0], (BATCH, SEQ, D_MODEL)),
        'x_sample': nrm(ks[1], (DEC_BATCH, DEC_SEQ, D_MODEL)),
        'cache_b_k': nrm(ks[2], (DEC_BATCH, N_EVEN, PAST_LEN, B_KV, HEAD_DIM)),
        'cache_b_v': nrm(ks[3], (DEC_BATCH, N_EVEN, PAST_LEN, B_KV, HEAD_DIM)),
        'cache_c_k': nrm(ks[4], (DEC_BATCH, N_ODD, PAST_LEN, C_KV, HEAD_DIM)),
        'cache_c_v': nrm(ks[5], (DEC_BATCH, N_ODD, PAST_LEN, C_KV, HEAD_DIM)),
        'cache_d_k': nrm(ks[6], (DEC_BATCH, N_ODD, PAST_LEN, D_HEADS, HEAD_DIM)),
        'cache_d_v': nrm(ks[7], (DEC_BATCH, N_ODD, PAST_LEN, D_HEADS, HEAD_DIM)),
        'c': nrm(ks[8], (DEC_BATCH, D_MODEL)),
        'c_ctx': nrm(ks[9], (D_MODEL,)),
        'w_mod': nrm(ks[10], (DEPTH, D_MODEL, 6 * D_MODEL), 0.5 * D_MODEL ** -0.5),
        'b_mod': nrm(ks[11], (DEPTH, 6 * D_MODEL), 0.01),
        'g_mix_pre': 1.0 + nrm(ks[12], (DEPTH, D_MODEL), 0.05),
        'g_mix_post': 1.0 + nrm(ks[13], (DEPTH, D_MODEL), 0.05),
        'g_ffn_pre': 1.0 + nrm(ks[14], (DEPTH, D_MODEL), 0.05),
        'g_ffn_post': 1.0 + nrm(ks[15], (DEPTH, D_MODEL), 0.05),
        'w_in_even': nrm(ks[16], (N_EVEN, D_MODEL, even_in), D_MODEL ** -0.5),
        'w_pool': nrm(ks[17], (N_EVEN, POOL_GROUPS, POOL_GW, POOL_GW), POOL_GW ** -0.5),
        'pool_scale': 1.0 + nrm(ks[18], (N_EVEN, POOL_WIDTH), 0.1),
        'g_q_b': 1.0 + nrm(ks[19], (N_EVEN, HEAD_DIM), 0.05),
        'g_k_b': 1.0 + nrm(ks[20], (N_EVEN, HEAD_DIM), 0.05),
        'w_in_odd': nrm(ks[21], (N_ODD, D_MODEL, odd_in), D_MODEL ** -0.5),
        'sink_c': nrm(ks[22], (N_ODD, C_HEADS)),
        'rpb_d': nrm(ks[23], (N_ODD, D_HEADS, 2 * NA_ROWS - 1, 2 * NA_COLS - 1), 0.2),
        'w_mix_out': nrm(ks[24], (DEPTH, MIX_WIDTH, D_MODEL), MIX_WIDTH ** -0.5),
        'w_ffn_in': nrm(ks[25], (DEPTH, D_MODEL, 2 * D_FF), D_MODEL ** -0.5),
        'w_ffn_conv': nrm(ks[26], (DEPTH, CONV_W, 2 * D_FF), CONV_W ** -0.5),
        'w_ffn_out': nrm(ks[27], (DEPTH, D_FF, D_MODEL), D_FF ** -0.5),
    }


def reference(x_prompt, x_sample, cache_b_k, cache_b_v, cache_c_k, cache_c_v, cache_d_k, cache_d_v,
              c, c_ctx, w_mod, b_mod, g_mix_pre, g_mix_post, g_ffn_pre, g_ffn_post,
              w_in_even, w_pool, pool_scale, g_q_b, g_k_b, w_in_odd, sink_c, rpb_d,
              w_mix_out, w_ffn_in, w_ffn_conv, w_ffn_out):
    cos, sin = axial_rope(x_sample.shape[1])
    xp, xs = x_prompt, x_sample
    b_k, b_v, c_k, c_v, d_k, d_v = [], [], [], [], [], []
    for l in range(DEPTH):
        i = l // 2
        ffn = (g_mix_pre[l], g_mix_post[l], g_ffn_pre[l], g_ffn_post[l], w_ffn_in[l], w_ffn_conv[l], w_ffn_out[l])
        mod_p = modulation(c_ctx[None, :], w_mod[l], b_mod[l])
        mod_s = modulation(c, w_mod[l], b_mod[l])
        if l % 2 == 0:
            mix = functools.partial(even_mixer, w_in=w_in_even[i], w_out=w_mix_out[l], w_pool=w_pool[i],
                                    pool_scale=pool_scale[i], g_q=g_q_b[i], g_k=g_k_b[i])
            xp, (kb, vb) = trunk_layer(xp, mod_p, *ffn, mixer=mix)
            xs, _ = trunk_layer(xs, mod_s, *ffn, mixer=functools.partial(
                mix, rope=(cos, sin), ctx=(cache_b_k[:, i], cache_b_v[:, i])))
            b_k.append(kb)
            b_v.append(vb)
        else:
            mix = functools.partial(odd_mixer, w_in=w_in_odd[i], w_out=w_mix_out[l], sink=sink_c[i], rpb=rpb_d[i])
            xp, (kc, vc, kd, vd) = trunk_layer(xp, mod_p, *ffn, mixer=mix)
            xs, _ = trunk_layer(xs, mod_s, *ffn, mixer=functools.partial(
                mix, rope=(cos, sin),
                ctx=(cache_c_k[:, i], cache_c_v[:, i], cache_d_k[:, i], cache_d_v[:, i])))
            c_k.append(kc)
            c_v.append(vc)
            d_k.append(kd)
            d_v.append(vd)
    new_b_k = jnp.stack(b_k, axis=1)
    new_b_v = jnp.stack(b_v, axis=1)
    new_c_k = jnp.stack(c_k, axis=1)
    new_c_v = jnp.stack(c_v, axis=1)
    new_d_k = jnp.stack(d_k, axis=1)
    new_d_v = jnp.stack(d_v, axis=1)
    return (xp, xs, new_b_k, new_b_v, new_c_k, new_c_v, new_d_k, new_d_v)
```

```python
import functools

import jax
import jax.numpy as jnp
from jax import lax
from jax.experimental import pallas as pl
from jax.experimental.pallas import tpu as pltpu

F32 = jnp.float32
BF16 = jnp.bfloat16

D_MODEL = 1024
DEPTH = 4
GRID_W = 64
HEAD_DIM = 64
EPS = 1e-6
ROPE_THETA = 10000.0
POOL_WINDOWS = (2, 4, 8, 16)
POOL_WIDTH = 256
B_HEADS, B_KV = 12, 4
C_HEADS, C_KV = 8, 2
D_HEADS = 8
C_WINDOW = 128
NA_ROWS, NA_COLS = 8, 16
D_FF = 2816
QK_SCALE = HEAD_DIM ** -0.5
NEG = -1e30

LANES = 128
HALO = 16
VMEM_LIMIT = 56 * 1024 * 1024

TM_PROJ = 512
TM_FFN = 512
TM_POOL = 256
FF_CHUNK = 256
TQ_FLASH = 256
TK_FLASH = 512
TQ_WIN = 128


def _cparams(sem):
    return pltpu.CompilerParams(dimension_semantics=sem, vmem_limit_bytes=VMEM_LIMIT)


def _norm_mod(x, g, scale, shift):
    ms = jnp.mean(x * x, axis=-1, keepdims=True)
    return (x * lax.rsqrt(ms + EPS) * g) * (1.0 + scale) + shift


def _rmsnorm(x, g):
    ms = jnp.mean(x * x, axis=-1, keepdims=True)
    return x * lax.rsqrt(ms + EPS) * g


def _dot(a, b):
    return jnp.dot(a, b, preferred_element_type=F32)


def _dot_nt(a, b):
    return lax.dot_general(a, b, (((1,), (1,)), ((), ())), preferred_element_type=F32)


def _low_half(rows):
    return lax.broadcasted_iota(jnp.int32, (rows, LANES), 1) < HEAD_DIM


def _mod_kernel(c_ref, w_ref, b_ref, o_ref):
    cv = c_ref[...]
    s = cv * (1.0 / (1.0 + jnp.exp(-cv)))
    o_ref[...] = _dot(s.astype(BF16), w_ref[...].astype(BF16)) + b_ref[...]


def _modulation(cvecs, w_mod, b_mod):
    tn = 1536
    return pl.pallas_call(
        _mod_kernel,
        out_shape=jax.ShapeDtypeStruct((DEPTH, 8, 6 * D_MODEL), F32),
        grid=(DEPTH, 6 * D_MODEL // tn),
        in_specs=[
            pl.BlockSpec((8, D_MODEL), lambda l, j: (0, 0)),
            pl.BlockSpec((None, D_MODEL, tn), lambda l, j: (l, 0, j)),
            pl.BlockSpec((None, 1, tn), lambda l, j: (l, 0, j)),
        ],
        out_specs=pl.BlockSpec((None, 8, tn), lambda l, j: (l, 0, j)),
        compiler_params=_cparams(("parallel", "parallel")),
        name="modulation",
    )(cvecs, w_mod, b_mod.reshape(DEPTH, 1, 6 * D_MODEL))


def _pair_swap(s):
    lane = lax.broadcasted_iota(jnp.int32, s.shape, 1)
    n = s.shape[1]
    return jnp.where(lane % 2 == 0, pltpu.roll(s, n - 1, 1), pltpu.roll(s, 1, 1))


def _store_expanded_q(q_ref, s, slab, kv_half_of, low):
    sr = pltpu.roll(s, HEAD_DIM, 1)
    for half in (0, 1):
        hd = 2 * slab + half
        dst = kv_half_of(hd)
        val = s if dst == half else sr
        keep = low if dst == 0 else jnp.logical_not(low)
        q_ref[:, hd * LANES:(hd + 1) * LANES] = jnp.where(keep, val, 0.0).astype(BF16)


def _proj_even_kernel(*refs, rope):
    if rope:
        (x_ref, mod_ref, g_ref, w_ref, gq_ref, gk_ref, e_ref, cos_ref, sin_ref,
         a_ref, q_ref, k_ref, v_ref) = refs
    else:
        (x_ref, mod_ref, g_ref, w_ref, gq_ref, gk_ref, e_ref,
         a_ref, q_ref, k_ref, v_ref) = refs
    rows = x_ref.shape[0]
    h = _norm_mod(x_ref[...], g_ref[...], mod_ref[1:2, :], mod_ref[0:1, :])
    u = _dot(h.astype(BF16), w_ref[...])
    low = _low_half(rows)
    a_ref[...] = u[:, :POOL_WIDTH]

    def headnorm(s, g):
        ms = _dot((s * s).astype(BF16), e_ref[...])
        return s * lax.rsqrt(ms + EPS) * g

    def rot(s):
        if not rope:
            return s
        return s * cos_ref[...] + _pair_swap(s) * sin_ref[...]

    q0 = POOL_WIDTH
    for j in range(B_HEADS // 2):
        s = u[:, q0 + LANES * j:q0 + LANES * (j + 1)]
        s = rot(headnorm(s, gq_ref[...])) * QK_SCALE
        _store_expanded_q(q_ref, s, j, lambda hd: (hd // (B_HEADS // B_KV)) % 2, low)
    k0 = q0 + B_HEADS * HEAD_DIM
    for j in range(B_KV // 2):
        s = headnorm(u[:, k0 + LANES * j:k0 + LANES * (j + 1)], gk_ref[...])
        k_ref[:, LANES * j:LANES * (j + 1)] = rot(s).astype(k_ref.dtype)
    v0 = k0 + B_KV * HEAD_DIM
    v_ref[...] = u[:, v0:v0 + B_KV * HEAD_DIM].astype(v_ref.dtype)


def _proj_odd_kernel(*refs, rope):
    if rope:
        (x_ref, mod_ref, g_ref, w_ref, cos_ref, sin_ref,
         qc_ref, kc_ref, vc_ref, qd_ref, kd_ref, vd_ref) = refs
    else:
        (x_ref, mod_ref, g_ref, w_ref,
         qc_ref, kc_ref, vc_ref, qd_ref, kd_ref, vd_ref) = refs
    rows = x_ref.shape[0]
    h = _norm_mod(x_ref[...], g_ref[...], mod_ref[1:2, :], mod_ref[0:1, :])
    u = _dot(h.astype(BF16), w_ref[...])
    low = _low_half(rows)

    def rot(s):
        if not rope:
            return s
        return s * cos_ref[...] + _pair_swap(s) * sin_ref[...]

    for j in range(C_HEADS // 2):
        s = rot(u[:, LANES * j:LANES * (j + 1)]) * QK_SCALE
        _store_expanded_q(qc_ref, s, j, lambda hd: hd // (C_HEADS // C_KV), low)
    o = C_HEADS * HEAD_DIM
    kc_ref[...] = rot(u[:, o:o + LANES]).astype(kc_ref.dtype)
    o += C_KV * HEAD_DIM
    vc_ref[...] = u[:, o:o + LANES].astype(vc_ref.dtype)
    o += C_KV * HEAD_DIM
    for j in range(D_HEADS // 2):
        s = u[:, o + LANES * j:o + LANES * (j + 1)] * QK_SCALE
        _store_expanded_q(qd_ref, s, j, lambda hd: hd % 2, low)
    o += D_HEADS * HEAD_DIM
    kd_ref[...] = u[:, o:o + D_HEADS * HEAD_DIM].astype(kd_ref.dtype)
    o += D_HEADS * HEAD_DIM
    vd_ref[...] = u[:, o:o + D_HEADS * HEAD_DIM].astype(vd_ref.dtype)


def _mod_index(seq_blocks):
    if seq_blocks is None:
        return lambda i: (0, 0, 0)
    return lambda i: (1 + i // seq_blocks, 0, 0)


def _proj_in(x, mod, g, w, extras, out_widths, out_dtypes, kernel, seq_blocks, rope_tabs, name):
    rows = x.shape[0]
    tm = TM_PROJ
    n_in = w.shape[1]
    const = lambda i: (0, 0)
    in_specs = [
        pl.BlockSpec((tm, D_MODEL), lambda i: (i, 0)),
        pl.BlockSpec((None, 6, D_MODEL), _mod_index(seq_blocks)),
        pl.BlockSpec((1, D_MODEL), const),
        pl.BlockSpec((D_MODEL, n_in), const),
    ]
    args = [x, mod, g, w]
    for e in extras:
        in_specs.append(pl.BlockSpec(e.shape, const))
        args.append(e)
    if rope_tabs is not None:
        for t in rope_tabs:
            in_specs.append(pl.BlockSpec((tm, LANES), lambda i: (i % seq_blocks, 0)))
            args.append(t)
    return pl.pallas_call(
        functools.partial(kernel, rope=rope_tabs is not None),
        out_shape=[jax.ShapeDtypeStruct((rows, wd), dt) for wd, dt in zip(out_widths, out_dtypes)],
        grid=(rows // tm,),
        in_specs=in_specs,
        out_specs=[pl.BlockSpec((tm, wd), lambda i: (i, 0)) for wd in out_widths],
        compiler_params=_cparams(("parallel",)),
        name=name,
    )(*args)


def _place_heads(out_ref, col0, heads, kv_half_of, low):
    for m in range(len(heads) // 2):
        a = heads[2 * m]
        if kv_half_of(2 * m) == 1:
            a = pltpu.roll(a, HEAD_DIM, 1)
        b = heads[2 * m + 1]
        if kv_half_of(2 * m + 1) == 0:
            b = pltpu.roll(b, HEAD_DIM, 1)
        out_ref[:, col0 + m * LANES:col0 + (m + 1) * LANES] = jnp.where(low, a, b).astype(out_ref.dtype)


def _stack_heads(q_ref, heads):
    return jnp.concatenate([q_ref[:, h * LANES:(h + 1) * LANES] for h in heads], axis=0)


def _softmax_pv(scores, values, sink=None):
    m = scores[0].max(axis=-1, keepdims=True)
    for s in scores[1:]:
        m = jnp.maximum(m, s.max(axis=-1, keepdims=True))
    if sink is not None:
        m = jnp.maximum(m, sink)
    den = None
    acc = None
    for s, v in zip(scores, values):
        e = jnp.exp(s - m)
        d = e.sum(axis=-1, keepdims=True)
        o = _dot(e.astype(BF16), v)
        den = d if den is None else den + d
        acc = o if acc is None else acc + o
    if sink is not None:
        den = den + jnp.exp(sink - m)
    return acc / den


def _sink_column(sink_ref, heads, rows_per_head):
    row = lax.broadcasted_iota(jnp.int32, (len(heads) * rows_per_head, 1), 0)
    col = jnp.full(row.shape, sink_ref[heads[-1]], F32)
    for g in range(len(heads) - 2, -1, -1):
        col = jnp.where(row < (g + 1) * rows_per_head, sink_ref[heads[g]], col)
    return col


def _ctx_even_attn_kernel(q_ref, k_ref, v_ref, y_ref):
    rows = q_ref.shape[0]
    low = _low_half(rows)
    grp = B_HEADS // B_KV
    outs = []
    for kv in range(B_KV):
        sl = slice((kv // 2) * LANES, (kv // 2 + 1) * LANES)
        ks = k_ref[:, sl].astype(BF16)
        vs = v_ref[:, sl].astype(BF16)
        heads = list(range(kv * grp, (kv + 1) * grp))
        o = _softmax_pv([_dot_nt(_stack_heads(q_ref, heads), ks)], [vs])
        outs += [o[g * rows:(g + 1) * rows] for g in range(grp)]
    _place_heads(y_ref, 0, outs, lambda hd: (hd // grp) % 2, low)


def _ctx_odd_attn_kernel(sink_ref, qc_ref, kc_ref, vc_ref, qd_ref, kd_ref, vd_ref, y_ref):
    rows = qc_ref.shape[0]
    low = _low_half(rows)
    grp = C_HEADS // C_KV
    ks = kc_ref[...].astype(BF16)
    vs = vc_ref[...].astype(BF16)
    outs = []
    for kv in range(C_KV):
        heads = list(range(kv * grp, (kv + 1) * grp))
        sink = _sink_column(sink_ref, heads, rows)
        o = _softmax_pv([_dot_nt(_stack_heads(qc_ref, heads), ks)], [vs], sink)
        outs += [o[g * rows:(g + 1) * rows] for g in range(grp)]
    _place_heads(y_ref, 0, outs, lambda hd: hd // grp, low)
    outs = []
    for hd in range(D_HEADS):
        sl = slice((hd // 2) * LANES, (hd // 2 + 1) * LANES)
        s = _dot_nt(qd_ref[:, hd * LANES:(hd + 1) * LANES], kd_ref[:, sl].astype(BF16))
        outs.append(_softmax_pv([s], [vd_ref[:, sl].astype(BF16)]))
    _place_heads(y_ref, C_HEADS * HEAD_DIM, outs, lambda hd: hd % 2, low)


def _ctx_even_attn(q, k, v, seq):
    rows = q.shape[0]
    blk = lambda wd: pl.BlockSpec((seq, wd), lambda b: (b, 0))
    return pl.pallas_call(
        _ctx_even_attn_kernel,
        out_shape=jax.ShapeDtypeStruct((rows, B_HEADS * HEAD_DIM), BF16),
        grid=(rows // seq,),
        in_specs=[blk(q.shape[1]), blk(k.shape[1]), blk(v.shape[1])],
        out_specs=blk(B_HEADS * HEAD_DIM),
        compiler_params=_cparams(("parallel",)),
        name="ctx_even_attn",
    )(q, k, v)


def _ctx_odd_attn(sink, qc, kc, vc, qd, kd, vd, seq):
    rows = qc.shape[0]
    blk = lambda a: pl.BlockSpec((seq, a.shape[1]), lambda b: (b, 0))
    return pl.pallas_call(
        _ctx_odd_attn_kernel,
        out_shape=jax.ShapeDtypeStruct((rows, D_MODEL), BF16),
        grid=(rows // seq,),
        in_specs=[pl.BlockSpec(memory_space=pltpu.SMEM)] + [blk(a) for a in (qc, kc, vc, qd, kd, vd)],
        out_specs=pl.BlockSpec((seq, D_MODEL), lambda b: (b, 0)),
        compiler_params=_cparams(("parallel",)),
        name="ctx_odd_attn",
    )(sink, qc, kc, vc, qd, kd, vd)


def _pool_kernel(ap_ref, a_ref, an_ref, w_ref, ps_ref, y_ref, *, seq):
    tm = a_ref.shape[0]
    i = pl.program_id(0)
    pos0 = (i * tm) % seq
    a = a_ref[...]
    a_ext = jnp.concatenate([ap_ref[...], a, an_ref[...]], axis=0).astype(BF16)
    ext = tm + 2 * HALO
    r = lax.broadcasted_iota(jnp.int32, (tm, ext), 0)
    c = lax.broadcasted_iota(jnp.int32, (tm, ext), 1)
    pos_c = pos0 - HALO + c
    in_seq = (pos_c >= 0) & (pos_c < seq)
    d = c - HALO - r
    pos_r = pos0 + lax.broadcasted_iota(jnp.int32, (tm, POOL_WIDTH), 0)
    grp = lax.broadcasted_iota(jnp.int32, (tm, POOL_WIDTH), 1) // (POOL_WIDTH // len(POOL_WINDOWS))
    mean = jnp.zeros((tm, POOL_WIDTH), F32)
    for gi, wdw in enumerate(POOL_WINDOWS):
        band = jnp.where(in_seq & (d >= -(wdw // 2)) & (d < wdw - wdw // 2), 1.0, 0.0).astype(BF16)
        lo = jnp.maximum(pos_r - wdw // 2, 0)
        hi = jnp.minimum(pos_r - wdw // 2 + wdw, seq)
        win_mean = _dot(band, a_ext) / (hi - lo).astype(F32)
        mean = jnp.where(grp == gi, win_mean, mean)
    pooled = (mean - a).astype(BF16)
    y_ref[...] = (_dot(pooled, w_ref[...]) * ps_ref[...]).astype(y_ref.dtype)


def _halo_specs(tm, width, n_rows):
    per = tm // HALO
    last = n_rows // HALO - 1
    prev = pl.BlockSpec((HALO, width), lambda i: (jnp.maximum(i * per - 1, 0), 0))
    cur = pl.BlockSpec((tm, width), lambda i: (i, 0))
    nxt = pl.BlockSpec((HALO, width), lambda i: (jnp.minimum((i + 1) * per, last), 0))
    return [prev, cur, nxt]


def _pool_mixer(a, w_bd, pscale, seq):
    rows = a.shape[0]
    tm = TM_POOL
    assert seq % tm == 0
    const = lambda i: (0, 0)
    return pl.pallas_call(
        functools.partial(_pool_kernel, seq=seq),
        out_shape=jax.ShapeDtypeStruct((rows, POOL_WIDTH), BF16),
        grid=(rows // tm,),
        in_specs=_halo_specs(tm, POOL_WIDTH, rows) + [
            pl.BlockSpec((POOL_WIDTH, POOL_WIDTH), const),
            pl.BlockSpec((1, POOL_WIDTH), const),
        ],
        out_specs=pl.BlockSpec((tm, POOL_WIDTH), lambda i: (i, 0)),
        compiler_params=_cparams(("parallel",)),
        name="pool_mixer",
    )(a, a, a, w_bd, pscale)


def _flash_kernel(q_ref, k_ref, v_ref, ck_ref, cv_ref, y_ref, m_ref, l_ref, acc_ref):
    tq = q_ref.shape[0]
    low = _low_half(tq)
    grp = B_HEADS // B_KV
    n_chunks = k_ref.shape[0] // TK_FLASH
    outs = []
    for j in range(2):
        q = _stack_heads(q_ref, list(range(j * grp, (j + 1) * grp)))
        m_ref[...] = jnp.full(m_ref.shape, NEG, F32)
        l_ref[...] = jnp.zeros(l_ref.shape, F32)
        acc_ref[...] = jnp.zeros(acc_ref.shape, F32)

        def step(kc, vc):
            s = _dot_nt(q, kc)
            m_prev = m_ref[...]
            m_new = jnp.maximum(m_prev, s.max(axis=-1, keepdims=True))
            alpha = jnp.exp(m_prev - m_new)
            p = jnp.exp(s - m_new)
            l_ref[...] = alpha * l_ref[...] + p.sum(axis=-1, keepdims=True)
            acc_ref[...] = alpha * acc_ref[...] + _dot(p.astype(BF16), vc)
            m_ref[...] = m_new

        def body(c, carry):
            off = pl.multiple_of(c * TK_FLASH, TK_FLASH)
            step(k_ref[pl.ds(off, TK_FLASH), :], v_ref[pl.ds(off, TK_FLASH), :])
            return carry

        lax.fori_loop(0, n_chunks, body, 0)
        step(ck_ref[...], cv_ref[...])
        o = acc_ref[...] / l_ref[...]
        outs += [o[g * tq:(g + 1) * tq] for g in range(grp)]
    _place_heads(y_ref, 0, outs, lambda hd: hd // grp, low)


def _flash_attn(q, k, v, ck, cv, n_batch):
    rows = q.shape[0]
    seq = rows // n_batch
    past = ck.shape[0] // n_batch
    tq = TQ_FLASH
    nq = seq // tq
    grp = B_HEADS // B_KV
    qw = 2 * grp * LANES
    yw = 2 * grp * HEAD_DIM
    return pl.pallas_call(
        _flash_kernel,
        out_shape=jax.ShapeDtypeStruct((rows, B_HEADS * HEAD_DIM), BF16),
        grid=(n_batch, B_KV // 2, nq),
        in_specs=[
            pl.BlockSpec((tq, qw), lambda b, p, i: (b * nq + i, p)),
            pl.BlockSpec((seq, LANES), lambda b, p, i: (b, p)),
            pl.BlockSpec((seq, LANES), lambda b, p, i: (b, p)),
            pl.BlockSpec((past, LANES), lambda b, p, i: (b, p)),
            pl.BlockSpec((past, LANES), lambda b, p, i: (b, p)),
        ],
        out_specs=pl.BlockSpec((tq, yw), lambda b, p, i: (b * nq + i, p)),
        scratch_shapes=[
            pltpu.VMEM((grp * tq, 1), F32),
            pltpu.VMEM((grp * tq, 1), F32),
            pltpu.VMEM((grp * tq, LANES), F32),
        ],
        compiler_params=_cparams(("parallel", "parallel", "arbitrary")),
        name="latent_flash_attn",
    )(q, k, v, ck, cv)


def _window_kernel(sink_ref, q_ref, kp_ref, kc_ref, kn_ref, vp_ref, vc_ref, vn_ref, ck_ref, cv_ref, y_ref,
                   *, n_blocks):
    tq = q_ref.shape[0]
    n = pl.program_id(1)
    low = _low_half(tq)
    grp = C_HEADS // C_KV
    rows = grp * tq
    r = lax.broadcasted_iota(jnp.int32, (rows, tq), 0) % tq
    c = lax.broadcasted_iota(jnp.int32, (rows, tq), 1)
    prev_ok = (c >= r) & (n > 0)
    next_ok = (c <= r) & (n < n_blocks - 1)
    outs = []
    for kv in range(C_KV):
        heads = list(range(kv * grp, (kv + 1) * grp))
        q = _stack_heads(q_ref, heads)
        sink = _sink_column(sink_ref, heads, tq)
        scores = [
            jnp.where(prev_ok, _dot_nt(q, kp_ref[...]), NEG),
            _dot_nt(q, kc_ref[...]),
            jnp.where(next_ok, _dot_nt(q, kn_ref[...]), NEG),
            _dot_nt(q, ck_ref[...]),
        ]
        o = _softmax_pv(scores, [vp_ref[...], vc_ref[...], vn_ref[...], cv_ref[...]], sink)
        outs += [o[g * tq:(g + 1) * tq] for g in range(grp)]
    _place_heads(y_ref, 0, outs, lambda hd: hd // grp, low)


def _window_attn(sink, q, k, v, ck, cv, n_batch):
    rows = q.shape[0]
    seq = rows // n_batch
    past = ck.shape[0] // n_batch
    tq = TQ_WIN
    nb = seq // tq
    prev = pl.BlockSpec((tq, LANES), lambda b, n: (b * nb + jnp.maximum(n - 1, 0), 0))
    cur = pl.BlockSpec((tq, LANES), lambda b, n: (b * nb + n, 0))
    nxt = pl.BlockSpec((tq, LANES), lambda b, n: (b * nb + jnp.minimum(n + 1, nb - 1), 0))
    ctx = pl.BlockSpec((past, LANES), lambda b, n: (b, 0))
    return pl.pallas_call(
        functools.partial(_window_kernel, n_blocks=nb),
        out_shape=jax.ShapeDtypeStruct((rows, C_HEADS * HEAD_DIM), BF16),
        grid=(n_batch, nb),
        in_specs=[
            pl.BlockSpec(memory_space=pltpu.SMEM),
            pl.BlockSpec((tq, C_HEADS * LANES), lambda b, n: (b * nb + n, 0)),
            prev, cur, nxt, prev, cur, nxt, ctx, ctx,
        ],
        out_specs=pl.BlockSpec((tq, C_HEADS * HEAD_DIM), lambda b, n: (b * nb + n, 0)),
        compiler_params=_cparams(("parallel", "parallel")),
        name="latent_window_attn",
    )(sink, q, k, k, k, v, v, v, ck, cv)


def _na_row_start(r, n_rows):
    return jnp.clip(r - NA_ROWS // 2, 0, n_rows - NA_ROWS)


def _na_kernel(q_ref, k_ref, v_ref, ck_ref, cv_ref, bias_ref, y_ref, *, n_rows):
    r = pl.program_id(1)
    low = _low_half(GRID_W)
    win = NA_ROWS * GRID_W
    start = pl.multiple_of(_na_row_start(r, n_rows) * GRID_W, GRID_W)
    outs = []
    for hd in range(D_HEADS):
        sl = slice((hd // 2) * LANES, (hd // 2 + 1) * LANES)
        q = q_ref[:, hd * LANES:(hd + 1) * LANES]
        kw = k_ref[pl.ds(start, win), sl]
        vw = v_ref[pl.ds(start, win), sl]
        s_loc = _dot_nt(q, kw) + bias_ref[hd]
        s_ctx = _dot_nt(q, ck_ref[:, sl])
        outs.append(_softmax_pv([s_loc, s_ctx], [vw, cv_ref[:, sl]]))
    _place_heads(y_ref, 0, outs, lambda hd: hd % 2, low)


def _na_attn(q, k, v, ck, cv, bias, n_batch):
    rows = q.shape[0]
    seq = rows // n_batch
    past = ck.shape[0] // n_batch
    n_rows = seq // GRID_W
    width = D_HEADS * HEAD_DIM
    win = NA_ROWS * GRID_W
    full = pl.BlockSpec((seq, width), lambda b, r: (b, 0))
    ctx = pl.BlockSpec((past, width), lambda b, r: (b, 0))
    return pl.pallas_call(
        functools.partial(_na_kernel, n_rows=n_rows),
        out_shape=jax.ShapeDtypeStruct((rows, width), BF16),
        grid=(n_batch, n_rows),
        in_specs=[
            pl.BlockSpec((GRID_W, D_HEADS * LANES), lambda b, r: (b * n_rows + r, 0)),
            full, full, ctx, ctx,
            pl.BlockSpec((None, D_HEADS, GRID_W, win),
                         lambda b, r: (r - _na_row_start(r, n_rows), 0, 0, 0)),
        ],
        out_specs=pl.BlockSpec((GRID_W, width), lambda b, r: (b * n_rows + r, 0)),
        compiler_params=_cparams(("parallel", "arbitrary")),
        name="latent_neighbourhood_attn",
    )(q, k, v, ck, cv, bias)


def _na_bias_table(rpb):
    dd = jnp.arange(NA_ROWS)[:, None, None, None]
    j = jnp.arange(NA_ROWS)[None, :, None, None]
    qc = jnp.arange(GRID_W)[None, None, :, None]
    kc = jnp.arange(GRID_W)[None, None, None, :]
    iy = jnp.broadcast_to(j - dd + NA_ROWS - 1, (NA_ROWS, NA_ROWS, GRID_W, GRID_W))
    cs = jnp.clip(qc - NA_COLS // 2, 0, GRID_W - NA_COLS)
    inside = jnp.broadcast_to((kc >= cs) & (kc < cs + NA_COLS), iy.shape)
    ix = jnp.broadcast_to(jnp.clip(kc - qc + NA_COLS - 1, 0, 2 * NA_COLS - 2), iy.shape)
    t = jnp.where(inside[None], rpb[:, iy, ix], NEG)
    t = jnp.transpose(t, (1, 0, 3, 2, 4))
    return t.reshape(NA_ROWS, D_HEADS, GRID_W, NA_ROWS * GRID_W)


def _out_proj_kernel(*refs, widths):
    x_ref, mod_ref, g_ref, w_ref = refs[:4]
    parts = refs[4:4 + len(widths)]
    o_ref = refs[4 + len(widths)]
    y = None
    off = 0
    for p_ref, wd in zip(parts, widths):
        t = _dot(p_ref[...], w_ref[off:off + wd, :])
        y = t if y is None else y + t
        off += wd
    o_ref[...] = x_ref[...] + mod_ref[2:3, :] * _rmsnorm(y, g_ref[...])


def _out_proj(x, mod, g, w, parts, seq_blocks):
    rows = x.shape[0]
    tm = TM_PROJ
    widths = tuple(p.shape[1] for p in parts)
    const = lambda i: (0, 0)
    return pl.pallas_call(
        functools.partial(_out_proj_kernel, widths=widths),
        out_shape=jax.ShapeDtypeStruct((rows, D_MODEL), F32),
        grid=(rows // tm,),
        in_specs=[
            pl.BlockSpec((tm, D_MODEL), lambda i: (i, 0)),
            pl.BlockSpec((None, 6, D_MODEL), _mod_index(seq_blocks)),
            pl.BlockSpec((1, D_MODEL), const),
            pl.BlockSpec((D_MODEL, D_MODEL), const),
        ] + [pl.BlockSpec((tm, wd), lambda i: (i, 0)) for wd in widths],
        out_specs=pl.BlockSpec((tm, D_MODEL), lambda i: (i, 0)),
        compiler_params=_cparams(("parallel",)),
        name="mixer_out_proj",
    )(x, mod, g, w, *parts)


def _ffn_kernel(xp_ref, x_ref, xn_ref, mod_ref, gpre_ref, gpost_ref, wg_ref, wv_ref, cg_ref, cv_ref, wo_ref,
                o_ref, h_ref, acc_ref, *, seq):
    tm = x_ref.shape[0]
    i = pl.program_id(0)
    scale, shift = mod_ref[4:5, :], mod_ref[3:4, :]
    g = gpre_ref[...]
    h_ref[0:HALO, :] = _norm_mod(xp_ref[...], g, scale, shift).astype(BF16)
    h_ref[HALO:HALO + tm, :] = _norm_mod(x_ref[...], g, scale, shift).astype(BF16)
    h_ref[HALO + tm:, :] = _norm_mod(xn_ref[...], g, scale, shift).astype(BF16)
    ext = tm + 2 * HALO
    pos = (i * tm + lax.broadcasted_iota(jnp.int32, (tm, 1), 0)) % seq
    has_prev = jnp.where(pos > 0, 1.0, 0.0)
    has_next = jnp.where(pos < seq - 1, 1.0, 0.0)

    def conv(u, wc):
        up = pltpu.roll(u, 1, 0)[HALO:HALO + tm]
        un = pltpu.roll(u, ext - 1, 0)[HALO:HALO + tm]
        return (u[HALO:HALO + tm] * wc[1:2, :] + (up * has_prev) * wc[0:1, :]
                + (un * has_next) * wc[2:3, :])

    def body(j, carry):
        hb = h_ref[...]
        gate = conv(_dot(hb, wg_ref[j]), cg_ref[j])
        val = conv(_dot(hb, wv_ref[j]), cv_ref[j])
        act = gate * (1.0 / (1.0 + jnp.exp(-gate))) * val
        contrib = _dot(act.astype(BF16), wo_ref[j])

        @pl.when(j == 0)
        def _():
            acc_ref[...] = contrib

        @pl.when(j > 0)
        def _():
            acc_ref[...] += contrib

        return carry

    lax.fori_loop(0, wg_ref.shape[0], body, 0)
    o_ref[...] = x_ref[...] + mod_ref[5:6, :] * _rmsnorm(acc_ref[...], gpost_ref[...])


def _conv_ffn(x, mod, g_pre, g_post, wg, wv, cg, cv, wo, seq, seq_blocks):
    rows = x.shape[0]
    tm = TM_FFN
    nch = wg.shape[0]
    const2 = lambda i: (0, 0)
    const3 = lambda i: (0, 0, 0)
    resident = dict(pipeline_mode=pl.Buffered(1))
    return pl.pallas_call(
        functools.partial(_ffn_kernel, seq=seq),
        out_shape=jax.ShapeDtypeStruct((rows, D_MODEL), F32),
        grid=(rows // tm,),
        in_specs=_halo_specs(tm, D_MODEL, rows) + [
            pl.BlockSpec((None, 6, D_MODEL), _mod_index(seq_blocks)),
            pl.BlockSpec((1, D_MODEL), const2),
            pl.BlockSpec((1, D_MODEL), const2),
            pl.BlockSpec((nch, D_MODEL, FF_CHUNK), const3, **resident),
            pl.BlockSpec((nch, D_MODEL, FF_CHUNK), const3, **resident),
            pl.BlockSpec((nch, 3, FF_CHUNK), const3),
            pl.BlockSpec((nch, 3, FF_CHUNK), const3),
            pl.BlockSpec((nch, FF_CHUNK, D_MODEL), const3, **resident),
        ],
        out_specs=pl.BlockSpec((tm, D_MODEL), lambda i: (i, 0)),
        scratch_shapes=[
            pltpu.VMEM((tm + 2 * HALO, D_MODEL), BF16),
            pltpu.VMEM((tm, D_MODEL), F32),
        ],
        compiler_params=_cparams(("parallel",)),
        name="conv_ffn",
    )(x, x, x, mod, g_pre, g_post, wg, wv, cg, cv, wo)


def _rope_tables(n_tokens):
    t = jnp.arange(n_tokens)
    row = (t // GRID_W).astype(F32)
    col = (t % GRID_W).astype(F32)
    axis_dim = HEAD_DIM // 2
    inv_freq = ROPE_THETA ** (-jnp.arange(0, axis_dim, 2, dtype=F32) / axis_dim)
    ang = jnp.concatenate([row[:, None] * inv_freq, col[:, None] * inv_freq], axis=-1)
    cos = jnp.repeat(jnp.cos(ang), 2, axis=-1)
    sin = jnp.sin(ang)
    sin = jnp.stack([-sin, sin], axis=-1).reshape(n_tokens, HEAD_DIM)
    return jnp.tile(cos, (1, 2)), jnp.tile(sin, (1, 2))


def _chunk_cols(w, n):
    return jnp.transpose(w.reshape(w.shape[0], n, FF_CHUNK), (1, 0, 2))


def kernel(x_prompt, x_sample, cache_b_k, cache_b_v, cache_c_k, cache_c_v, cache_d_k, cache_d_v, c, c_ctx,
           w_mod, b_mod, g_mix_pre, g_mix_post, g_ffn_pre, g_ffn_post, w_in_even, w_pool, pool_scale,
           g_q_b, g_k_b, w_in_odd, sink_c, rpb_d, w_mix_out, w_ffn_in, w_ffn_conv, w_ffn_out):
    n_ctx, ctx_seq, _ = x_prompt.shape
    n_lat, lat_seq, _ = x_sample.shape
    past = cache_b_k.shape[2]
    xp = x_prompt.reshape(n_ctx * ctx_seq, D_MODEL)
    xs = x_sample.reshape(n_lat * lat_seq, D_MODEL)
    lat_blocks = lat_seq // TM_PROJ

    cvecs = jnp.concatenate([c_ctx[None, :], c, jnp.zeros((8 - 1 - n_lat, D_MODEL), F32)], axis=0)
    mods = _modulation(cvecs, w_mod, b_mod)[:, :1 + n_lat].reshape(DEPTH, 1 + n_lat, 6, D_MODEL)

    rope_tabs = _rope_tables(lat_seq)
    blk = jnp.arange(LANES) // HEAD_DIM
    e_mat = jnp.where(blk[:, None] == blk[None, :], 1.0 / HEAD_DIM, 0.0).astype(BF16)
    n_ff = D_FF // FF_CHUNK

    b_k, b_v, c_k, c_v, d_k, d_v = [], [], [], [], [], []
    for l in range(DEPTH):
        i = l // 2
        mod = mods[l]
        g_pre = g_mix_pre[l][None, :]
        if l % 2 == 0:
            w_in = w_in_even[i].astype(BF16)
            gq = jnp.tile(g_q_b[i], 2)[None, :]
            gk = jnp.tile(g_k_b[i], 2)[None, :]
            kvw = B_KV * HEAD_DIM
            widths = (POOL_WIDTH, B_HEADS * LANES, kvw, kvw)
            a_p, q_p, k_p, v_p = _proj_in(xp, mod, g_pre, w_in, (gq, gk, e_mat), widths,
                                          (F32, BF16, F32, F32), _proj_even_kernel, None, None,
                                          "proj_even_ctx")
            a_s, q_s, k_s, v_s = _proj_in(xs, mod, g_pre, w_in, (gq, gk, e_mat), widths,
                                          (F32, BF16, BF16, BF16), _proj_even_kernel, lat_blocks, rope_tabs,
                                          "proj_even_lat")
            b_k.append(k_p.reshape(n_ctx, ctx_seq, B_KV, HEAD_DIM))
            b_v.append(v_p.reshape(n_ctx, ctx_seq, B_KV, HEAD_DIM))
            w_bd = jax.scipy.linalg.block_diag(*[w_pool[i, g] for g in range(len(POOL_WINDOWS))]).astype(BF16)
            pscale = pool_scale[i][None, :]
            ya_p = _pool_mixer(a_p, w_bd, pscale, ctx_seq)
            ya_s = _pool_mixer(a_s, w_bd, pscale, lat_seq)
            yb_p = _ctx_even_attn(q_p, k_p, v_p, ctx_seq)
            ck = cache_b_k[:, i].reshape(n_lat * past, kvw).astype(BF16)
            cv = cache_b_v[:, i].reshape(n_lat * past, kvw).astype(BF16)
            yb_s = _flash_attn(q_s, k_s, v_s, ck, cv, n_lat)
            parts_p, parts_s = (ya_p, yb_p), (ya_s, yb_s)
        else:
            w_in = w_in_odd[i].astype(BF16)
            ckw, dw = C_KV * HEAD_DIM, D_HEADS * HEAD_DIM
            widths = (C_HEADS * LANES, ckw, ckw, D_HEADS * LANES, dw, dw)
            qc_p, kc_p, vc_p, qd_p, kd_p, vd_p = _proj_in(
                xp, mod, g_pre, w_in, (), widths, (BF16, F32, F32, BF16, F32, F32),
                _proj_odd_kernel, None, None, "proj_odd_ctx")
            qc_s, kc_s, vc_s, qd_s, kd_s, vd_s = _proj_in(
                xs, mod, g_pre, w_in, (), widths, (BF16,) * 6,
                _proj_odd_kernel, lat_blocks, rope_tabs, "proj_odd_lat")
            c_k.append(kc_p.reshape(n_ctx, ctx_seq, C_KV, HEAD_DIM))
            c_v.append(vc_p.reshape(n_ctx, ctx_seq, C_KV, HEAD_DIM))
            d_k.append(kd_p.reshape(n_ctx, ctx_seq, D_HEADS, HEAD_DIM))
            d_v.append(vd_p.reshape(n_ctx, ctx_seq, D_HEADS, HEAD_DIM))
            sink = sink_c[i]
            y_p = _ctx_odd_attn(sink, qc_p, kc_p, vc_p, qd_p, kd_p, vd_p, ctx_seq)
            cck = cache_c_k[:, i].reshape(n_lat * past, ckw).astype(BF16)
            ccv = cache_c_v[:, i].reshape(n_lat * past, ckw).astype(BF16)
            cdk = cache_d_k[:, i].reshape(n_lat * past, dw).astype(BF16)
            cdv = cache_d_v[:, i].reshape(n_lat * past, dw).astype(BF16)
            yc_s = _window_attn(sink, qc_s, kc_s, vc_s, cck, ccv, n_lat)
            yd_s = _na_attn(qd_s, kd_s, vd_s, cdk, cdv, _na_bias_table(rpb_d[i]), n_lat)
            parts_p, parts_s = (y_p,), (yc_s, yd_s)

        w_out = w_mix_out[l].astype(BF16)
        g_post = g_mix_post[l][None, :]
        xp = _out_proj(xp, mod, g_post, w_out, parts_p, None)
        xs = _out_proj(xs, mod, g_post, w_out, parts_s, lat_blocks)

        wg = _chunk_cols(w_ffn_in[l][:, :D_FF], n_ff).astype(BF16)
        wv = _chunk_cols(w_ffn_in[l][:, D_FF:], n_ff).astype(BF16)
        cg = _chunk_cols(w_ffn_conv[l][:, :D_FF], n_ff)
        cvw = _chunk_cols(w_ffn_conv[l][:, D_FF:], n_ff)
        wo = w_ffn_out[l].reshape(n_ff, FF_CHUNK, D_MODEL).astype(BF16)
        gf_pre, gf_post = g_ffn_pre[l][None, :], g_ffn_post[l][None, :]
        xp = _conv_ffn(xp, mod, gf_pre, gf_post, wg, wv, cg, cvw, wo, ctx_seq, None)
        xs = _conv_ffn(xs, mod, gf_pre, gf_post, wg, wv, cg, cvw, wo, lat_seq, lat_seq // TM_FFN)

    return (xp.reshape(n_ctx, ctx_seq, D_MODEL), xs.reshape(n_lat, lat_seq, D_MODEL),
            jnp.stack(b_k, axis=1), jnp.stack(b_v, axis=1),
            jnp.stack(c_k, axis=1), jnp.stack(c_v, axis=1),
            jnp.stack(d_k, axis=1), jnp.stack(d_v, axis=1))
```

```python
import functools

import jax
import jax.numpy as jnp
import numpy as np
from jax import lax
from jax.experimental import pallas as pl
from jax.experimental.pallas import tpu as pltpu

F32 = jnp.float32
BF16 = jnp.bfloat16

D_MODEL = 1024
DEPTH = 4
GRID_W = 64
HEAD_DIM = 64
EPS = 1e-6
ROPE_THETA = 10000.0
POOL_WINDOWS = (2, 4, 8, 16)
POOL_WIDTH = 256
B_HEADS, B_KV = 12, 4
C_HEADS, C_KV = 8, 2
D_HEADS = 8
C_WINDOW = 128
NA_ROWS, NA_COLS = 8, 16
D_FF = 2816
QK_SCALE = HEAD_DIM ** -0.5
NEG = -1e30

LANES = 128
HALO = 16
VMEM_LIMIT = 56 * 1024 * 1024

TM_PROJ = 512
TM_FFN = 512
TM_POOL = 256
FF_CHUNK = 1408
TQ_FLASH = 256
TK_FLASH = 512
TQ_WIN = 128


def _cparams(sem):
    return pltpu.CompilerParams(dimension_semantics=sem, vmem_limit_bytes=VMEM_LIMIT)


def _norm_mod(x, g, scale, shift):
    ms = jnp.mean(x * x, axis=-1, keepdims=True)
    return (x * lax.rsqrt(ms + EPS) * g) * (1.0 + scale) + shift


def _rmsnorm(x, g):
    ms = jnp.mean(x * x, axis=-1, keepdims=True)
    return x * lax.rsqrt(ms + EPS) * g


def _dot(a, b):
    return jnp.dot(a, b, preferred_element_type=F32)


def _dot_nt(a, b):
    return lax.dot_general(a, b, (((1,), (1,)), ((), ())), preferred_element_type=F32)


def _low_half(rows):
    return lax.broadcasted_iota(jnp.int32, (rows, LANES), 1) < HEAD_DIM


def _mod_kernel(c_ref, w_ref, b_ref, o_ref):
    cv = c_ref[...]
    s = cv * (1.0 / (1.0 + jnp.exp(-cv)))
    o_ref[...] = _dot(s.astype(BF16), w_ref[...].astype(BF16)) + b_ref[...]


def _modulation(cvecs, w_mod, b_mod):
    tn = 1536
    return pl.pallas_call(
        _mod_kernel,
        out_shape=jax.ShapeDtypeStruct((DEPTH, 8, 6 * D_MODEL), F32),
        grid=(DEPTH, 6 * D_MODEL // tn),
        in_specs=[
            pl.BlockSpec((8, D_MODEL), lambda l, j: (0, 0)),
            pl.BlockSpec((None, D_MODEL, tn), lambda l, j: (l, 0, j)),
            pl.BlockSpec((None, 1, tn), lambda l, j: (l, 0, j)),
        ],
        out_specs=pl.BlockSpec((None, 8, tn), lambda l, j: (l, 0, j)),
        compiler_params=_cparams(("parallel", "parallel")),
        name="modulation",
    )(cvecs, w_mod, b_mod.reshape(DEPTH, 1, 6 * D_MODEL))


def _pair_swap(s):
    lane = lax.broadcasted_iota(jnp.int32, s.shape, 1)
    n = s.shape[1]
    return jnp.where(lane % 2 == 0, pltpu.roll(s, n - 1, 1), pltpu.roll(s, 1, 1))


def _store_expanded_q(q_ref, s, slab, kv_half_of, low):
    sr = pltpu.roll(s, HEAD_DIM, 1)
    for half in (0, 1):
        hd = 2 * slab + half
        dst = kv_half_of(hd)
        val = s if dst == half else sr
        keep = low if dst == 0 else jnp.logical_not(low)
        q_ref[:, hd * LANES:(hd + 1) * LANES] = jnp.where(keep, val, 0.0).astype(BF16)


def _proj_even_kernel(*refs, rope):
    if rope:
        (x_ref, mod_ref, g_ref, w_ref, gq_ref, gk_ref, e_ref, cos_ref, sin_ref,
         a_ref, q_ref, k_ref, v_ref) = refs
    else:
        (x_ref, mod_ref, g_ref, w_ref, gq_ref, gk_ref, e_ref,
         a_ref, q_ref, k_ref, v_ref) = refs
    rows = x_ref.shape[0]
    h = _norm_mod(x_ref[...], g_ref[...], mod_ref[1:2, :], mod_ref[0:1, :])
    u = _dot(h.astype(BF16), w_ref[...])
    low = _low_half(rows)
    a_ref[...] = u[:, :POOL_WIDTH]

    def headnorm(s, g):
        ms = _dot((s * s).astype(BF16), e_ref[...])
        return s * lax.rsqrt(ms + EPS) * g

    def rot(s):
        if not rope:
            return s
        return s * cos_ref[...] + _pair_swap(s) * sin_ref[...]

    q0 = POOL_WIDTH
    for j in range(B_HEADS // 2):
        s = u[:, q0 + LANES * j:q0 + LANES * (j + 1)]
        s = rot(headnorm(s, gq_ref[...])) * QK_SCALE
        _store_expanded_q(q_ref, s, j, lambda hd: (hd // (B_HEADS // B_KV)) % 2, low)
    k0 = q0 + B_HEADS * HEAD_DIM
    for j in range(B_KV // 2):
        s = headnorm(u[:, k0 + LANES * j:k0 + LANES * (j + 1)], gk_ref[...])
        k_ref[:, LANES * j:LANES * (j + 1)] = rot(s).astype(k_ref.dtype)
    v0 = k0 + B_KV * HEAD_DIM
    v_ref[...] = u[:, v0:v0 + B_KV * HEAD_DIM].astype(v_ref.dtype)


def _proj_odd_kernel(*refs, rope):
    if rope:
        (x_ref, mod_ref, g_ref, w_ref, cos_ref, sin_ref,
         qc_ref, kc_ref, vc_ref, qd_ref, kd_ref, vd_ref) = refs
    else:
        (x_ref, mod_ref, g_ref, w_ref,
         qc_ref, kc_ref, vc_ref, qd_ref, kd_ref, vd_ref) = refs
    rows = x_ref.shape[0]
    h = _norm_mod(x_ref[...], g_ref[...], mod_ref[1:2, :], mod_ref[0:1, :])
    u = _dot(h.astype(BF16), w_ref[...])
    low = _low_half(rows)

    def rot(s):
        if not rope:
            return s
        return s * cos_ref[...] + _pair_swap(s) * sin_ref[...]

    for j in range(C_HEADS // 2):
        s = rot(u[:, LANES * j:LANES * (j + 1)]) * QK_SCALE
        _store_expanded_q(qc_ref, s, j, lambda hd: hd // (C_HEADS // C_KV), low)
    o = C_HEADS * HEAD_DIM
    kc_ref[...] = rot(u[:, o:o + LANES]).astype(kc_ref.dtype)
    o += C_KV * HEAD_DIM
    vc_ref[...] = u[:, o:o + LANES].astype(vc_ref.dtype)
    o += C_KV * HEAD_DIM
    for j in range(D_HEADS // 2):
        s = u[:, o + LANES * j:o + LANES * (j + 1)] * QK_SCALE
        _store_expanded_q(qd_ref, s, j, lambda hd: hd % 2, low)
    o += D_HEADS * HEAD_DIM
    kd_ref[...] = u[:, o:o + D_HEADS * HEAD_DIM].astype(kd_ref.dtype)
    o += D_HEADS * HEAD_DIM
    vd_ref[...] = u[:, o:o + D_HEADS * HEAD_DIM].astype(vd_ref.dtype)


def _mod_index(seq_blocks):
    if seq_blocks is None:
        return lambda i: (0, 0, 0)
    return lambda i: (1 + i // seq_blocks, 0, 0)


def _proj_in(x, mod, g, w, extras, out_widths, out_dtypes, kernel, seq_blocks, rope_tabs, name):
    rows = x.shape[0]
    tm = TM_PROJ
    n_in = w.shape[1]
    const = lambda i: (0, 0)
    in_specs = [
        pl.BlockSpec((tm, D_MODEL), lambda i: (i, 0)),
        pl.BlockSpec((None, 6, D_MODEL), _mod_index(seq_blocks)),
        pl.BlockSpec((1, D_MODEL), const),
        pl.BlockSpec((D_MODEL, n_in), const),
    ]
    args = [x, mod, g, w]
    for e in extras:
        in_specs.append(pl.BlockSpec(e.shape, const))
        args.append(e)
    if rope_tabs is not None:
        for t in rope_tabs:
            in_specs.append(pl.BlockSpec((tm, LANES), lambda i: (i % seq_blocks, 0)))
            args.append(t)
    return pl.pallas_call(
        functools.partial(kernel, rope=rope_tabs is not None),
        out_shape=[jax.ShapeDtypeStruct((rows, wd), dt) for wd, dt in zip(out_widths, out_dtypes)],
        grid=(rows // tm,),
        in_specs=in_specs,
        out_specs=[pl.BlockSpec((tm, wd), lambda i: (i, 0)) for wd in out_widths],
        compiler_params=_cparams(("parallel",)),
        name=name,
    )(*args)


def _place_heads(out_ref, col0, heads, kv_half_of, low):
    for m in range(len(heads) // 2):
        a = heads[2 * m]
        if kv_half_of(2 * m) == 1:
            a = pltpu.roll(a, HEAD_DIM, 1)
        b = heads[2 * m + 1]
        if kv_half_of(2 * m + 1) == 0:
            b = pltpu.roll(b, HEAD_DIM, 1)
        out_ref[:, col0 + m * LANES:col0 + (m + 1) * LANES] = jnp.where(low, a, b).astype(out_ref.dtype)


def _stack_heads(q_ref, heads):
    return jnp.concatenate([q_ref[:, h * LANES:(h + 1) * LANES] for h in heads], axis=0)


def _softmax_pv(scores, values, sink=None):
    m = scores[0].max(axis=-1, keepdims=True)
    for s in scores[1:]:
        m = jnp.maximum(m, s.max(axis=-1, keepdims=True))
    if sink is not None:
        m = jnp.maximum(m, sink)
    den = None
    acc = None
    for s, v in zip(scores, values):
        e = jnp.exp(s - m)
        d = e.sum(axis=-1, keepdims=True)
        o = _dot(e.astype(BF16), v)
        den = d if den is None else den + d
        acc = o if acc is None else acc + o
    if sink is not None:
        den = den + jnp.exp(sink - m)
    return acc / den


def _sink_column(sink_ref, heads, rows_per_head):
    row = lax.broadcasted_iota(jnp.int32, (len(heads) * rows_per_head, 1), 0)
    col = jnp.full(row.shape, sink_ref[heads[-1]], F32)
    for g in range(len(heads) - 2, -1, -1):
        col = jnp.where(row < (g + 1) * rows_per_head, sink_ref[heads[g]], col)
    return col


def _ctx_even_attn_kernel(q_ref, k_ref, v_ref, y_ref):
    rows = q_ref.shape[0]
    low = _low_half(rows)
    grp = B_HEADS // B_KV
    outs = []
    for kv in range(B_KV):
        sl = slice((kv // 2) * LANES, (kv // 2 + 1) * LANES)
        ks = k_ref[:, sl].astype(BF16)
        vs = v_ref[:, sl].astype(BF16)
        heads = list(range(kv * grp, (kv + 1) * grp))
        o = _softmax_pv([_dot_nt(_stack_heads(q_ref, heads), ks)], [vs])
        outs += [o[g * rows:(g + 1) * rows] for g in range(grp)]
    _place_heads(y_ref, 0, outs, lambda hd: (hd // grp) % 2, low)


def _ctx_odd_attn_kernel(sink_ref, qc_ref, kc_ref, vc_ref, qd_ref, kd_ref, vd_ref, y_ref):
    rows = qc_ref.shape[0]
    low = _low_half(rows)
    grp = C_HEADS // C_KV
    ks = kc_ref[...].astype(BF16)
    vs = vc_ref[...].astype(BF16)
    outs = []
    for kv in range(C_KV):
        heads = list(range(kv * grp, (kv + 1) * grp))
        sink = _sink_column(sink_ref, heads, rows)
        o = _softmax_pv([_dot_nt(_stack_heads(qc_ref, heads), ks)], [vs], sink)
        outs += [o[g * rows:(g + 1) * rows] for g in range(grp)]
    _place_heads(y_ref, 0, outs, lambda hd: hd // grp, low)
    outs = []
    for hd in range(D_HEADS):
        sl = slice((hd // 2) * LANES, (hd // 2 + 1) * LANES)
        s = _dot_nt(qd_ref[:, hd * LANES:(hd + 1) * LANES], kd_ref[:, sl].astype(BF16))
        outs.append(_softmax_pv([s], [vd_ref[:, sl].astype(BF16)]))
    _place_heads(y_ref, C_HEADS * HEAD_DIM, outs, lambda hd: hd % 2, low)


def _ctx_even_attn(q, k, v, seq):
    rows = q.shape[0]
    blk = lambda wd: pl.BlockSpec((seq, wd), lambda b: (b, 0))
    return pl.pallas_call(
        _ctx_even_attn_kernel,
        out_shape=jax.ShapeDtypeStruct((rows, B_HEADS * HEAD_DIM), BF16),
        grid=(rows // seq,),
        in_specs=[blk(q.shape[1]), blk(k.shape[1]), blk(v.shape[1])],
        out_specs=blk(B_HEADS * HEAD_DIM),
        compiler_params=_cparams(("parallel",)),
        name="ctx_even_attn",
    )(q, k, v)


def _ctx_odd_attn(sink, qc, kc, vc, qd, kd, vd, seq):
    rows = qc.shape[0]
    blk = lambda a: pl.BlockSpec((seq, a.shape[1]), lambda b: (b, 0))
    return pl.pallas_call(
        _ctx_odd_attn_kernel,
        out_shape=jax.ShapeDtypeStruct((rows, D_MODEL), BF16),
        grid=(rows // seq,),
        in_specs=[pl.BlockSpec(memory_space=pltpu.SMEM)] + [blk(a) for a in (qc, kc, vc, qd, kd, vd)],
        out_specs=pl.BlockSpec((seq, D_MODEL), lambda b: (b, 0)),
        compiler_params=_cparams(("parallel",)),
        name="ctx_odd_attn",
    )(sink, qc, kc, vc, qd, kd, vd)


def _pool_kernel(ap_ref, a_ref, an_ref, w_ref, ps_ref, y_ref, *, seq):
    tm = a_ref.shape[0]
    i = pl.program_id(0)
    pos0 = (i * tm) % seq
    a = a_ref[...]
    a_ext = jnp.concatenate([ap_ref[...], a, an_ref[...]], axis=0).astype(BF16)
    ext = tm + 2 * HALO
    r = lax.broadcasted_iota(jnp.int32, (tm, ext), 0)
    c = lax.broadcasted_iota(jnp.int32, (tm, ext), 1)
    pos_c = pos0 - HALO + c
    in_seq = (pos_c >= 0) & (pos_c < seq)
    d = c - HALO - r
    pos_r = pos0 + lax.broadcasted_iota(jnp.int32, (tm, POOL_WIDTH), 0)
    grp = lax.broadcasted_iota(jnp.int32, (tm, POOL_WIDTH), 1) // (POOL_WIDTH // len(POOL_WINDOWS))
    mean = jnp.zeros((tm, POOL_WIDTH), F32)
    for gi, wdw in enumerate(POOL_WINDOWS):
        band = jnp.where(in_seq & (d >= -(wdw // 2)) & (d < wdw - wdw // 2), 1.0, 0.0).astype(BF16)
        lo = jnp.maximum(pos_r - wdw // 2, 0)
        hi = jnp.minimum(pos_r - wdw // 2 + wdw, seq)
        win_mean = _dot(band, a_ext) / (hi - lo).astype(F32)
        mean = jnp.where(grp == gi, win_mean, mean)
    pooled = (mean - a).astype(BF16)
    y_ref[...] = (_dot(pooled, w_ref[...]) * ps_ref[...]).astype(y_ref.dtype)


def _halo_specs(tm, width, n_rows):
    per = tm // HALO
    last = n_rows // HALO - 1
    prev = pl.BlockSpec((HALO, width), lambda i: (jnp.maximum(i * per - 1, 0), 0))
    cur = pl.BlockSpec((tm, width), lambda i: (i, 0))
    nxt = pl.BlockSpec((HALO, width), lambda i: (jnp.minimum((i + 1) * per, last), 0))
    return [prev, cur, nxt]


def _pool_mixer(a, w_bd, pscale, seq):
    rows = a.shape[0]
    tm = TM_POOL
    assert seq % tm == 0
    const = lambda i: (0, 0)
    return pl.pallas_call(
        functools.partial(_pool_kernel, seq=seq),
        out_shape=jax.ShapeDtypeStruct((rows, POOL_WIDTH), BF16),
        grid=(rows // tm,),
        in_specs=_halo_specs(tm, POOL_WIDTH, rows) + [
            pl.BlockSpec((POOL_WIDTH, POOL_WIDTH), const),
            pl.BlockSpec((1, POOL_WIDTH), const),
        ],
        out_specs=pl.BlockSpec((tm, POOL_WIDTH), lambda i: (i, 0)),
        compiler_params=_cparams(("parallel",)),
        name="pool_mixer",
    )(a, a, a, w_bd, pscale)


def _flash_kernel(q_ref, k_ref, v_ref, ck_ref, cv_ref, y_ref, m_ref, acc_ref):
    tq = q_ref.shape[0]
    low = _low_half(tq)
    grp = B_HEADS // B_KV
    n_chunks = k_ref.shape[0] // TK_FLASH
    outs = []
    for j in range(2):
        q = _stack_heads(q_ref, list(range(j * grp, (j + 1) * grp)))
        m_ref[...] = jnp.full(m_ref.shape, NEG, F32)
        acc_ref[...] = jnp.zeros(acc_ref.shape, F32)

        def step(kc, vc):
            own = _low_half(vc.shape[0]) == (j == 0)
            vj = jnp.where(own, vc, jnp.ones_like(vc))
            s = _dot_nt(q, kc)
            m_prev = m_ref[...]
            m_new = jnp.maximum(m_prev, s.max(axis=-1, keepdims=True))
            p = jnp.exp(s - pltpu.repeat(m_new, s.shape[1] // LANES, 1))
            acc_ref[...] = jnp.exp(m_prev - m_new) * acc_ref[...] + _dot(p.astype(BF16), vj)
            m_ref[...] = m_new

        def body(c, carry):
            off = pl.multiple_of(c * TK_FLASH, TK_FLASH)
            step(k_ref[pl.ds(off, TK_FLASH), :], v_ref[pl.ds(off, TK_FLASH), :])
            return carry

        lax.fori_loop(0, n_chunks, body, 0)
        step(ck_ref[...], cv_ref[...])
        acc = acc_ref[...]
        o = acc / pltpu.roll(acc, HEAD_DIM, 1)
        outs += [o[g * tq:(g + 1) * tq] for g in range(grp)]
    _place_heads(y_ref, 0, outs, lambda hd: hd // grp, low)


def _flash_attn(q, k, v, ck, cv, n_batch):
    rows = q.shape[0]
    seq = rows // n_batch
    past = ck.shape[0] // n_batch
    tq = TQ_FLASH
    nq = seq // tq
    grp = B_HEADS // B_KV
    qw = 2 * grp * LANES
    yw = 2 * grp * HEAD_DIM
    return pl.pallas_call(
        _flash_kernel,
        out_shape=jax.ShapeDtypeStruct((rows, B_HEADS * HEAD_DIM), BF16),
        grid=(n_batch, B_KV // 2, nq),
        in_specs=[
            pl.BlockSpec((tq, qw), lambda b, p, i: (b * nq + i, p)),
            pl.BlockSpec((seq, LANES), lambda b, p, i: (b, p)),
            pl.BlockSpec((seq, LANES), lambda b, p, i: (b, p)),
            pl.BlockSpec((past, LANES), lambda b, p, i: (b, p)),
            pl.BlockSpec((past, LANES), lambda b, p, i: (b, p)),
        ],
        out_specs=pl.BlockSpec((tq, yw), lambda b, p, i: (b * nq + i, p)),
        scratch_shapes=[
            pltpu.VMEM((grp * tq, LANES), F32),
            pltpu.VMEM((grp * tq, LANES), F32),
        ],
        compiler_params=_cparams(("parallel", "parallel", "arbitrary")),
        name="latent_flash_attn",
    )(q, k, v, ck, cv)


def _window_kernel(sink_ref, q_ref, kp_ref, kc_ref, kn_ref, vp_ref, vc_ref, vn_ref, ck_ref, cv_ref, y_ref,
                   *, n_blocks):
    tq = q_ref.shape[0]
    n = pl.program_id(1)
    low = _low_half(tq)
    grp = C_HEADS // C_KV
    rows = grp * tq
    r = lax.broadcasted_iota(jnp.int32, (rows, tq), 0) % tq
    c = lax.broadcasted_iota(jnp.int32, (rows, tq), 1)
    prev_ok = (c >= r) & (n > 0)
    next_ok = (c <= r) & (n < n_blocks - 1)
    outs = []
    for kv in range(C_KV):
        heads = list(range(kv * grp, (kv + 1) * grp))
        q = _stack_heads(q_ref, heads)
        sink = _sink_column(sink_ref, heads, tq)
        scores = [
            jnp.where(prev_ok, _dot_nt(q, kp_ref[...]), NEG),
            _dot_nt(q, kc_ref[...]),
            jnp.where(next_ok, _dot_nt(q, kn_ref[...]), NEG),
            _dot_nt(q, ck_ref[...]),
        ]
        o = _softmax_pv(scores, [vp_ref[...], vc_ref[...], vn_ref[...], cv_ref[...]], sink)
        outs += [o[g * tq:(g + 1) * tq] for g in range(grp)]
    _place_heads(y_ref, 0, outs, lambda hd: hd // grp, low)


def _window_attn(sink, q, k, v, ck, cv, n_batch):
    rows = q.shape[0]
    seq = rows // n_batch
    past = ck.shape[0] // n_batch
    tq = TQ_WIN
    nb = seq // tq
    prev = pl.BlockSpec((tq, LANES), lambda b, n: (b * nb + jnp.maximum(n - 1, 0), 0))
    cur = pl.BlockSpec((tq, LANES), lambda b, n: (b * nb + n, 0))
    nxt = pl.BlockSpec((tq, LANES), lambda b, n: (b * nb + jnp.minimum(n + 1, nb - 1), 0))
    ctx = pl.BlockSpec((past, LANES), lambda b, n: (b, 0))
    return pl.pallas_call(
        functools.partial(_window_kernel, n_blocks=nb),
        out_shape=jax.ShapeDtypeStruct((rows, C_HEADS * HEAD_DIM), BF16),
        grid=(n_batch, nb),
        in_specs=[
            pl.BlockSpec(memory_space=pltpu.SMEM),
            pl.BlockSpec((tq, C_HEADS * LANES), lambda b, n: (b * nb + n, 0)),
            prev, cur, nxt, prev, cur, nxt, ctx, ctx,
        ],
        out_specs=pl.BlockSpec((tq, C_HEADS * HEAD_DIM), lambda b, n: (b * nb + n, 0)),
        compiler_params=_cparams(("parallel", "parallel")),
        name="latent_window_attn",
    )(sink, q, k, k, k, v, v, v, ck, cv)


def _na_row_start(r, n_rows):
    return jnp.clip(r - NA_ROWS // 2, 0, n_rows - NA_ROWS)


def _na_kernel(q_ref, k_ref, v_ref, ck_ref, cv_ref, bias_ref, y_ref, *, n_rows):
    r = pl.program_id(1)
    low = _low_half(GRID_W)
    win = NA_ROWS * GRID_W
    start = pl.multiple_of(_na_row_start(r, n_rows) * GRID_W, GRID_W)
    outs = []
    for hd in range(D_HEADS):
        sl = slice((hd // 2) * LANES, (hd // 2 + 1) * LANES)
        q = q_ref[:, hd * LANES:(hd + 1) * LANES]
        kw = k_ref[pl.ds(start, win), sl]
        vw = v_ref[pl.ds(start, win), sl]
        s_loc = _dot_nt(q, kw) + bias_ref[hd]
        s_ctx = _dot_nt(q, ck_ref[:, sl])
        outs.append(_softmax_pv([s_loc, s_ctx], [vw, cv_ref[:, sl]]))
    _place_heads(y_ref, 0, outs, lambda hd: hd % 2, low)


def _na_attn(q, k, v, ck, cv, bias, n_batch):
    rows = q.shape[0]
    seq = rows // n_batch
    past = ck.shape[0] // n_batch
    n_rows = seq // GRID_W
    width = D_HEADS * HEAD_DIM
    win = NA_ROWS * GRID_W
    full = pl.BlockSpec((seq, width), lambda b, r: (b, 0))
    ctx = pl.BlockSpec((past, width), lambda b, r: (b, 0))
    return pl.pallas_call(
        functools.partial(_na_kernel, n_rows=n_rows),
        out_shape=jax.ShapeDtypeStruct((rows, width), BF16),
        grid=(n_batch, n_rows),
        in_specs=[
            pl.BlockSpec((GRID_W, D_HEADS * LANES), lambda b, r: (b * n_rows + r, 0)),
            full, full, ctx, ctx,
            pl.BlockSpec((None, D_HEADS, GRID_W, win),
                         lambda b, r: (r - _na_row_start(r, n_rows), 0, 0, 0)),
        ],
        out_specs=pl.BlockSpec((GRID_W, width), lambda b, r: (b * n_rows + r, 0)),
        compiler_params=_cparams(("parallel", "arbitrary")),
        name="latent_neighbourhood_attn",
    )(q, k, v, ck, cv, bias)


def _na_bias_table(rpb):
    n_h, n_y, n_x = rpb.shape
    period = n_x + GRID_W
    padded = jnp.concatenate([rpb, jnp.full((n_h, n_y, GRID_W), NEG, F32)], axis=-1)
    skew = jnp.tile(padded, (1, 1, GRID_W))[:, :, :GRID_W * (period - 1)]
    skew = skew.reshape(n_h, n_y, GRID_W, period - 1)[..., NA_COLS - 1:NA_COLS - 1 + GRID_W]
    qc = np.arange(GRID_W)[:, None]
    kc = np.arange(GRID_W)[None, :]
    cs = np.clip(qc - NA_COLS // 2, 0, GRID_W - NA_COLS)
    inside = (kc >= cs) & (kc < cs + NA_COLS)
    skew = jnp.where(inside, skew, NEG)
    tabs = []
    for dd in range(NA_ROWS):
        t = skew[:, NA_ROWS - 1 - dd:2 * NA_ROWS - 1 - dd]
        tabs.append(jnp.transpose(t, (0, 2, 1, 3)).reshape(n_h, GRID_W, NA_ROWS * GRID_W))
    return jnp.stack(tabs, axis=0)


def _out_proj_kernel(*refs, widths):
    x_ref, mod_ref, g_ref, w_ref = refs[:4]
    parts = refs[4:4 + len(widths)]
    o_ref = refs[4 + len(widths)]
    y = None
    off = 0
    for p_ref, wd in zip(parts, widths):
        t = _dot(p_ref[...], w_ref[off:off + wd, :])
        y = t if y is None else y + t
        off += wd
    o_ref[...] = x_ref[...] + mod_ref[2:3, :] * _rmsnorm(y, g_ref[...])


def _out_proj(x, mod, g, w, parts, seq_blocks):
    rows = x.shape[0]
    tm = TM_PROJ
    widths = tuple(p.shape[1] for p in parts)
    const = lambda i: (0, 0)
    return pl.pallas_call(
        functools.partial(_out_proj_kernel, widths=widths),
        out_shape=jax.ShapeDtypeStruct((rows, D_MODEL), F32),
        grid=(rows // tm,),
        in_specs=[
            pl.BlockSpec((tm, D_MODEL), lambda i: (i, 0)),
            pl.BlockSpec((None, 6, D_MODEL), _mod_index(seq_blocks)),
            pl.BlockSpec((1, D_MODEL), const),
            pl.BlockSpec((D_MODEL, D_MODEL), const),
        ] + [pl.BlockSpec((tm, wd), lambda i: (i, 0)) for wd in widths],
        out_specs=pl.BlockSpec((tm, D_MODEL), lambda i: (i, 0)),
        compiler_params=_cparams(("parallel",)),
        name="mixer_out_proj",
    )(x, mod, g, w, *parts)


def _ffn_kernel(xp_ref, x_ref, xn_ref, mod_ref, gpre_ref, gpost_ref, wg_ref, wv_ref, cg_ref, cv_ref, wo_ref,
                o_ref, h_ref, *, seq):
    tm = x_ref.shape[0]
    ext = tm + 2 * HALO
    pos0 = (pl.program_id(0) * tm) % seq
    keep_prev = jnp.where(pos0 > 0, 1.0, 0.0)
    keep_next = jnp.where(pos0 + tm < seq, 1.0, 0.0)
    scale, shift = mod_ref[4:5, :], mod_ref[3:4, :]
    g = gpre_ref[...]
    h_ref[0:HALO, :] = (_norm_mod(xp_ref[...], g, scale, shift) * keep_prev).astype(BF16)
    h_ref[HALO:HALO + tm, :] = _norm_mod(x_ref[...], g, scale, shift).astype(BF16)
    h_ref[HALO + tm:, :] = (_norm_mod(xn_ref[...], g, scale, shift) * keep_next).astype(BF16)

    def conv(u, wc):
        up = pltpu.roll(u, 1, 0)[HALO:HALO + tm]
        un = pltpu.roll(u, ext - 1, 0)[HALO:HALO + tm]
        return u[HALO:HALO + tm] * wc[1:2, :] + up * wc[0:1, :] + un * wc[2:3, :]

    y = None
    for j in range(wg_ref.shape[0]):
        hb = h_ref[...]
        gate = conv(_dot(hb, wg_ref[j]), cg_ref[j])
        val = conv(_dot(hb, wv_ref[j]), cv_ref[j])
        act = gate * (1.0 / (1.0 + jnp.exp(-gate))) * val
        contrib = _dot(act.astype(BF16), wo_ref[j])
        y = contrib if y is None else y + contrib
    o_ref[...] = x_ref[...] + mod_ref[5:6, :] * _rmsnorm(y, gpost_ref[...])


def _conv_ffn(x, mod, g_pre, g_post, wg, wv, cg, cv, wo, seq, seq_blocks):
    rows = x.shape[0]
    tm = min(TM_FFN, seq)
    assert seq % tm == 0
    nch = wg.shape[0]
    const2 = lambda i: (0, 0)
    const3 = lambda i: (0, 0, 0)
    resident = dict(pipeline_mode=pl.Buffered(1))
    return pl.pallas_call(
        functools.partial(_ffn_kernel, seq=seq),
        out_shape=jax.ShapeDtypeStruct((rows, D_MODEL), F32),
        grid=(rows // tm,),
        in_specs=_halo_specs(tm, D_MODEL, rows) + [
            pl.BlockSpec((None, 6, D_MODEL), _mod_index(seq_blocks)),
            pl.BlockSpec((1, D_MODEL), const2),
            pl.BlockSpec((1, D_MODEL), const2),
            pl.BlockSpec((nch, D_MODEL, FF_CHUNK), const3, **resident),
            pl.BlockSpec((nch, D_MODEL, FF_CHUNK), const3, **resident),
            pl.BlockSpec((nch, 3, FF_CHUNK), const3),
            pl.BlockSpec((nch, 3, FF_CHUNK), const3),
            pl.BlockSpec((nch, FF_CHUNK, D_MODEL), const3, **resident),
        ],
        out_specs=pl.BlockSpec((tm, D_MODEL), lambda i: (i, 0)),
        scratch_shapes=[pltpu.VMEM((tm + 2 * HALO, D_MODEL), BF16)],
        compiler_params=_cparams(("parallel",)),
        name="conv_ffn",
    )(x, x, x, mod, g_pre, g_post, wg, wv, cg, cv, wo)


def _rope_tables(n_tokens):
    t = jnp.arange(n_tokens)
    row = (t // GRID_W).astype(F32)
    col = (t % GRID_W).astype(F32)
    axis_dim = HEAD_DIM // 2
    inv_freq = ROPE_THETA ** (-jnp.arange(0, axis_dim, 2, dtype=F32) / axis_dim)
    ang = jnp.concatenate([row[:, None] * inv_freq, col[:, None] * inv_freq], axis=-1)
    cos = jnp.repeat(jnp.cos(ang), 2, axis=-1)
    sin = jnp.sin(ang)
    sin = jnp.stack([-sin, sin], axis=-1).reshape(n_tokens, HEAD_DIM)
    return jnp.tile(cos, (1, 2)), jnp.tile(sin, (1, 2))


def _chunk_cols(w, n):
    return jnp.transpose(w.reshape(w.shape[0], n, FF_CHUNK), (1, 0, 2))


def kernel(x_prompt, x_sample, cache_b_k, cache_b_v, cache_c_k, cache_c_v, cache_d_k, cache_d_v, c, c_ctx,
           w_mod, b_mod, g_mix_pre, g_mix_post, g_ffn_pre, g_ffn_post, w_in_even, w_pool, pool_scale,
           g_q_b, g_k_b, w_in_odd, sink_c, rpb_d, w_mix_out, w_ffn_in, w_ffn_conv, w_ffn_out):
    n_ctx, ctx_seq, _ = x_prompt.shape
    n_lat, lat_seq, _ = x_sample.shape
    past = cache_b_k.shape[2]
    xp = x_prompt.reshape(n_ctx * ctx_seq, D_MODEL)
    xs = x_sample.reshape(n_lat * lat_seq, D_MODEL)
    lat_blocks = lat_seq // TM_PROJ

    cvecs = jnp.concatenate([c_ctx[None, :], c, jnp.zeros((8 - 1 - n_lat, D_MODEL), F32)], axis=0)
    mods = _modulation(cvecs, w_mod, b_mod)[:, :1 + n_lat].reshape(DEPTH, 1 + n_lat, 6, D_MODEL)

    rope_tabs = _rope_tables(lat_seq)
    blk = jnp.arange(LANES) // HEAD_DIM
    e_mat = jnp.where(blk[:, None] == blk[None, :], 1.0 / HEAD_DIM, 0.0).astype(BF16)
    n_ff = D_FF // FF_CHUNK

    b_k, b_v, c_k, c_v, d_k, d_v = [], [], [], [], [], []
    for l in range(DEPTH):
        i = l // 2
        mod = mods[l]
        g_pre = g_mix_pre[l][None, :]
        if l % 2 == 0:
            w_in = w_in_even[i].astype(BF16)
            gq = jnp.tile(g_q_b[i], 2)[None, :]
            gk = jnp.tile(g_k_b[i], 2)[None, :]
            kvw = B_KV * HEAD_DIM
            widths = (POOL_WIDTH, B_HEADS * LANES, kvw, kvw)
            a_p, q_p, k_p, v_p = _proj_in(xp, mod, g_pre, w_in, (gq, gk, e_mat), widths,
                                          (F32, BF16, F32, F32), _proj_even_kernel, None, None,
                                          "proj_even_ctx")
            a_s, q_s, k_s, v_s = _proj_in(xs, mod, g_pre, w_in, (gq, gk, e_mat), widths,
                                          (F32, BF16, BF16, BF16), _proj_even_kernel, lat_blocks, rope_tabs,
                                          "proj_even_lat")
            b_k.append(k_p.reshape(n_ctx, ctx_seq, B_KV, HEAD_DIM))
            b_v.append(v_p.reshape(n_ctx, ctx_seq, B_KV, HEAD_DIM))
            w_bd = jax.scipy.linalg.block_diag(*[w_pool[i, g] for g in range(len(POOL_WINDOWS))]).astype(BF16)
            pscale = pool_scale[i][None, :]
            ya_p = _pool_mixer(a_p, w_bd, pscale, ctx_seq)
            ya_s = _pool_mixer(a_s, w_bd, pscale, lat_seq)
            yb_p = _ctx_even_attn(q_p, k_p, v_p, ctx_seq)
            ck = cache_b_k[:, i].reshape(n_lat * past, kvw).astype(BF16)
            cv = cache_b_v[:, i].reshape(n_lat * past, kvw).astype(BF16)
            yb_s = _flash_attn(q_s, k_s, v_s, ck, cv, n_lat)
            parts_p, parts_s = (ya_p, yb_p), (ya_s, yb_s)
        else:
            w_in = w_in_odd[i].astype(BF16)
            ckw, dw = C_KV * HEAD_DIM, D_HEADS * HEAD_DIM
            widths = (C_HEADS * LANES, ckw, ckw, D_HEADS * LANES, dw, dw)
            qc_p, kc_p, vc_p, qd_p, kd_p, vd_p = _proj_in(
                xp, mod, g_pre, w_in, (), widths, (BF16, F32, F32, BF16, F32, F32),
                _proj_odd_kernel, None, None, "proj_odd_ctx")
            qc_s, kc_s, vc_s, qd_s, kd_s, vd_s = _proj_in(
                xs, mod, g_pre, w_in, (), widths, (BF16,) * 6,
                _proj_odd_kernel, lat_blocks, rope_tabs, "proj_odd_lat")
            c_k.append(kc_p.reshape(n_ctx, ctx_seq, C_KV, HEAD_DIM))
            c_v.append(vc_p.reshape(n_ctx, ctx_seq, C_KV, HEAD_DIM))
            d_k.append(kd_p.reshape(n_ctx, ctx_seq, D_HEADS, HEAD_DIM))
            d_v.append(vd_p.reshape(n_ctx, ctx_seq, D_HEADS, HEAD_DIM))
            sink = sink_c[i]
            y_p = _ctx_odd_attn(sink, qc_p, kc_p, vc_p, qd_p, kd_p, vd_p, ctx_seq)
            cck = cache_c_k[:, i].reshape(n_lat * past, ckw).astype(BF16)
            ccv = cache_c_v[:, i].reshape(n_lat * past, ckw).astype(BF16)
            cdk = cache_d_k[:, i].reshape(n_lat * past, dw).astype(BF16)
            cdv = cache_d_v[:, i].reshape(n_lat * past, dw).astype(BF16)
            yc_s = _window_attn(sink, qc_s, kc_s, vc_s, cck, ccv, n_lat)
            yd_s = _na_attn(qd_s, kd_s, vd_s, cdk, cdv, _na_bias_table(rpb_d[i]), n_lat)
            parts_p, parts_s = (y_p,), (yc_s, yd_s)

        w_out = w_mix_out[l].astype(BF16)
        g_post = g_mix_post[l][None, :]
        xp = _out_proj(xp, mod, g_post, w_out, parts_p, None)
        xs = _out_proj(xs, mod, g_post, w_out, parts_s, lat_blocks)

        wg = _chunk_cols(w_ffn_in[l][:, :D_FF], n_ff).astype(BF16)
        wv = _chunk_cols(w_ffn_in[l][:, D_FF:], n_ff).astype(BF16)
        cg = _chunk_cols(w_ffn_conv[l][:, :D_FF], n_ff)
        cvw = _chunk_cols(w_ffn_conv[l][:, D_FF:], n_ff)
        wo = w_ffn_out[l].reshape(n_ff, FF_CHUNK, D_MODEL).astype(BF16)
        gf_pre, gf_post = g_ffn_pre[l][None, :], g_ffn_post[l][None, :]
        xp = _conv_ffn(xp, mod, gf_pre, gf_post, wg, wv, cg, cvw, wo, ctx_seq, None)
        xs = _conv_ffn(xs, mod, gf_pre, gf_post, wg, wv, cg, cvw, wo, lat_seq, lat_seq // TM_FFN)

    return (xp.reshape(n_ctx, ctx_seq, D_MODEL), xs.reshape(n_lat, lat_seq, D_MODEL),
            jnp.stack(b_k, axis=1), jnp.stack(b_v, axis=1),
            jnp.stack(c_k, axis=1), jnp.stack(c_v, axis=1),
            jnp.stack(d_k, axis=1), jnp.stack(d_v, axis=1))
```

```python
import functools

import jax
import jax.numpy as jnp
import numpy as np
from jax import lax
from jax.experimental import pallas as pl
from jax.experimental.pallas import tpu as pltpu

F32 = jnp.float32
BF16 = jnp.bfloat16

D_MODEL = 1024
DEPTH = 4
GRID_W = 64
HEAD_DIM = 64
EPS = 1e-6
ROPE_THETA = 10000.0
POOL_WINDOWS = (2, 4, 8, 16)
POOL_WIDTH = 256
B_HEADS, B_KV = 12, 4
C_HEADS, C_KV = 8, 2
D_HEADS = 8
C_WINDOW = 128
NA_ROWS, NA_COLS = 8, 16
D_FF = 2816
QK_SCALE = HEAD_DIM ** -0.5
NEG = -1e30

LANES = 128
HALO = 16
VMEM_LIMIT = 56 * 1024 * 1024

TM_PROJ = 512
TM_FFN = 512
TM_POOL = 256
FF_CHUNK = 1408
TQ_FLASH = 256
TK_FLASH = 512
TQ_WIN = 128
NA_QROWS = 4
NA_WIN = NA_ROWS + NA_QROWS


def _cparams(sem):
    return pltpu.CompilerParams(dimension_semantics=sem, vmem_limit_bytes=VMEM_LIMIT)


def _norm_mod(x, g, scale, shift):
    ms = jnp.mean(x * x, axis=-1, keepdims=True)
    return (x * lax.rsqrt(ms + EPS) * g) * (1.0 + scale) + shift


def _rmsnorm(x, g):
    ms = jnp.mean(x * x, axis=-1, keepdims=True)
    return x * lax.rsqrt(ms + EPS) * g


def _dot(a, b):
    return jnp.dot(a, b, preferred_element_type=F32)


def _dot_nt(a, b):
    return lax.dot_general(a, b, (((1,), (1,)), ((), ())), preferred_element_type=F32)


def _low_half(rows):
    return lax.broadcasted_iota(jnp.int32, (rows, LANES), 1) < HEAD_DIM


def _layer_spec(shape, layer):
    zeros = (0,) * len(shape)
    return pl.BlockSpec((None,) + tuple(shape), lambda *_: (layer,) + zeros)


def _mod_spec(layer, seq_blocks):
    if seq_blocks is None:
        return pl.BlockSpec((None, None, 6, D_MODEL), lambda i: (layer, 0, 0, 0))
    return pl.BlockSpec((None, None, 6, D_MODEL), lambda i: (layer, 1 + i // seq_blocks, 0, 0))


def _mod_kernel(c_ref, w_ref, b_ref, o_ref):
    cv = c_ref[...]
    s = cv * (1.0 / (1.0 + jnp.exp(-cv)))
    o_ref[...] = _dot(s.astype(BF16), w_ref[...].astype(BF16)) + b_ref[...]


def _modulation(cvecs, w_mod, b_mod):
    tn = 1536
    return pl.pallas_call(
        _mod_kernel,
        out_shape=jax.ShapeDtypeStruct((DEPTH, 8, 6 * D_MODEL), F32),
        grid=(DEPTH, 6 * D_MODEL // tn),
        in_specs=[
            pl.BlockSpec((8, D_MODEL), lambda l, j: (0, 0)),
            pl.BlockSpec((None, D_MODEL, tn), lambda l, j: (l, 0, j)),
            pl.BlockSpec((None, 1, tn), lambda l, j: (l, 0, j)),
        ],
        out_specs=pl.BlockSpec((None, 8, tn), lambda l, j: (l, 0, j)),
        compiler_params=_cparams(("parallel", "parallel")),
        name="modulation",
    )(cvecs, w_mod, b_mod.reshape(DEPTH, 1, 6 * D_MODEL))


def _pair_swap(s):
    lane = lax.broadcasted_iota(jnp.int32, s.shape, 1)
    n = s.shape[1]
    return jnp.where(lane % 2 == 0, pltpu.roll(s, n - 1, 1), pltpu.roll(s, 1, 1))


def _store_expanded_q(q_ref, s, slab, kv_half_of, low):
    sr = pltpu.roll(s, HEAD_DIM, 1)
    for half in (0, 1):
        hd = 2 * slab + half
        dst = kv_half_of(hd)
        val = s if dst == half else sr
        keep = low if dst == 0 else jnp.logical_not(low)
        q_ref[:, hd * LANES:(hd + 1) * LANES] = jnp.where(keep, val, 0.0).astype(BF16)


def _proj_even_kernel(*refs, rope):
    if rope:
        (x_ref, mod_ref, g_ref, w_ref, gq_ref, gk_ref, e_ref, cos_ref, sin_ref,
         a_ref, q_ref, k_ref, v_ref) = refs
    else:
        (x_ref, mod_ref, g_ref, w_ref, gq_ref, gk_ref, e_ref,
         a_ref, q_ref, k_ref, v_ref) = refs
    rows = x_ref.shape[0]
    h = _norm_mod(x_ref[...], g_ref[...], mod_ref[1:2, :], mod_ref[0:1, :])
    u = _dot(h.astype(BF16), w_ref[...])
    low = _low_half(rows)
    a_ref[...] = u[:, :POOL_WIDTH]

    def headnorm(s, g):
        ms = _dot((s * s).astype(BF16), e_ref[...])
        return s * lax.rsqrt(ms + EPS) * g

    def rot(s):
        if not rope:
            return s
        return s * cos_ref[...] + _pair_swap(s) * sin_ref[...]

    q0 = POOL_WIDTH
    for j in range(B_HEADS // 2):
        s = u[:, q0 + LANES * j:q0 + LANES * (j + 1)]
        s = rot(headnorm(s, gq_ref[...])) * QK_SCALE
        _store_expanded_q(q_ref, s, j, lambda hd: (hd // (B_HEADS // B_KV)) % 2, low)
    k0 = q0 + B_HEADS * HEAD_DIM
    for j in range(B_KV // 2):
        s = headnorm(u[:, k0 + LANES * j:k0 + LANES * (j + 1)], gk_ref[...])
        k_ref[:, LANES * j:LANES * (j + 1)] = rot(s).astype(k_ref.dtype)
    v0 = k0 + B_KV * HEAD_DIM
    v_ref[...] = u[:, v0:v0 + B_KV * HEAD_DIM].astype(v_ref.dtype)


def _proj_odd_kernel(*refs, rope):
    if rope:
        (x_ref, mod_ref, g_ref, w_ref, cos_ref, sin_ref,
         qc_ref, kc_ref, vc_ref, qd_ref, kd_ref, vd_ref) = refs
    else:
        (x_ref, mod_ref, g_ref, w_ref,
         qc_ref, kc_ref, vc_ref, qd_ref, kd_ref, vd_ref) = refs
    rows = x_ref.shape[0]
    h = _norm_mod(x_ref[...], g_ref[...], mod_ref[1:2, :], mod_ref[0:1, :])
    u = _dot(h.astype(BF16), w_ref[...])
    low = _low_half(rows)

    def rot(s):
        if not rope:
            return s
        return s * cos_ref[...] + _pair_swap(s) * sin_ref[...]

    for j in range(C_HEADS // 2):
        s = rot(u[:, LANES * j:LANES * (j + 1)]) * QK_SCALE
        _store_expanded_q(qc_ref, s, j, lambda hd: hd // (C_HEADS // C_KV), low)
    o = C_HEADS * HEAD_DIM
    kc_ref[...] = rot(u[:, o:o + LANES]).astype(kc_ref.dtype)
    o += C_KV * HEAD_DIM
    vc_ref[...] = u[:, o:o + LANES].astype(vc_ref.dtype)
    o += C_KV * HEAD_DIM
    for j in range(D_HEADS // 2):
        s = u[:, o + LANES * j:o + LANES * (j + 1)] * QK_SCALE
        _store_expanded_q(qd_ref, s, j, lambda hd: hd % 2, low)
    o += D_HEADS * HEAD_DIM
    kd_ref[...] = u[:, o:o + D_HEADS * HEAD_DIM].astype(kd_ref.dtype)
    o += D_HEADS * HEAD_DIM
    vd_ref[...] = u[:, o:o + D_HEADS * HEAD_DIM].astype(vd_ref.dtype)


def _proj_in(x, mods, layer, g_all, w_all, w_layer, extras, out_widths, out_dtypes, kernel, seq_blocks,
             rope_tabs, name):
    rows = x.shape[0]
    tm = TM_PROJ
    const = lambda i: (0, 0)
    in_specs = [
        pl.BlockSpec((tm, D_MODEL), lambda i: (i, 0)),
        _mod_spec(layer, seq_blocks),
        _layer_spec((1, D_MODEL), layer),
        _layer_spec(w_all.shape[1:], w_layer),
    ]
    args = [x, mods, g_all, w_all]
    for e in extras:
        in_specs.append(pl.BlockSpec(e.shape, const))
        args.append(e)
    if rope_tabs is not None:
        for t in rope_tabs:
            in_specs.append(pl.BlockSpec((tm, LANES), lambda i: (i % seq_blocks, 0)))
            args.append(t)
    return pl.pallas_call(
        functools.partial(kernel, rope=rope_tabs is not None),
        out_shape=[jax.ShapeDtypeStruct((rows, wd), dt) for wd, dt in zip(out_widths, out_dtypes)],
        grid=(rows // tm,),
        in_specs=in_specs,
        out_specs=[pl.BlockSpec((tm, wd), lambda i: (i, 0)) for wd in out_widths],
        compiler_params=_cparams(("parallel",)),
        name=name,
    )(*args)


def _place_heads(out_ref, col0, heads, kv_half_of, low):
    for m in range(len(heads) // 2):
        a = heads[2 * m]
        if kv_half_of(2 * m) == 1:
            a = pltpu.roll(a, HEAD_DIM, 1)
        b = heads[2 * m + 1]
        if kv_half_of(2 * m + 1) == 0:
            b = pltpu.roll(b, HEAD_DIM, 1)
        out_ref[:, col0 + m * LANES:col0 + (m + 1) * LANES] = jnp.where(low, a, b).astype(out_ref.dtype)


def _stack_heads(q_ref, heads):
    return jnp.concatenate([q_ref[:, h * LANES:(h + 1) * LANES] for h in heads], axis=0)


def _softmax_pv(scores, values, sink=None):
    m = scores[0].max(axis=-1, keepdims=True)
    for s in scores[1:]:
        m = jnp.maximum(m, s.max(axis=-1, keepdims=True))
    if sink is not None:
        m = jnp.maximum(m, sink)
    den = None
    acc = None
    for s, v in zip(scores, values):
        e = jnp.exp(s - m)
        d = e.sum(axis=-1, keepdims=True)
        o = _dot(e.astype(BF16), v)
        den = d if den is None else den + d
        acc = o if acc is None else acc + o
    if sink is not None:
        den = den + jnp.exp(sink - m)
    return acc / den


def _sink_column(sink_ref, heads, rows_per_head):
    row = lax.broadcasted_iota(jnp.int32, (len(heads) * rows_per_head, 1), 0)
    col = jnp.full(row.shape, sink_ref[heads[-1]], F32)
    for g in range(len(heads) - 2, -1, -1):
        col = jnp.where(row < (g + 1) * rows_per_head, sink_ref[heads[g]], col)
    return col


def _ctx_even_attn_kernel(q_ref, k_ref, v_ref, y_ref):
    rows = q_ref.shape[0]
    low = _low_half(rows)
    grp = B_HEADS // B_KV
    outs = []
    for kv in range(B_KV):
        sl = slice((kv // 2) * LANES, (kv // 2 + 1) * LANES)
        ks = k_ref[:, sl].astype(BF16)
        vs = v_ref[:, sl].astype(BF16)
        heads = list(range(kv * grp, (kv + 1) * grp))
        o = _softmax_pv([_dot_nt(_stack_heads(q_ref, heads), ks)], [vs])
        outs += [o[g * rows:(g + 1) * rows] for g in range(grp)]
    _place_heads(y_ref, 0, outs, lambda hd: (hd // grp) % 2, low)


def _ctx_odd_attn_kernel(sink_ref, qc_ref, kc_ref, vc_ref, qd_ref, kd_ref, vd_ref, y_ref):
    rows = qc_ref.shape[0]
    low = _low_half(rows)
    grp = C_HEADS // C_KV
    ks = kc_ref[...].astype(BF16)
    vs = vc_ref[...].astype(BF16)
    outs = []
    for kv in range(C_KV):
        heads = list(range(kv * grp, (kv + 1) * grp))
        sink = _sink_column(sink_ref, heads, rows)
        o = _softmax_pv([_dot_nt(_stack_heads(qc_ref, heads), ks)], [vs], sink)
        outs += [o[g * rows:(g + 1) * rows] for g in range(grp)]
    _place_heads(y_ref, 0, outs, lambda hd: hd // grp, low)
    outs = []
    for hd in range(D_HEADS):
        sl = slice((hd // 2) * LANES, (hd // 2 + 1) * LANES)
        s = _dot_nt(qd_ref[:, hd * LANES:(hd + 1) * LANES], kd_ref[:, sl].astype(BF16))
        outs.append(_softmax_pv([s], [vd_ref[:, sl].astype(BF16)]))
    _place_heads(y_ref, C_HEADS * HEAD_DIM, outs, lambda hd: hd % 2, low)


def _ctx_even_attn(q, k, v, seq):
    rows = q.shape[0]
    blk = lambda wd: pl.BlockSpec((seq, wd), lambda b: (b, 0))
    return pl.pallas_call(
        _ctx_even_attn_kernel,
        out_shape=jax.ShapeDtypeStruct((rows, B_HEADS * HEAD_DIM), BF16),
        grid=(rows // seq,),
        in_specs=[blk(q.shape[1]), blk(k.shape[1]), blk(v.shape[1])],
        out_specs=blk(B_HEADS * HEAD_DIM),
        compiler_params=_cparams(("parallel",)),
        name="ctx_even_attn",
    )(q, k, v)


def _ctx_odd_attn(sink_all, layer, qc, kc, vc, qd, kd, vd, seq):
    rows = qc.shape[0]
    blk = lambda a: pl.BlockSpec((seq, a.shape[1]), lambda b: (b, 0))
    return pl.pallas_call(
        functools.partial(_ctx_odd_attn_kernel),
        out_shape=jax.ShapeDtypeStruct((rows, D_MODEL), BF16),
        grid=(rows // seq,),
        in_specs=[pl.BlockSpec(memory_space=pltpu.SMEM)] + [blk(a) for a in (qc, kc, vc, qd, kd, vd)],
        out_specs=pl.BlockSpec((seq, D_MODEL), lambda b: (b, 0)),
        compiler_params=_cparams(("parallel",)),
        name="ctx_odd_attn",
    )(sink_all[layer], qc, kc, vc, qd, kd, vd)


def _pool_kernel(ap_ref, a_ref, an_ref, w_ref, ps_ref, y_ref, *, seq):
    tm = a_ref.shape[0]
    i = pl.program_id(0)
    pos0 = (i * tm) % seq
    a = a_ref[...]
    a_ext = jnp.concatenate([ap_ref[...], a, an_ref[...]], axis=0).astype(BF16)
    ext = tm + 2 * HALO
    r = lax.broadcasted_iota(jnp.int32, (tm, ext), 0)
    c = lax.broadcasted_iota(jnp.int32, (tm, ext), 1)
    pos_c = pos0 - HALO + c
    in_seq = (pos_c >= 0) & (pos_c < seq)
    d = c - HALO - r
    pos_r = pos0 + lax.broadcasted_iota(jnp.int32, (tm, POOL_WIDTH), 0)
    grp = lax.broadcasted_iota(jnp.int32, (tm, POOL_WIDTH), 1) // (POOL_WIDTH // len(POOL_WINDOWS))
    mean = jnp.zeros((tm, POOL_WIDTH), F32)
    for gi, wdw in enumerate(POOL_WINDOWS):
        band = jnp.where(in_seq & (d >= -(wdw // 2)) & (d < wdw - wdw // 2), 1.0, 0.0).astype(BF16)
        lo = jnp.maximum(pos_r - wdw // 2, 0)
        hi = jnp.minimum(pos_r - wdw // 2 + wdw, seq)
        win_mean = _dot(band, a_ext) / (hi - lo).astype(F32)
        mean = jnp.where(grp == gi, win_mean, mean)
    pooled = (mean - a).astype(BF16)
    y_ref[...] = (_dot(pooled, w_ref[...]) * ps_ref[...]).astype(y_ref.dtype)


def _halo_specs(tm, width, n_rows):
    per = tm // HALO
    last = n_rows // HALO - 1
    prev = pl.BlockSpec((HALO, width), lambda i: (jnp.maximum(i * per - 1, 0), 0))
    cur = pl.BlockSpec((tm, width), lambda i: (i, 0))
    nxt = pl.BlockSpec((HALO, width), lambda i: (jnp.minimum((i + 1) * per, last), 0))
    return [prev, cur, nxt]


def _pool_mixer(a, w_bd, pscale_all, layer, seq):
    rows = a.shape[0]
    tm = TM_POOL
    assert seq % tm == 0
    const = lambda i: (0, 0)
    return pl.pallas_call(
        functools.partial(_pool_kernel, seq=seq),
        out_shape=jax.ShapeDtypeStruct((rows, POOL_WIDTH), BF16),
        grid=(rows // tm,),
        in_specs=_halo_specs(tm, POOL_WIDTH, rows) + [
            pl.BlockSpec((POOL_WIDTH, POOL_WIDTH), const),
            _layer_spec((1, POOL_WIDTH), layer),
        ],
        out_specs=pl.BlockSpec((tm, POOL_WIDTH), lambda i: (i, 0)),
        compiler_params=_cparams(("parallel",)),
        name="pool_mixer",
    )(a, a, a, w_bd, pscale_all)


def _flash_kernel(q_ref, k_ref, v_ref, ck_ref, cv_ref, y_ref, m_ref, acc_ref):
    tq = q_ref.shape[0]
    low = _low_half(tq)
    grp = B_HEADS // B_KV
    n_chunks = k_ref.shape[0] // TK_FLASH
    qs = [_stack_heads(q_ref, list(range(j * grp, (j + 1) * grp))) for j in range(2)]
    m_ref[...] = jnp.full(m_ref.shape, NEG, F32)
    acc_ref[...] = jnp.zeros(acc_ref.shape, F32)

    def step(kc, vc):
        lowk = _low_half(vc.shape[0])
        for j in range(2):
            own = lowk if j == 0 else jnp.logical_not(lowk)
            vj = jnp.where(own, vc, jnp.ones_like(vc))
            s = _dot_nt(qs[j], kc)
            m_prev = m_ref[j]
            m_new = jnp.maximum(m_prev, s.max(axis=-1, keepdims=True))
            p = jnp.exp(s - jnp.concatenate([m_new] * (s.shape[1] // LANES), axis=1))
            acc_ref[j] = jnp.exp(m_prev - m_new) * acc_ref[j] + _dot(p.astype(BF16), vj)
            m_ref[j] = m_new

    def body(c, carry):
        off = pl.multiple_of(c * TK_FLASH, TK_FLASH)
        step(k_ref[pl.ds(off, TK_FLASH), :], v_ref[pl.ds(off, TK_FLASH), :])
        return carry

    lax.fori_loop(0, n_chunks, body, 0)
    step(ck_ref[...], cv_ref[...])
    outs = []
    for j in range(2):
        acc = acc_ref[j]
        o = acc / pltpu.roll(acc, HEAD_DIM, 1)
        outs += [o[g * tq:(g + 1) * tq] for g in range(grp)]
    _place_heads(y_ref, 0, outs, lambda hd: hd // grp, low)


def _flash_attn(q, k, v, ck_all, cv_all, layer, n_batch):
    rows = q.shape[0]
    seq = rows // n_batch
    past = ck_all.shape[2]
    tq = TQ_FLASH
    nq = seq // tq
    grp = B_HEADS // B_KV
    qw = 2 * grp * LANES
    yw = 2 * grp * HEAD_DIM
    ctx = pl.BlockSpec((None, None, past, LANES), lambda b, p, i: (b, layer, 0, p))
    return pl.pallas_call(
        _flash_kernel,
        out_shape=jax.ShapeDtypeStruct((rows, B_HEADS * HEAD_DIM), BF16),
        grid=(n_batch, B_KV // 2, nq),
        in_specs=[
            pl.BlockSpec((tq, qw), lambda b, p, i: (b * nq + i, p)),
            pl.BlockSpec((seq, LANES), lambda b, p, i: (b, p)),
            pl.BlockSpec((seq, LANES), lambda b, p, i: (b, p)),
            ctx, ctx,
        ],
        out_specs=pl.BlockSpec((tq, yw), lambda b, p, i: (b * nq + i, p)),
        scratch_shapes=[
            pltpu.VMEM((2, grp * tq, LANES), F32),
            pltpu.VMEM((2, grp * tq, LANES), F32),
        ],
        compiler_params=_cparams(("parallel", "parallel", "arbitrary")),
        name="latent_flash_attn",
    )(q, k, v, ck_all, cv_all)


def _window_kernel(sink_ref, q_ref, kp_ref, kc_ref, kn_ref, vp_ref, vc_ref, vn_ref, ck_ref, cv_ref, y_ref,
                   *, n_blocks):
    tq = q_ref.shape[0]
    n = pl.program_id(1)
    low = _low_half(tq)
    grp = C_HEADS // C_KV
    rows = grp * tq
    r = lax.broadcasted_iota(jnp.int32, (rows, tq), 0) % tq
    c = lax.broadcasted_iota(jnp.int32, (rows, tq), 1)
    prev_ok = (c >= r) & (n > 0)
    next_ok = (c <= r) & (n < n_blocks - 1)
    outs = []
    for kv in range(C_KV):
        heads = list(range(kv * grp, (kv + 1) * grp))
        q = _stack_heads(q_ref, heads)
        sink = _sink_column(sink_ref, heads, tq)
        scores = [
            jnp.where(prev_ok, _dot_nt(q, kp_ref[...]), NEG),
            _dot_nt(q, kc_ref[...]),
            jnp.where(next_ok, _dot_nt(q, kn_ref[...]), NEG),
            _dot_nt(q, ck_ref[...]),
        ]
        o = _softmax_pv(scores, [vp_ref[...], vc_ref[...], vn_ref[...], cv_ref[...]], sink)
        outs += [o[g * tq:(g + 1) * tq] for g in range(grp)]
    _place_heads(y_ref, 0, outs, lambda hd: hd // grp, low)


def _window_attn(sink_all, q, k, v, ck_all, cv_all, layer, n_batch):
    rows = q.shape[0]
    seq = rows // n_batch
    past = ck_all.shape[2]
    tq = TQ_WIN
    nb = seq // tq
    prev = pl.BlockSpec((tq, LANES), lambda b, n: (b * nb + jnp.maximum(n - 1, 0), 0))
    cur = pl.BlockSpec((tq, LANES), lambda b, n: (b * nb + n, 0))
    nxt = pl.BlockSpec((tq, LANES), lambda b, n: (b * nb + jnp.minimum(n + 1, nb - 1), 0))
    ctx = pl.BlockSpec((None, None, past, LANES), lambda b, n: (b, layer, 0, 0))
    return pl.pallas_call(
        functools.partial(_window_kernel, n_blocks=nb),
        out_shape=jax.ShapeDtypeStruct((rows, C_HEADS * HEAD_DIM), BF16),
        grid=(n_batch, nb),
        in_specs=[
            pl.BlockSpec(memory_space=pltpu.SMEM),
            pl.BlockSpec((tq, C_HEADS * LANES), lambda b, n: (b * nb + n, 0)),
            prev, cur, nxt, prev, cur, nxt, ctx, ctx,
        ],
        out_specs=pl.BlockSpec((tq, C_HEADS * HEAD_DIM), lambda b, n: (b * nb + n, 0)),
        compiler_params=_cparams(("parallel", "parallel")),
        name="latent_window_attn",
    )(sink_all[layer], q, k, k, k, v, v, v, ck_all, cv_all)


def _na_window_start(blk, n_rows):
    return jnp.clip(blk * NA_QROWS - NA_ROWS // 2, 0, n_rows - NA_WIN)


def _na_kernel(q_ref, k_ref, v_ref, ck_ref, cv_ref, bias_ref, y_ref, *, n_rows):
    blk = pl.program_id(1)
    tq = NA_QROWS * GRID_W
    win = NA_WIN * GRID_W
    low = _low_half(tq)
    start = pl.multiple_of(_na_window_start(blk, n_rows) * GRID_W, GRID_W)
    outs = []
    for hd in range(D_HEADS):
        sl = slice((hd // 2) * LANES, (hd // 2 + 1) * LANES)
        q = q_ref[:, hd * LANES:(hd + 1) * LANES]
        kw = k_ref[pl.ds(start, win), sl]
        vw = v_ref[pl.ds(start, win), sl]
        s_loc = _dot_nt(q, kw) + bias_ref[hd]
        s_ctx = _dot_nt(q, ck_ref[:, sl])
        outs.append(_softmax_pv([s_loc, s_ctx], [vw, cv_ref[:, sl]]))
    _place_heads(y_ref, 0, outs, lambda hd: hd % 2, low)


def _na_attn(q, k, v, ck_all, cv_all, bias, layer, n_batch):
    rows = q.shape[0]
    seq = rows // n_batch
    past = ck_all.shape[2]
    n_rows = seq // GRID_W
    n_blk = n_rows // NA_QROWS
    width = D_HEADS * HEAD_DIM
    tq = NA_QROWS * GRID_W
    full = pl.BlockSpec((seq, width), lambda b, r: (b, 0))
    ctx = pl.BlockSpec((None, None, past, width), lambda b, r: (b, layer, 0, 0))
    which = lambda r: jnp.where(r == 0, 0, jnp.where(r == n_blk - 1, 2, 1))
    return pl.pallas_call(
        functools.partial(_na_kernel, n_rows=n_rows),
        out_shape=jax.ShapeDtypeStruct((rows, width), BF16),
        grid=(n_batch, n_blk),
        in_specs=[
            pl.BlockSpec((tq, D_HEADS * LANES), lambda b, r: (b * n_blk + r, 0)),
            full, full, ctx, ctx,
            pl.BlockSpec((None, D_HEADS, tq, NA_WIN * GRID_W), lambda b, r: (which(r), 0, 0, 0)),
        ],
        out_specs=pl.BlockSpec((tq, width), lambda b, r: (b * n_blk + r, 0)),
        compiler_params=_cparams(("parallel", "arbitrary")),
        name="latent_neighbourhood_attn",
    )(q, k, v, ck_all, cv_all, bias)


def _na_bias_table(rpb, n_rows):
    n_h, n_y, n_x = rpb.shape
    n_blk = n_rows // NA_QROWS
    assert n_blk >= 3
    period = n_x + GRID_W
    padded = jnp.concatenate([rpb, jnp.full((n_h, n_y, GRID_W), NEG, F32)], axis=-1)
    skew = jnp.tile(padded, (1, 1, GRID_W))[:, :, :GRID_W * (period - 1)]
    skew = skew.reshape(n_h, n_y, GRID_W, period - 1)[..., NA_COLS - 1:NA_COLS - 1 + GRID_W]
    qc = np.arange(GRID_W)[:, None]
    kc = np.arange(GRID_W)[None, :]
    cs = np.clip(qc - NA_COLS // 2, 0, GRID_W - NA_COLS)
    inside = (kc >= cs) & (kc < cs + NA_COLS)
    skew = jnp.where(inside, skew, NEG)
    masked = jnp.full((n_h, GRID_W, GRID_W), NEG, F32)
    tabs = []
    for blk in (0, 1, n_blk - 1):
        ws = int(np.clip(blk * NA_QROWS - NA_ROWS // 2, 0, n_rows - NA_WIN))
        q_rows = []
        for a in range(NA_QROWS):
            r = blk * NA_QROWS + a
            rs = int(np.clip(r - NA_ROWS // 2, 0, n_rows - NA_ROWS))
            pieces = [skew[:, ws + j - r + NA_ROWS - 1] if rs <= ws + j < rs + NA_ROWS else masked
                      for j in range(NA_WIN)]
            q_rows.append(jnp.concatenate(pieces, axis=-1))
        tabs.append(jnp.concatenate(q_rows, axis=1))
    return jnp.stack(tabs, axis=0)


def _out_proj_kernel(*refs, widths):
    x_ref, mod_ref, g_ref, w_ref = refs[:4]
    parts = refs[4:4 + len(widths)]
    o_ref = refs[4 + len(widths)]
    y = None
    off = 0
    for p_ref, wd in zip(parts, widths):
        t = _dot(p_ref[...], w_ref[off:off + wd, :])
        y = t if y is None else y + t
        off += wd
    o_ref[...] = x_ref[...] + mod_ref[2:3, :] * _rmsnorm(y, g_ref[...])


def _out_proj(x, mods, layer, g_all, w_all, parts, seq_blocks):
    rows = x.shape[0]
    tm = TM_PROJ
    widths = tuple(p.shape[1] for p in parts)
    return pl.pallas_call(
        functools.partial(_out_proj_kernel, widths=widths),
        out_shape=jax.ShapeDtypeStruct((rows, D_MODEL), F32),
        grid=(rows // tm,),
        in_specs=[
            pl.BlockSpec((tm, D_MODEL), lambda i: (i, 0)),
            _mod_spec(layer, seq_blocks),
            _layer_spec((1, D_MODEL), layer),
            _layer_spec((D_MODEL, D_MODEL), layer),
        ] + [pl.BlockSpec((tm, wd), lambda i: (i, 0)) for wd in widths],
        out_specs=pl.BlockSpec((tm, D_MODEL), lambda i: (i, 0)),
        compiler_params=_cparams(("parallel",)),
        name="mixer_out_proj",
    )(x, mods, g_all, w_all, *parts)


def _ffn_kernel(xp_ref, x_ref, xn_ref, mod_ref, gpre_ref, gpost_ref, wi_ref, wc_ref, wo_ref, o_ref, h_ref, *, seq):
    tm = x_ref.shape[0]
    ext = tm + 2 * HALO
    pos0 = (pl.program_id(0) * tm) % seq
    keep_prev = jnp.where(pos0 > 0, 1.0, 0.0)
    keep_next = jnp.where(pos0 + tm < seq, 1.0, 0.0)
    scale, shift = mod_ref[4:5, :], mod_ref[3:4, :]
    g = gpre_ref[...]
    h_ref[0:HALO, :] = (_norm_mod(xp_ref[...], g, scale, shift) * keep_prev).astype(BF16)
    h_ref[HALO:HALO + tm, :] = _norm_mod(x_ref[...], g, scale, shift).astype(BF16)
    h_ref[HALO + tm:, :] = (_norm_mod(xn_ref[...], g, scale, shift) * keep_next).astype(BF16)

    def conv_up(c0):
        u = _dot(h_ref[...], wi_ref[:, c0:c0 + FF_CHUNK])
        up = pltpu.roll(u, 1, 0)[HALO:HALO + tm]
        un = pltpu.roll(u, ext - 1, 0)[HALO:HALO + tm]
        return (u[HALO:HALO + tm] * wc_ref[1:2, c0:c0 + FF_CHUNK] + up * wc_ref[0:1, c0:c0 + FF_CHUNK]
                + un * wc_ref[2:3, c0:c0 + FF_CHUNK])

    y = None
    for j in range(D_FF // FF_CHUNK):
        gate = conv_up(j * FF_CHUNK)
        val = conv_up(D_FF + j * FF_CHUNK)
        act = gate * (1.0 / (1.0 + jnp.exp(-gate))) * val
        contrib = _dot(act.astype(BF16), wo_ref[j * FF_CHUNK:(j + 1) * FF_CHUNK, :])
        y = contrib if y is None else y + contrib
    o_ref[...] = x_ref[...] + mod_ref[5:6, :] * _rmsnorm(y, gpost_ref[...])


def _conv_ffn(x, mods, layer, g_pre_all, g_post_all, wi_all, wc_all, wo_all, seq, seq_blocks):
    rows = x.shape[0]
    tm = min(TM_FFN, seq)
    assert seq % tm == 0 and D_FF % FF_CHUNK == 0
    resident = dict(pipeline_mode=pl.Buffered(1))
    zeros3 = lambda i: (layer, 0, 0)
    return pl.pallas_call(
        functools.partial(_ffn_kernel, seq=seq),
        out_shape=jax.ShapeDtypeStruct((rows, D_MODEL), F32),
        grid=(rows // tm,),
        in_specs=_halo_specs(tm, D_MODEL, rows) + [
            _mod_spec(layer, seq_blocks),
            _layer_spec((1, D_MODEL), layer),
            _layer_spec((1, D_MODEL), layer),
            pl.BlockSpec((None, D_MODEL, 2 * D_FF), zeros3, **resident),
            _layer_spec((3, 2 * D_FF), layer),
            pl.BlockSpec((None, D_FF, D_MODEL), zeros3, **resident),
        ],
        out_specs=pl.BlockSpec((tm, D_MODEL), lambda i: (i, 0)),
        scratch_shapes=[pltpu.VMEM((tm + 2 * HALO, D_MODEL), BF16)],
        compiler_params=_cparams(("parallel",)),
        name="conv_ffn",
    )(x, x, x, mods, g_pre_all, g_post_all, wi_all, wc_all, wo_all)


def _rope_tables(n_tokens):
    t = jnp.arange(n_tokens)
    row = (t // GRID_W).astype(F32)
    col = (t % GRID_W).astype(F32)
    axis_dim = HEAD_DIM // 2
    inv_freq = ROPE_THETA ** (-jnp.arange(0, axis_dim, 2, dtype=F32) / axis_dim)
    ang = jnp.concatenate([row[:, None] * inv_freq, col[:, None] * inv_freq], axis=-1)
    cos = jnp.repeat(jnp.cos(ang), 2, axis=-1)
    sin = jnp.sin(ang)
    sin = jnp.stack([-sin, sin], axis=-1).reshape(n_tokens, HEAD_DIM)
    return jnp.tile(cos, (1, 2)), jnp.tile(sin, (1, 2))


def _rows3(p):
    return p.reshape(p.shape[0], 1, p.shape[1])


def kernel(x_prompt, x_sample, cache_b_k, cache_b_v, cache_c_k, cache_c_v, cache_d_k, cache_d_v, c, c_ctx,
           w_mod, b_mod, g_mix_pre, g_mix_post, g_ffn_pre, g_ffn_post, w_in_even, w_pool, pool_scale,
           g_q_b, g_k_b, w_in_odd, sink_c, rpb_d, w_mix_out, w_ffn_in, w_ffn_conv, w_ffn_out):
    n_ctx, ctx_seq, _ = x_prompt.shape
    n_lat, lat_seq, _ = x_sample.shape
    past = cache_b_k.shape[2]
    xp = x_prompt.reshape(n_ctx * ctx_seq, D_MODEL)
    xs = x_sample.reshape(n_lat * lat_seq, D_MODEL)
    lat_blocks = lat_seq // TM_PROJ

    cvecs = jnp.concatenate([c_ctx[None, :], c, jnp.zeros((8 - 1 - n_lat, D_MODEL), F32)], axis=0)
    mods = _modulation(cvecs, w_mod, b_mod).reshape(DEPTH, 8, 6, D_MODEL)

    rope_tabs = _rope_tables(lat_seq)
    blk = jnp.arange(LANES) // HEAD_DIM
    e_mat = jnp.where(blk[:, None] == blk[None, :], 1.0 / HEAD_DIM, 0.0).astype(BF16)

    w_even = w_in_even.astype(BF16)
    w_odd = w_in_odd.astype(BF16)
    w_out = w_mix_out.astype(BF16)
    w_fi = w_ffn_in.astype(BF16)
    w_fo = w_ffn_out.astype(BF16)
    n_even = cache_b_k.shape[1]
    n_odd = cache_c_k.shape[1]
    slabs = lambda t: t.astype(BF16).reshape(t.shape[0], t.shape[1], past, t.shape[3] * HEAD_DIM)
    cbk, cbv, cck, ccv, cdk, cdv = map(slabs, (cache_b_k, cache_b_v, cache_c_k, cache_c_v, cache_d_k, cache_d_v))
    g_pre, g_post = _rows3(g_mix_pre), _rows3(g_mix_post)
    gf_pre, gf_post = _rows3(g_ffn_pre), _rows3(g_ffn_post)
    pscale = _rows3(pool_scale)
    n_rows = lat_seq // GRID_W

    b_k, b_v, c_k, c_v, d_k, d_v = [], [], [], [], [], []
    for l in range(DEPTH):
        i = l // 2
        if l % 2 == 0:
            gq = jnp.tile(g_q_b[i], 2)[None, :]
            gk = jnp.tile(g_k_b[i], 2)[None, :]
            kvw = B_KV * HEAD_DIM
            widths = (POOL_WIDTH, B_HEADS * LANES, kvw, kvw)
            a_p, q_p, k_p, v_p = _proj_in(xp, mods, l, g_pre, w_even, i, (gq, gk, e_mat), widths,
                                          (F32, BF16, F32, F32), _proj_even_kernel, None, None,
                                          "proj_even_ctx")
            a_s, q_s, k_s, v_s = _proj_in(xs, mods, l, g_pre, w_even, i, (gq, gk, e_mat), widths,
                                          (F32, BF16, BF16, BF16), _proj_even_kernel, lat_blocks, rope_tabs,
                                          "proj_even_lat")
            b_k.append(k_p.reshape(n_ctx, ctx_seq, B_KV, HEAD_DIM))
            b_v.append(v_p.reshape(n_ctx, ctx_seq, B_KV, HEAD_DIM))
            w_bd = jax.scipy.linalg.block_diag(*[w_pool[i, g] for g in range(len(POOL_WINDOWS))]).astype(BF16)
            ya_p = _pool_mixer(a_p, w_bd, pscale, i, ctx_seq)
            ya_s = _pool_mixer(a_s, w_bd, pscale, i, lat_seq)
            yb_p = _ctx_even_attn(q_p, k_p, v_p, ctx_seq)
            yb_s = _flash_attn(q_s, k_s, v_s, cbk, cbv, i, n_lat)
            parts_p, parts_s = (ya_p, yb_p), (ya_s, yb_s)
        else:
            ckw, dw = C_KV * HEAD_DIM, D_HEADS * HEAD_DIM
            widths = (C_HEADS * LANES, ckw, ckw, D_HEADS * LANES, dw, dw)
            qc_p, kc_p, vc_p, qd_p, kd_p, vd_p = _proj_in(
                xp, mods, l, g_pre, w_odd, i, (), widths, (BF16, F32, F32, BF16, F32, F32),
                _proj_odd_kernel, None, None, "proj_odd_ctx")
            qc_s, kc_s, vc_s, qd_s, kd_s, vd_s = _proj_in(
                xs, mods, l, g_pre, w_odd, i, (), widths, (BF16,) * 6,
                _proj_odd_kernel, lat_blocks, rope_tabs, "proj_odd_lat")
            c_k.append(kc_p.reshape(n_ctx, ctx_seq, C_KV, HEAD_DIM))
            c_v.append(vc_p.reshape(n_ctx, ctx_seq, C_KV, HEAD_DIM))
            d_k.append(kd_p.reshape(n_ctx, ctx_seq, D_HEADS, HEAD_DIM))
            d_v.append(vd_p.reshape(n_ctx, ctx_seq, D_HEADS, HEAD_DIM))
            y_p = _ctx_odd_attn(sink_c, i, qc_p, kc_p, vc_p, qd_p, kd_p, vd_p, ctx_seq)
            yc_s = _window_attn(sink_c, qc_s, kc_s, vc_s, cck, ccv, i, n_lat)
            yd_s = _na_attn(qd_s, kd_s, vd_s, cdk, cdv, _na_bias_table(rpb_d[i], n_rows), i, n_lat)
            parts_p, parts_s = (y_p,), (yc_s, yd_s)

        xp = _out_proj(xp, mods, l, g_post, w_out, parts_p, None)
        xs = _out_proj(xs, mods, l, g_post, w_out, parts_s, lat_blocks)
        xp = _conv_ffn(xp, mods, l, gf_pre, gf_post, w_fi, w_ffn_conv, w_fo, ctx_seq, None)
        xs = _conv_ffn(xs, mods, l, gf_pre, gf_post, w_fi, w_ffn_conv, w_fo, lat_seq, lat_seq // TM_FFN)

    return (xp.reshape(n_ctx, ctx_seq, D_MODEL), xs.reshape(n_lat, lat_seq, D_MODEL),
            jnp.stack(b_k, axis=1), jnp.stack(b_v, axis=1),
            jnp.stack(c_k, axis=1), jnp.stack(c_v, axis=1),
            jnp.stack(d_k, axis=1), jnp.stack(d_v, axis=1))
```

```python
import functools

import jax
import jax.numpy as jnp
import numpy as np
from jax import lax
from jax.experimental import pallas as pl
from jax.experimental.pallas import tpu as pltpu

F32 = jnp.float32
BF16 = jnp.bfloat16

D_MODEL = 1024
DEPTH = 4
GRID_W = 64
HEAD_DIM = 64
EPS = 1e-6
ROPE_THETA = 10000.0
POOL_WINDOWS = (2, 4, 8, 16)
POOL_WIDTH = 256
B_HEADS, B_KV = 12, 4
C_HEADS, C_KV = 8, 2
D_HEADS = 8
C_WINDOW = 128
NA_ROWS, NA_COLS = 8, 16
D_FF = 2816
QK_SCALE = HEAD_DIM ** -0.5
NEG = -1e30

LANES = 128
HALO = 16
VMEM_LIMIT = 56 * 1024 * 1024

TM_PROJ = 1024
TM_FFN = 512
TM_POOL = 256
FF_CHUNK = 1408
TQ_FLASH = 256
TK_FLASH = 512
TQ_WIN = 256
CTX_SEQS = 2
NA_QROWS = 4
NA_WIN = NA_ROWS + NA_QROWS


def _cparams(sem):
    return pltpu.CompilerParams(dimension_semantics=sem, vmem_limit_bytes=VMEM_LIMIT)


def _norm_mod(x, g, scale, shift):
    ms = jnp.mean(x * x, axis=-1, keepdims=True)
    return (x * lax.rsqrt(ms + EPS) * g) * (1.0 + scale) + shift


def _rmsnorm(x, g):
    ms = jnp.mean(x * x, axis=-1, keepdims=True)
    return x * lax.rsqrt(ms + EPS) * g


def _dot(a, b):
    return jnp.dot(a, b, preferred_element_type=F32)


def _dot_nt(a, b):
    return lax.dot_general(a, b, (((1,), (1,)), ((), ())), preferred_element_type=F32)


def _low_half(rows):
    return lax.broadcasted_iota(jnp.int32, (rows, LANES), 1) < HEAD_DIM


def _layer_spec(shape, layer):
    zeros = (0,) * len(shape)
    return pl.BlockSpec((None,) + tuple(shape), lambda *_: (layer,) + zeros)


def _mod_spec(layer, seq_blocks):
    if seq_blocks is None:
        return pl.BlockSpec((None, None, 6, D_MODEL), lambda i: (layer, 0, 0, 0))
    return pl.BlockSpec((None, None, 6, D_MODEL), lambda i: (layer, 1 + i // seq_blocks, 0, 0))


def _mod_kernel(c_ref, w_ref, b_ref, o_ref):
    cv = c_ref[...]
    s = cv * (1.0 / (1.0 + jnp.exp(-cv)))
    o_ref[...] = _dot(s.astype(BF16), w_ref[...].astype(BF16)) + b_ref[...]


def _modulation(cvecs, w_mod, b_mod):
    tn = 1536
    return pl.pallas_call(
        _mod_kernel,
        out_shape=jax.ShapeDtypeStruct((DEPTH, 8, 6 * D_MODEL), F32),
        grid=(DEPTH, 6 * D_MODEL // tn),
        in_specs=[
            pl.BlockSpec((8, D_MODEL), lambda l, j: (0, 0)),
            pl.BlockSpec((None, D_MODEL, tn), lambda l, j: (l, 0, j)),
            pl.BlockSpec((None, 1, tn), lambda l, j: (l, 0, j)),
        ],
        out_specs=pl.BlockSpec((None, 8, tn), lambda l, j: (l, 0, j)),
        compiler_params=_cparams(("parallel", "parallel")),
        name="modulation",
    )(cvecs, w_mod, b_mod.reshape(DEPTH, 1, 6 * D_MODEL))


def _pair_swap(s):
    lane = lax.broadcasted_iota(jnp.int32, s.shape, 1)
    n = s.shape[1]
    return jnp.where(lane % 2 == 0, pltpu.roll(s, n - 1, 1), pltpu.roll(s, 1, 1))


def _store_expanded_q(q_ref, s, slab, kv_half_of, low):
    sr = pltpu.roll(s, HEAD_DIM, 1)
    for half in (0, 1):
        hd = 2 * slab + half
        dst = kv_half_of(hd)
        val = s if dst == half else sr
        keep = low if dst == 0 else jnp.logical_not(low)
        q_ref[:, hd * LANES:(hd + 1) * LANES] = jnp.where(keep, val, 0.0).astype(BF16)


def _proj_even_kernel(*refs, rope):
    if rope:
        (x_ref, mod_ref, g_ref, w_ref, gq_ref, gk_ref, e_ref, cos_ref, sin_ref,
         a_ref, q_ref, k_ref, v_ref) = refs
    else:
        (x_ref, mod_ref, g_ref, w_ref, gq_ref, gk_ref, e_ref,
         a_ref, q_ref, k_ref, v_ref) = refs
    rows = x_ref.shape[0]
    h = _norm_mod(x_ref[...], g_ref[...], mod_ref[1:2, :], mod_ref[0:1, :])
    u = _dot(h.astype(BF16), w_ref[...])
    low = _low_half(rows)
    a_ref[...] = u[:, :POOL_WIDTH]

    def headnorm(s, g):
        ms = _dot((s * s).astype(BF16), e_ref[...])
        return s * lax.rsqrt(ms + EPS) * g

    def rot(s):
        if not rope:
            return s
        return s * cos_ref[...] + _pair_swap(s) * sin_ref[...]

    q0 = POOL_WIDTH
    for j in range(B_HEADS // 2):
        s = u[:, q0 + LANES * j:q0 + LANES * (j + 1)]
        s = rot(headnorm(s, gq_ref[...])) * QK_SCALE
        _store_expanded_q(q_ref, s, j, lambda hd: (hd // (B_HEADS // B_KV)) % 2, low)
    k0 = q0 + B_HEADS * HEAD_DIM
    for j in range(B_KV // 2):
        s = headnorm(u[:, k0 + LANES * j:k0 + LANES * (j + 1)], gk_ref[...])
        k_ref[:, LANES * j:LANES * (j + 1)] = rot(s).astype(k_ref.dtype)
    v0 = k0 + B_KV * HEAD_DIM
    v_ref[...] = u[:, v0:v0 + B_KV * HEAD_DIM].astype(v_ref.dtype)


def _proj_odd_kernel(*refs, rope):
    if rope:
        (x_ref, mod_ref, g_ref, w_ref, cos_ref, sin_ref,
         qc_ref, kc_ref, vc_ref, qd_ref, kd_ref, vd_ref) = refs
    else:
        (x_ref, mod_ref, g_ref, w_ref,
         qc_ref, kc_ref, vc_ref, qd_ref, kd_ref, vd_ref) = refs
    rows = x_ref.shape[0]
    h = _norm_mod(x_ref[...], g_ref[...], mod_ref[1:2, :], mod_ref[0:1, :])
    u = _dot(h.astype(BF16), w_ref[...])
    low = _low_half(rows)

    def rot(s):
        if not rope:
            return s
        return s * cos_ref[...] + _pair_swap(s) * sin_ref[...]

    for j in range(C_HEADS // 2):
        s = rot(u[:, LANES * j:LANES * (j + 1)]) * QK_SCALE
        _store_expanded_q(qc_ref, s, j, lambda hd: hd // (C_HEADS // C_KV), low)
    o = C_HEADS * HEAD_DIM
    kc_ref[...] = rot(u[:, o:o + LANES]).astype(kc_ref.dtype)
    o += C_KV * HEAD_DIM
    vc_ref[...] = u[:, o:o + LANES].astype(vc_ref.dtype)
    o += C_KV * HEAD_DIM
    for j in range(D_HEADS // 2):
        s = u[:, o + LANES * j:o + LANES * (j + 1)] * QK_SCALE
        _store_expanded_q(qd_ref, s, j, lambda hd: hd % 2, low)
    o += D_HEADS * HEAD_DIM
    kd_ref[...] = u[:, o:o + D_HEADS * HEAD_DIM].astype(kd_ref.dtype)
    o += D_HEADS * HEAD_DIM
    vd_ref[...] = u[:, o:o + D_HEADS * HEAD_DIM].astype(vd_ref.dtype)


def _proj_in(x, mods, layer, g_all, w_all, w_layer, extras, out_widths, out_dtypes, kernel, seq_blocks,
             rope_tabs, name):
    rows = x.shape[0]
    tm = TM_PROJ
    const = lambda i: (0, 0)
    in_specs = [
        pl.BlockSpec((tm, D_MODEL), lambda i: (i, 0)),
        _mod_spec(layer, seq_blocks),
        _layer_spec((1, D_MODEL), layer),
        _layer_spec(w_all.shape[1:], w_layer),
    ]
    args = [x, mods, g_all, w_all]
    for e in extras:
        in_specs.append(pl.BlockSpec(e.shape, const))
        args.append(e)
    if rope_tabs is not None:
        for t in rope_tabs:
            in_specs.append(pl.BlockSpec((tm, LANES), lambda i: (i % seq_blocks, 0)))
            args.append(t)
    return pl.pallas_call(
        functools.partial(kernel, rope=rope_tabs is not None),
        out_shape=[jax.ShapeDtypeStruct((rows, wd), dt) for wd, dt in zip(out_widths, out_dtypes)],
        grid=(rows // tm,),
        in_specs=in_specs,
        out_specs=[pl.BlockSpec((tm, wd), lambda i: (i, 0)) for wd in out_widths],
        compiler_params=_cparams(("parallel",)),
        name=name,
    )(*args)


def _place_heads(out_ref, col0, heads, kv_half_of, low):
    for m in range(len(heads) // 2):
        a = heads[2 * m]
        if kv_half_of(2 * m) == 1:
            a = pltpu.roll(a, HEAD_DIM, 1)
        b = heads[2 * m + 1]
        if kv_half_of(2 * m + 1) == 0:
            b = pltpu.roll(b, HEAD_DIM, 1)
        out_ref[:, col0 + m * LANES:col0 + (m + 1) * LANES] = jnp.where(low, a, b).astype(out_ref.dtype)


def _stack_heads(q_ref, heads):
    return jnp.concatenate([q_ref[:, h * LANES:(h + 1) * LANES] for h in heads], axis=0)


def _softmax_pv(scores, values, sink=None):
    m = scores[0].max(axis=-1, keepdims=True)
    for s in scores[1:]:
        m = jnp.maximum(m, s.max(axis=-1, keepdims=True))
    if sink is not None:
        m = jnp.maximum(m, sink)
    den = None
    acc = None
    for s, v in zip(scores, values):
        e = jnp.exp(s - m)
        d = e.sum(axis=-1, keepdims=True)
        o = _dot(e.astype(BF16), v)
        den = d if den is None else den + d
        acc = o if acc is None else acc + o
    if sink is not None:
        den = den + jnp.exp(sink - m)
    return acc / den


def _sink_column(sink_ref, heads, rows_per_head):
    row = lax.broadcasted_iota(jnp.int32, (len(heads) * rows_per_head, 1), 0)
    col = jnp.full(row.shape, sink_ref[heads[-1]], F32)
    for g in range(len(heads) - 2, -1, -1):
        col = jnp.where(row < (g + 1) * rows_per_head, sink_ref[heads[g]], col)
    return col


def _ctx_even_attn_seq(q_ref, k_ref, v_ref, y_ref):
    rows = q_ref.shape[0]
    low = _low_half(rows)
    grp = B_HEADS // B_KV
    outs = []
    for kv in range(B_KV):
        sl = slice((kv // 2) * LANES, (kv // 2 + 1) * LANES)
        ks = k_ref[:, sl].astype(BF16)
        vs = v_ref[:, sl].astype(BF16)
        heads = list(range(kv * grp, (kv + 1) * grp))
        o = _softmax_pv([_dot_nt(_stack_heads(q_ref, heads), ks)], [vs])
        outs += [o[g * rows:(g + 1) * rows] for g in range(grp)]
    _place_heads(y_ref, 0, outs, lambda hd: (hd // grp) % 2, low)


def _ctx_odd_attn_seq(sink_ref, qc_ref, kc_ref, vc_ref, qd_ref, kd_ref, vd_ref, y_ref):
    rows = qc_ref.shape[0]
    low = _low_half(rows)
    grp = C_HEADS // C_KV
    ks = kc_ref[...].astype(BF16)
    vs = vc_ref[...].astype(BF16)
    outs = []
    for kv in range(C_KV):
        heads = list(range(kv * grp, (kv + 1) * grp))
        sink = _sink_column(sink_ref, heads, rows)
        o = _softmax_pv([_dot_nt(_stack_heads(qc_ref, heads), ks)], [vs], sink)
        outs += [o[g * rows:(g + 1) * rows] for g in range(grp)]
    _place_heads(y_ref, 0, outs, lambda hd: hd // grp, low)
    outs = []
    for hd in range(D_HEADS):
        sl = slice((hd // 2) * LANES, (hd // 2 + 1) * LANES)
        s = _dot_nt(qd_ref[:, hd * LANES:(hd + 1) * LANES], kd_ref[:, sl].astype(BF16))
        outs.append(_softmax_pv([s], [vd_ref[:, sl].astype(BF16)]))
    _place_heads(y_ref, C_HEADS * HEAD_DIM, outs, lambda hd: hd % 2, low)


def _per_sequence(*refs, fn, seq, n_scalar=0):
    for i in range(refs[n_scalar].shape[0] // seq):
        rows = slice(i * seq, (i + 1) * seq)
        fn(*refs[:n_scalar], *[r.at[rows] for r in refs[n_scalar:]])


def _ctx_even_attn(q, k, v, seq):
    rows = q.shape[0]
    blk = lambda wd: pl.BlockSpec((CTX_SEQS * seq, wd), lambda b: (b, 0))
    return pl.pallas_call(
        functools.partial(_per_sequence, fn=_ctx_even_attn_seq, seq=seq),
        out_shape=jax.ShapeDtypeStruct((rows, B_HEADS * HEAD_DIM), BF16),
        grid=(rows // (CTX_SEQS * seq),),
        in_specs=[blk(q.shape[1]), blk(k.shape[1]), blk(v.shape[1])],
        out_specs=blk(B_HEADS * HEAD_DIM),
        compiler_params=_cparams(("parallel",)),
        name="ctx_even_attn",
    )(q, k, v)


def _ctx_odd_attn(sink_all, layer, qc, kc, vc, qd, kd, vd, seq):
    rows = qc.shape[0]
    blk = lambda wd: pl.BlockSpec((CTX_SEQS * seq, wd), lambda b: (b, 0))
    return pl.pallas_call(
        functools.partial(_per_sequence, fn=_ctx_odd_attn_seq, seq=seq, n_scalar=1),
        out_shape=jax.ShapeDtypeStruct((rows, D_MODEL), BF16),
        grid=(rows // (CTX_SEQS * seq),),
        in_specs=[pl.BlockSpec(memory_space=pltpu.SMEM)] + [blk(a.shape[1]) for a in (qc, kc, vc, qd, kd, vd)],
        out_specs=blk(D_MODEL),
        compiler_params=_cparams(("parallel",)),
        name="ctx_odd_attn",
    )(sink_all[layer], qc, kc, vc, qd, kd, vd)


def _pool_kernel(ap_ref, a_ref, an_ref, w_ref, ps_ref, y_ref, *, seq):
    tm = a_ref.shape[0]
    i = pl.program_id(0)
    pos0 = (i * tm) % seq
    a = a_ref[...]
    a_ext = jnp.concatenate([ap_ref[...], a, an_ref[...]], axis=0).astype(BF16)
    ext = tm + 2 * HALO
    r = lax.broadcasted_iota(jnp.int32, (tm, ext), 0)
    c = lax.broadcasted_iota(jnp.int32, (tm, ext), 1)
    pos_c = pos0 - HALO + c
    in_seq = (pos_c >= 0) & (pos_c < seq)
    d = c - HALO - r
    pos_r = pos0 + lax.broadcasted_iota(jnp.int32, (tm, POOL_WIDTH), 0)
    grp = lax.broadcasted_iota(jnp.int32, (tm, POOL_WIDTH), 1) // (POOL_WIDTH // len(POOL_WINDOWS))
    mean = jnp.zeros((tm, POOL_WIDTH), F32)
    for gi, wdw in enumerate(POOL_WINDOWS):
        band = jnp.where(in_seq & (d >= -(wdw // 2)) & (d < wdw - wdw // 2), 1.0, 0.0).astype(BF16)
        lo = jnp.maximum(pos_r - wdw // 2, 0)
        hi = jnp.minimum(pos_r - wdw // 2 + wdw, seq)
        win_mean = _dot(band, a_ext) / (hi - lo).astype(F32)
        mean = jnp.where(grp == gi, win_mean, mean)
    pooled = (mean - a).astype(BF16)
    y_ref[...] = (_dot(pooled, w_ref[...]) * ps_ref[...]).astype(y_ref.dtype)


def _halo_specs(tm, width, n_rows):
    per = tm // HALO
    last = n_rows // HALO - 1
    prev = pl.BlockSpec((HALO, width), lambda i: (jnp.maximum(i * per - 1, 0), 0))
    cur = pl.BlockSpec((tm, width), lambda i: (i, 0))
    nxt = pl.BlockSpec((HALO, width), lambda i: (jnp.minimum((i + 1) * per, last), 0))
    return [prev, cur, nxt]


def _pool_mixer(a, w_bd, pscale_all, layer, seq):
    rows = a.shape[0]
    tm = TM_POOL
    assert seq % tm == 0
    const = lambda i: (0, 0)
    return pl.pallas_call(
        functools.partial(_pool_kernel, seq=seq),
        out_shape=jax.ShapeDtypeStruct((rows, POOL_WIDTH), BF16),
        grid=(rows // tm,),
        in_specs=_halo_specs(tm, POOL_WIDTH, rows) + [
            pl.BlockSpec((POOL_WIDTH, POOL_WIDTH), const),
            _layer_spec((1, POOL_WIDTH), layer),
        ],
        out_specs=pl.BlockSpec((tm, POOL_WIDTH), lambda i: (i, 0)),
        compiler_params=_cparams(("parallel",)),
        name="pool_mixer",
    )(a, a, a, w_bd, pscale_all)


def _flash_kernel(q_ref, k_ref, v_ref, ck_ref, cv_ref, y_ref, m_ref, acc_ref):
    tq = q_ref.shape[0]
    low = _low_half(tq)
    grp = B_HEADS // B_KV
    n_chunks = k_ref.shape[0] // TK_FLASH
    qs = [_stack_heads(q_ref, list(range(j * grp, (j + 1) * grp))) for j in range(2)]
    m_ref[...] = jnp.full(m_ref.shape, NEG, F32)
    acc_ref[...] = jnp.zeros(acc_ref.shape, F32)

    def step(kc, vc):
        lowk = _low_half(vc.shape[0])
        for j in range(2):
            own = lowk if j == 0 else jnp.logical_not(lowk)
            vj = jnp.where(own, vc, jnp.ones_like(vc))
            s = _dot_nt(qs[j], kc)
            m_prev = m_ref[j]
            m_new = jnp.maximum(m_prev, s.max(axis=-1, keepdims=True))
            p = jnp.exp(s - jnp.concatenate([m_new] * (s.shape[1] // LANES), axis=1))
            acc_ref[j] = jnp.exp(m_prev - m_new) * acc_ref[j] + _dot(p.astype(BF16), vj)
            m_ref[j] = m_new

    def body(c, carry):
        off = pl.multiple_of(c * TK_FLASH, TK_FLASH)
        step(k_ref[pl.ds(off, TK_FLASH), :], v_ref[pl.ds(off, TK_FLASH), :])
        return carry

    lax.fori_loop(0, n_chunks, body, 0)
    step(ck_ref[...], cv_ref[...])
    outs = []
    for j in range(2):
        acc = acc_ref[j]
        o = acc / pltpu.roll(acc, HEAD_DIM, 1)
        outs += [o[g * tq:(g + 1) * tq] for g in range(grp)]
    _place_heads(y_ref, 0, outs, lambda hd: hd // grp, low)


def _flash_attn(q, k, v, ck_all, cv_all, layer, n_batch):
    rows = q.shape[0]
    seq = rows // n_batch
    past = ck_all.shape[2]
    tq = TQ_FLASH
    nq = seq // tq
    grp = B_HEADS // B_KV
    qw = 2 * grp * LANES
    yw = 2 * grp * HEAD_DIM
    ctx = pl.BlockSpec((None, None, past, LANES), lambda b, p, i: (b, layer, 0, p))
    return pl.pallas_call(
        _flash_kernel,
        out_shape=jax.ShapeDtypeStruct((rows, B_HEADS * HEAD_DIM), BF16),
        grid=(n_batch, B_KV // 2, nq),
        in_specs=[
            pl.BlockSpec((tq, qw), lambda b, p, i: (b * nq + i, p)),
            pl.BlockSpec((seq, LANES), lambda b, p, i: (b, p)),
            pl.BlockSpec((seq, LANES), lambda b, p, i: (b, p)),
            ctx, ctx,
        ],
        out_specs=pl.BlockSpec((tq, yw), lambda b, p, i: (b * nq + i, p)),
        scratch_shapes=[
            pltpu.VMEM((2, grp * tq, LANES), F32),
            pltpu.VMEM((2, grp * tq, LANES), F32),
        ],
        compiler_params=_cparams(("parallel", "parallel", "arbitrary")),
        name="latent_flash_attn",
    )(q, k, v, ck_all, cv_all)


def _window_kernel(sink_ref, q_ref, kp_ref, kc_ref, kn_ref, vp_ref, vc_ref, vn_ref, ck_ref, cv_ref, y_ref,
                   *, n_blocks):
    tq = q_ref.shape[0]
    n = pl.program_id(1)
    low = _low_half(tq)
    grp = C_HEADS // C_KV
    rows = grp * tq
    r = lax.broadcasted_iota(jnp.int32, (rows, C_WINDOW), 0) % tq
    c = lax.broadcasted_iota(jnp.int32, (rows, C_WINDOW), 1)
    prev_ok = (r - (c - C_WINDOW) <= C_WINDOW) & (n > 0)
    next_ok = ((c + tq) - r <= C_WINDOW) & (n < n_blocks - 1)
    r2 = lax.broadcasted_iota(jnp.int32, (rows, tq), 0) % tq
    c2 = lax.broadcasted_iota(jnp.int32, (rows, tq), 1)
    cur_ok = jnp.abs(r2 - c2) <= C_WINDOW
    outs = []
    for kv in range(C_KV):
        heads = list(range(kv * grp, (kv + 1) * grp))
        q = _stack_heads(q_ref, heads)
        sink = _sink_column(sink_ref, heads, tq)
        scores = [
            jnp.where(prev_ok, _dot_nt(q, kp_ref[...]), NEG),
            jnp.where(cur_ok, _dot_nt(q, kc_ref[...]), NEG),
            jnp.where(next_ok, _dot_nt(q, kn_ref[...]), NEG),
            _dot_nt(q, ck_ref[...]),
        ]
        o = _softmax_pv(scores, [vp_ref[...], vc_ref[...], vn_ref[...], cv_ref[...]], sink)
        outs += [o[g * tq:(g + 1) * tq] for g in range(grp)]
    _place_heads(y_ref, 0, outs, lambda hd: hd // grp, low)


def _window_attn(sink_all, q, k, v, ck_all, cv_all, layer, n_batch):
    rows = q.shape[0]
    seq = rows // n_batch
    past = ck_all.shape[2]
    tq = TQ_WIN
    assert tq % C_WINDOW == 0 and seq % tq == 0
    nb = seq // tq
    per = tq // C_WINDOW
    nw = seq // C_WINDOW
    prev = pl.BlockSpec((C_WINDOW, LANES), lambda b, n: (b * nw + jnp.maximum(n * per - 1, 0), 0))
    cur = pl.BlockSpec((tq, LANES), lambda b, n: (b * nb + n, 0))
    nxt = pl.BlockSpec((C_WINDOW, LANES), lambda b, n: (b * nw + jnp.minimum((n + 1) * per, nw - 1), 0))
    ctx = pl.BlockSpec((None, None, past, LANES), lambda b, n: (b, layer, 0, 0))
    return pl.pallas_call(
        functools.partial(_window_kernel, n_blocks=nb),
        out_shape=jax.ShapeDtypeStruct((rows, C_HEADS * HEAD_DIM), BF16),
        grid=(n_batch, nb),
        in_specs=[
            pl.BlockSpec(memory_space=pltpu.SMEM),
            pl.BlockSpec((tq, C_HEADS * LANES), lambda b, n: (b * nb + n, 0)),
            prev, cur, nxt, prev, cur, nxt, ctx, ctx,
        ],
        out_specs=pl.BlockSpec((tq, C_HEADS * HEAD_DIM), lambda b, n: (b * nb + n, 0)),
        compiler_params=_cparams(("parallel", "parallel")),
        name="latent_window_attn",
    )(sink_all[layer], q, k, k, k, v, v, v, ck_all, cv_all)


def _na_window_start(blk, n_rows):
    return jnp.clip(blk * NA_QROWS - NA_ROWS // 2, 0, n_rows - NA_WIN)


def _na_kernel(q_ref, k_ref, v_ref, ck_ref, cv_ref, pair_ref, y_ref, *, n_rows):
    blk = pl.program_id(1)
    tq = NA_QROWS * GRID_W
    win = NA_WIN * GRID_W
    low = _low_half(tq)
    ws = _na_window_start(blk, n_rows)
    start = pl.multiple_of(ws * GRID_W, GRID_W)
    q_row = blk * NA_QROWS + lax.broadcasted_iota(jnp.int32, (tq, win), 0) // GRID_W
    k_row = ws + lax.broadcasted_iota(jnp.int32, (tq, win), 1) // GRID_W
    rs = jnp.clip(q_row - NA_ROWS // 2, 0, n_rows - NA_ROWS)
    row_mask = jnp.where((k_row >= rs) & (k_row < rs + NA_ROWS), 0.0, NEG)
    outs = []
    for hd in range(D_HEADS):
        sl = slice((hd // 2) * LANES, (hd // 2 + 1) * LANES)
        q = q_ref[:, hd * LANES:(hd + 1) * LANES]
        kw = k_ref[pl.ds(start, win), sl]
        vw = v_ref[pl.ds(start, win), sl]
        bias = jnp.concatenate([
            jnp.concatenate([
                pair_ref[hd, jnp.clip(ws + 2 * jp - (blk * NA_QROWS + a) + NA_ROWS, 0, 2 * NA_ROWS)]
                for jp in range(NA_WIN // 2)], axis=1)
            for a in range(NA_QROWS)], axis=0)
        s_loc = _dot_nt(q, kw) + bias + row_mask
        s_ctx = _dot_nt(q, ck_ref[:, sl])
        outs.append(_softmax_pv([s_loc, s_ctx], [vw, cv_ref[:, sl]]))
    _place_heads(y_ref, 0, outs, lambda hd: hd % 2, low)


def _na_attn(q, k, v, ck_all, cv_all, pair_all, layer, n_batch):
    rows = q.shape[0]
    seq = rows // n_batch
    past = ck_all.shape[2]
    n_rows = seq // GRID_W
    n_blk = n_rows // NA_QROWS
    assert n_rows >= NA_WIN and NA_WIN % 2 == 0
    width = D_HEADS * HEAD_DIM
    tq = NA_QROWS * GRID_W
    full = pl.BlockSpec((seq, width), lambda b, r: (b, 0))
    ctx = pl.BlockSpec((None, None, past, width), lambda b, r: (b, layer, 0, 0))
    return pl.pallas_call(
        functools.partial(_na_kernel, n_rows=n_rows),
        out_shape=jax.ShapeDtypeStruct((rows, width), BF16),
        grid=(n_batch, n_blk),
        in_specs=[
            pl.BlockSpec((tq, D_HEADS * LANES), lambda b, r: (b * n_blk + r, 0)),
            full, full, ctx, ctx,
            _layer_spec(pair_all.shape[1:], layer),
        ],
        out_specs=pl.BlockSpec((tq, width), lambda b, r: (b * n_blk + r, 0)),
        compiler_params=_cparams(("parallel", "arbitrary")),
        name="latent_neighbourhood_attn",
    )(q, k, v, ck_all, cv_all, pair_all)


def _na_pair_tiles(rpb_all):
    n_l, n_h, n_y, n_x = rpb_all.shape
    period = n_x + GRID_W
    padded = jnp.concatenate([rpb_all, jnp.full((n_l, n_h, n_y, GRID_W), NEG, F32)], axis=-1)
    skew = jnp.tile(padded, (1, 1, 1, GRID_W))[..., :GRID_W * (period - 1)]
    skew = skew.reshape(n_l, n_h, n_y, GRID_W, period - 1)[..., NA_COLS - 1:NA_COLS - 1 + GRID_W]
    qc = np.arange(GRID_W)[:, None]
    kc = np.arange(GRID_W)[None, :]
    cs = np.clip(qc - NA_COLS // 2, 0, GRID_W - NA_COLS)
    inside = (kc >= cs) & (kc < cs + NA_COLS)
    skew = jnp.where(inside, skew, NEG)
    masked = jnp.full((n_l, n_h, 1, GRID_W, GRID_W), NEG, F32)
    rows = jnp.concatenate([masked, skew, masked, masked], axis=2)
    return jnp.concatenate([rows[:, :, :-1], rows[:, :, 1:]], axis=-1)


def _out_proj_kernel(*refs, widths):
    x_ref, mod_ref, g_ref, w_ref = refs[:4]
    parts = refs[4:4 + len(widths)]
    o_ref = refs[4 + len(widths)]
    y = None
    off = 0
    for p_ref, wd in zip(parts, widths):
        t = _dot(p_ref[...], w_ref[off:off + wd, :])
        y = t if y is None else y + t
        off += wd
    o_ref[...] = x_ref[...] + mod_ref[2:3, :] * _rmsnorm(y, g_ref[...])


def _out_proj(x, mods, layer, g_all, w_all, parts, seq_blocks):
    rows = x.shape[0]
    tm = TM_PROJ
    widths = tuple(p.shape[1] for p in parts)
    return pl.pallas_call(
        functools.partial(_out_proj_kernel, widths=widths),
        out_shape=jax.ShapeDtypeStruct((rows, D_MODEL), F32),
        grid=(rows // tm,),
        in_specs=[
            pl.BlockSpec((tm, D_MODEL), lambda i: (i, 0)),
            _mod_spec(layer, seq_blocks),
            _layer_spec((1, D_MODEL), layer),
            _layer_spec((D_MODEL, D_MODEL), layer),
        ] + [pl.BlockSpec((tm, wd), lambda i: (i, 0)) for wd in widths],
        out_specs=pl.BlockSpec((tm, D_MODEL), lambda i: (i, 0)),
        compiler_params=_cparams(("parallel",)),
        name="mixer_out_proj",
    )(x, mods, g_all, w_all, *parts)


def _ffn_kernel(xp_ref, x_ref, xn_ref, mod_ref, gpre_ref, gpost_ref, wi_ref, wc_ref, wo_ref, o_ref, h_ref, *, seq):
    tm = x_ref.shape[0]
    ext = tm + 2 * HALO
    pos0 = (pl.program_id(0) * tm) % seq
    keep_prev = jnp.where(pos0 > 0, 1.0, 0.0)
    keep_next = jnp.where(pos0 + tm < seq, 1.0, 0.0)
    scale, shift = mod_ref[4:5, :], mod_ref[3:4, :]
    g = gpre_ref[...]
    h_ref[0:HALO, :] = (_norm_mod(xp_ref[...], g, scale, shift) * keep_prev).astype(BF16)
    h_ref[HALO:HALO + tm, :] = _norm_mod(x_ref[...], g, scale, shift).astype(BF16)
    h_ref[HALO + tm:, :] = (_norm_mod(xn_ref[...], g, scale, shift) * keep_next).astype(BF16)

    def conv_up(c0):
        u = _dot(h_ref[...], wi_ref[:, c0:c0 + FF_CHUNK])
        up = pltpu.roll(u, 1, 0)[HALO:HALO + tm]
        un = pltpu.roll(u, ext - 1, 0)[HALO:HALO + tm]
        return (u[HALO:HALO + tm] * wc_ref[1:2, c0:c0 + FF_CHUNK] + up * wc_ref[0:1, c0:c0 + FF_CHUNK]
                + un * wc_ref[2:3, c0:c0 + FF_CHUNK])

    y = None
    for j in range(D_FF // FF_CHUNK):
        gate = conv_up(j * FF_CHUNK)
        val = conv_up(D_FF + j * FF_CHUNK)
        act = gate * (1.0 / (1.0 + jnp.exp(-gate))) * val
        contrib = _dot(act.astype(BF16), wo_ref[j * FF_CHUNK:(j + 1) * FF_CHUNK, :])
        y = contrib if y is None else y + contrib
    o_ref[...] = x_ref[...] + mod_ref[5:6, :] * _rmsnorm(y, gpost_ref[...])


def _conv_ffn(x, mods, layer, g_pre_all, g_post_all, wi_all, wc_all, wo_all, seq, seq_blocks):
    rows = x.shape[0]
    tm = min(TM_FFN, seq)
    assert seq % tm == 0 and D_FF % FF_CHUNK == 0
    resident = dict(pipeline_mode=pl.Buffered(1))
    zeros3 = lambda i: (layer, 0, 0)
    return pl.pallas_call(
        functools.partial(_ffn_kernel, seq=seq),
        out_shape=jax.ShapeDtypeStruct((rows, D_MODEL), F32),
        grid=(rows // tm,),
        in_specs=_halo_specs(tm, D_MODEL, rows) + [
            _mod_spec(layer, seq_blocks),
            _layer_spec((1, D_MODEL), layer),
            _layer_spec((1, D_MODEL), layer),
            pl.BlockSpec((None, D_MODEL, 2 * D_FF), zeros3, **resident),
            _layer_spec((3, 2 * D_FF), layer),
            pl.BlockSpec((None, D_FF, D_MODEL), zeros3, **resident),
        ],
        out_specs=pl.BlockSpec((tm, D_MODEL), lambda i: (i, 0)),
        scratch_shapes=[pltpu.VMEM((tm + 2 * HALO, D_MODEL), BF16)],
        compiler_params=_cparams(("parallel",)),
        name="conv_ffn",
    )(x, x, x, mods, g_pre_all, g_post_all, wi_all, wc_all, wo_all)


def _rope_tables(n_tokens):
    t = jnp.arange(n_tokens)
    row = (t // GRID_W).astype(F32)
    col = (t % GRID_W).astype(F32)
    axis_dim = HEAD_DIM // 2
    inv_freq = ROPE_THETA ** (-jnp.arange(0, axis_dim, 2, dtype=F32) / axis_dim)
    ang = jnp.concatenate([row[:, None] * inv_freq, col[:, None] * inv_freq], axis=-1)
    cos = jnp.repeat(jnp.cos(ang), 2, axis=-1)
    sin = jnp.sin(ang)
    sin = jnp.stack([-sin, sin], axis=-1).reshape(n_tokens, HEAD_DIM)
    return jnp.tile(cos, (1, 2)), jnp.tile(sin, (1, 2))


def _rows3(p):
    return p.reshape(p.shape[0], 1, p.shape[1])


def kernel(x_prompt, x_sample, cache_b_k, cache_b_v, cache_c_k, cache_c_v, cache_d_k, cache_d_v, c, c_ctx,
           w_mod, b_mod, g_mix_pre, g_mix_post, g_ffn_pre, g_ffn_post, w_in_even, w_pool, pool_scale,
           g_q_b, g_k_b, w_in_odd, sink_c, rpb_d, w_mix_out, w_ffn_in, w_ffn_conv, w_ffn_out):
    n_ctx, ctx_seq, _ = x_prompt.shape
    n_lat, lat_seq, _ = x_sample.shape
    past = cache_b_k.shape[2]
    xp = x_prompt.reshape(n_ctx * ctx_seq, D_MODEL)
    xs = x_sample.reshape(n_lat * lat_seq, D_MODEL)
    lat_blocks = lat_seq // TM_PROJ

    cvecs = jnp.concatenate([c_ctx[None, :], c, jnp.zeros((8 - 1 - n_lat, D_MODEL), F32)], axis=0)
    mods = _modulation(cvecs, w_mod, b_mod).reshape(DEPTH, 8, 6, D_MODEL)

    rope_tabs = _rope_tables(lat_seq)
    blk = jnp.arange(LANES) // HEAD_DIM
    e_mat = jnp.where(blk[:, None] == blk[None, :], 1.0 / HEAD_DIM, 0.0).astype(BF16)

    w_even = w_in_even.astype(BF16)
    w_odd = w_in_odd.astype(BF16)
    w_out = w_mix_out.astype(BF16)
    w_fi = w_ffn_in.astype(BF16)
    w_fo = w_ffn_out.astype(BF16)
    slabs = lambda t: t.astype(BF16).reshape(t.shape[0], t.shape[1], past, t.shape[3] * HEAD_DIM)
    cbk, cbv, cck, ccv, cdk, cdv = map(slabs, (cache_b_k, cache_b_v, cache_c_k, cache_c_v, cache_d_k, cache_d_v))
    g_pre, g_post = _rows3(g_mix_pre), _rows3(g_mix_post)
    gf_pre, gf_post = _rows3(g_ffn_pre), _rows3(g_ffn_post)
    pscale = _rows3(pool_scale)
    na_pairs = _na_pair_tiles(rpb_d)

    b_k, b_v, c_k, c_v, d_k, d_v = [], [], [], [], [], []
    for l in range(DEPTH):
        i = l // 2
        if l % 2 == 0:
            gq = jnp.tile(g_q_b[i], 2)[None, :]
            gk = jnp.tile(g_k_b[i], 2)[None, :]
            kvw = B_KV * HEAD_DIM
            widths = (POOL_WIDTH, B_HEADS * LANES, kvw, kvw)
            a_p, q_p, k_p, v_p = _proj_in(xp, mods, l, g_pre, w_even, i, (gq, gk, e_mat), widths,
                                          (F32, BF16, F32, F32), _proj_even_kernel, None, None,
                                          "proj_even_ctx")
            a_s, q_s, k_s, v_s = _proj_in(xs, mods, l, g_pre, w_even, i, (gq, gk, e_mat), widths,
                                          (F32, BF16, BF16, BF16), _proj_even_kernel, lat_blocks, rope_tabs,
                                          "proj_even_lat")
            b_k.append(k_p.reshape(n_ctx, ctx_seq, B_KV, HEAD_DIM))
            b_v.append(v_p.reshape(n_ctx, ctx_seq, B_KV, HEAD_DIM))
            w_bd = jax.scipy.linalg.block_diag(*[w_pool[i, g] for g in range(len(POOL_WINDOWS))]).astype(BF16)
            ya_p = _pool_mixer(a_p, w_bd, pscale, i, ctx_seq)
            ya_s = _pool_mixer(a_s, w_bd, pscale, i, lat_seq)
            yb_p = _ctx_even_attn(q_p, k_p, v_p, ctx_seq)
            yb_s = _flash_attn(q_s, k_s, v_s, cbk, cbv, i, n_lat)
            parts_p, parts_s = (ya_p, yb_p), (ya_s, yb_s)
        else:
            ckw, dw = C_KV * HEAD_DIM, D_HEADS * HEAD_DIM
            widths = (C_HEADS * LANES, ckw, ckw, D_HEADS * LANES, dw, dw)
            qc_p, kc_p, vc_p, qd_p, kd_p, vd_p = _proj_in(
                xp, mods, l, g_pre, w_odd, i, (), widths, (BF16, F32, F32, BF16, F32, F32),
                _proj_odd_kernel, None, None, "proj_odd_ctx")
            qc_s, kc_s, vc_s, qd_s, kd_s, vd_s = _proj_in(
                xs, mods, l, g_pre, w_odd, i, (), widths, (BF16,) * 6,
                _proj_odd_kernel, lat_blocks, rope_tabs, "proj_odd_lat")
            c_k.append(kc_p.reshape(n_ctx, ctx_seq, C_KV, HEAD_DIM))
            c_v.append(vc_p.reshape(n_ctx, ctx_seq, C_KV, HEAD_DIM))
            d_k.append(kd_p.reshape(n_ctx, ctx_seq, D_HEADS, HEAD_DIM))
            d_v.append(vd_p.reshape(n_ctx, ctx_seq, D_HEADS, HEAD_DIM))
            y_p = _ctx_odd_attn(sink_c, i, qc_p, kc_p, vc_p, qd_p, kd_p, vd_p, ctx_seq)
            yc_s = _window_attn(sink_c, qc_s, kc_s, vc_s, cck, ccv, i, n_lat)
            yd_s = _na_attn(qd_s, kd_s, vd_s, cdk, cdv, na_pairs, i, n_lat)
            parts_p, parts_s = (y_p,), (yc_s, yd_s)

        xp = _out_proj(xp, mods, l, g_post, w_out, parts_p, None)
        xs = _out_proj(xs, mods, l, g_post, w_out, parts_s, lat_blocks)
        xp = _conv_ffn(xp, mods, l, gf_pre, gf_post, w_fi, w_ffn_conv, w_fo, ctx_seq, None)
        xs = _conv_ffn(xs, mods, l, gf_pre, gf_post, w_fi, w_ffn_conv, w_fo, lat_seq, lat_seq // TM_FFN)

    return (xp.reshape(n_ctx, ctx_seq, D_MODEL), xs.reshape(n_lat, lat_seq, D_MODEL),
            jnp.stack(b_k, axis=1), jnp.stack(b_v, axis=1),
            jnp.stack(c_k, axis=1), jnp.stack(c_v, axis=1),
            jnp.stack(d_k, axis=1), jnp.stack(d_v, axis=1))
```

```python
import functools

import jax
import jax.numpy as jnp
import numpy as np
from jax import lax
from jax.experimental import pallas as pl
from jax.experimental.pallas import tpu as pltpu

F32 = jnp.float32
BF16 = jnp.bfloat16

D_MODEL = 1024
DEPTH = 4
GRID_W = 64
HEAD_DIM = 64
EPS = 1e-6
ROPE_THETA = 10000.0
POOL_WINDOWS = (2, 4, 8, 16)
POOL_WIDTH = 256
B_HEADS, B_KV = 12, 4
C_HEADS, C_KV = 8, 2
D_HEADS = 8
C_WINDOW = 128
NA_ROWS, NA_COLS = 8, 16
D_FF = 2816
QK_SCALE = HEAD_DIM ** -0.5
LOG2E = 1.4426950408889634
NEG = -1e30

LANES = 128
HALO = 16
VMEM_LIMIT = 56 * 1024 * 1024

TM_PROJ = 1024
TM_FFN = 512
TM_POOL = 256
FF_CHUNK = 1408
TQ_FLASH = 256
TK_FLASH = 512
TQ_WIN = 256
CTX_SEQS = 2
NA_QROWS = 4
NA_WIN = NA_ROWS + NA_QROWS


def _cparams(sem):
    return pltpu.CompilerParams(dimension_semantics=sem, vmem_limit_bytes=VMEM_LIMIT)


def _norm_mod(x, g, scale, shift):
    ms = jnp.mean(x * x, axis=-1, keepdims=True)
    return (x * lax.rsqrt(ms + EPS) * g) * (1.0 + scale) + shift


def _rmsnorm(x, g):
    ms = jnp.mean(x * x, axis=-1, keepdims=True)
    return x * lax.rsqrt(ms + EPS) * g


def _dot(a, b):
    return jnp.dot(a, b, preferred_element_type=F32)


def _dot_nt(a, b):
    return lax.dot_general(a, b, (((1,), (1,)), ((), ())), preferred_element_type=F32)


def _low_half(rows):
    return lax.broadcasted_iota(jnp.int32, (rows, LANES), 1) < HEAD_DIM


def _layer_spec(shape, layer):
    zeros = (0,) * len(shape)
    return pl.BlockSpec((None,) + tuple(shape), lambda *_: (layer,) + zeros)


def _mod_spec(layer, seq_blocks):
    if seq_blocks is None:
        return pl.BlockSpec((None, None, 6, D_MODEL), lambda i: (layer, 0, 0, 0))
    return pl.BlockSpec((None, None, 6, D_MODEL), lambda i: (layer, 1 + i // seq_blocks, 0, 0))


def _mod_kernel(c_ref, w_ref, b_ref, o_ref):
    cv = c_ref[...]
    s = cv * (1.0 / (1.0 + jnp.exp(-cv)))
    o_ref[...] = _dot(s.astype(BF16), w_ref[...].astype(BF16)) + b_ref[...]


def _modulation(cvecs, w_mod, b_mod):
    tn = 1536
    return pl.pallas_call(
        _mod_kernel,
        out_shape=jax.ShapeDtypeStruct((DEPTH, 8, 6 * D_MODEL), F32),
        grid=(DEPTH, 6 * D_MODEL // tn),
        in_specs=[
            pl.BlockSpec((8, D_MODEL), lambda l, j: (0, 0)),
            pl.BlockSpec((None, D_MODEL, tn), lambda l, j: (l, 0, j)),
            pl.BlockSpec((None, 1, tn), lambda l, j: (l, 0, j)),
        ],
        out_specs=pl.BlockSpec((None, 8, tn), lambda l, j: (l, 0, j)),
        compiler_params=_cparams(("parallel", "parallel")),
        name="modulation",
    )(cvecs, w_mod, b_mod.reshape(DEPTH, 1, 6 * D_MODEL))


def _pair_swap(s):
    lane = lax.broadcasted_iota(jnp.int32, s.shape, 1)
    n = s.shape[1]
    return jnp.where(lane % 2 == 0, pltpu.roll(s, n - 1, 1), pltpu.roll(s, 1, 1))


def _store_expanded_q(q_ref, s, slab, kv_half_of, low):
    sr = pltpu.roll(s, HEAD_DIM, 1)
    for half in (0, 1):
        hd = 2 * slab + half
        dst = kv_half_of(hd)
        val = s if dst == half else sr
        keep = low if dst == 0 else jnp.logical_not(low)
        q_ref[:, hd * LANES:(hd + 1) * LANES] = jnp.where(keep, val, 0.0).astype(BF16)


def _proj_even_kernel(*refs, rope, q_scale):
    if rope:
        (x_ref, mod_ref, g_ref, w_ref, gq_ref, gk_ref, e_ref, cos_ref, sin_ref,
         a_ref, q_ref, k_ref, v_ref) = refs
    else:
        (x_ref, mod_ref, g_ref, w_ref, gq_ref, gk_ref, e_ref,
         a_ref, q_ref, k_ref, v_ref) = refs
    rows = x_ref.shape[0]
    h = _norm_mod(x_ref[...], g_ref[...], mod_ref[1:2, :], mod_ref[0:1, :])
    u = _dot(h.astype(BF16), w_ref[...])
    low = _low_half(rows)
    a_ref[...] = u[:, :POOL_WIDTH]

    def headnorm(s, g):
        ms = _dot((s * s).astype(BF16), e_ref[...])
        return s * lax.rsqrt(ms + EPS) * g

    def rot(s):
        if not rope:
            return s
        return s * cos_ref[...] + _pair_swap(s) * sin_ref[...]

    q0 = POOL_WIDTH
    for j in range(B_HEADS // 2):
        s = u[:, q0 + LANES * j:q0 + LANES * (j + 1)]
        s = rot(headnorm(s, gq_ref[...])) * q_scale
        _store_expanded_q(q_ref, s, j, lambda hd: (hd // (B_HEADS // B_KV)) % 2, low)
    k0 = q0 + B_HEADS * HEAD_DIM
    for j in range(B_KV // 2):
        s = headnorm(u[:, k0 + LANES * j:k0 + LANES * (j + 1)], gk_ref[...])
        k_ref[:, LANES * j:LANES * (j + 1)] = rot(s).astype(k_ref.dtype)
    v0 = k0 + B_KV * HEAD_DIM
    v_ref[...] = u[:, v0:v0 + B_KV * HEAD_DIM].astype(v_ref.dtype)


def _proj_odd_kernel(*refs, rope):
    if rope:
        (x_ref, mod_ref, g_ref, w_ref, cos_ref, sin_ref,
         qc_ref, kc_ref, vc_ref, qd_ref, kd_ref, vd_ref) = refs
    else:
        (x_ref, mod_ref, g_ref, w_ref,
         qc_ref, kc_ref, vc_ref, qd_ref, kd_ref, vd_ref) = refs
    rows = x_ref.shape[0]
    h = _norm_mod(x_ref[...], g_ref[...], mod_ref[1:2, :], mod_ref[0:1, :])
    u = _dot(h.astype(BF16), w_ref[...])
    low = _low_half(rows)

    def rot(s):
        if not rope:
            return s
        return s * cos_ref[...] + _pair_swap(s) * sin_ref[...]

    for j in range(C_HEADS // 2):
        s = rot(u[:, LANES * j:LANES * (j + 1)]) * QK_SCALE
        _store_expanded_q(qc_ref, s, j, lambda hd: hd // (C_HEADS // C_KV), low)
    o = C_HEADS * HEAD_DIM
    kc_ref[...] = rot(u[:, o:o + LANES]).astype(kc_ref.dtype)
    o += C_KV * HEAD_DIM
    vc_ref[...] = u[:, o:o + LANES].astype(vc_ref.dtype)
    o += C_KV * HEAD_DIM
    for j in range(D_HEADS // 2):
        s = u[:, o + LANES * j:o + LANES * (j + 1)] * QK_SCALE
        _store_expanded_q(qd_ref, s, j, lambda hd: hd % 2, low)
    o += D_HEADS * HEAD_DIM
    kd_ref[...] = u[:, o:o + D_HEADS * HEAD_DIM].astype(kd_ref.dtype)
    o += D_HEADS * HEAD_DIM
    vd_ref[...] = u[:, o:o + D_HEADS * HEAD_DIM].astype(vd_ref.dtype)


def _proj_in(x, mods, layer, g_all, w_all, w_layer, extras, out_widths, out_dtypes, kernel, seq_blocks,
             rope_tabs, name):
    rows = x.shape[0]
    tm = TM_PROJ
    const = lambda i: (0, 0)
    in_specs = [
        pl.BlockSpec((tm, D_MODEL), lambda i: (i, 0)),
        _mod_spec(layer, seq_blocks),
        _layer_spec((1, D_MODEL), layer),
        _layer_spec(w_all.shape[1:], w_layer),
    ]
    args = [x, mods, g_all, w_all]
    for e in extras:
        in_specs.append(pl.BlockSpec(e.shape, const))
        args.append(e)
    if rope_tabs is not None:
        for t in rope_tabs:
            in_specs.append(pl.BlockSpec((tm, LANES), lambda i: (i % seq_blocks, 0)))
            args.append(t)
    return pl.pallas_call(
        functools.partial(kernel, rope=rope_tabs is not None),
        out_shape=[jax.ShapeDtypeStruct((rows, wd), dt) for wd, dt in zip(out_widths, out_dtypes)],
        grid=(rows // tm,),
        in_specs=in_specs,
        out_specs=[pl.BlockSpec((tm, wd), lambda i: (i, 0)) for wd in out_widths],
        compiler_params=_cparams(("parallel",)),
        name=name,
    )(*args)


def _place_heads(out_ref, col0, heads, kv_half_of, low):
    for m in range(len(heads) // 2):
        a = heads[2 * m]
        if kv_half_of(2 * m) == 1:
            a = pltpu.roll(a, HEAD_DIM, 1)
        b = heads[2 * m + 1]
        if kv_half_of(2 * m + 1) == 0:
            b = pltpu.roll(b, HEAD_DIM, 1)
        out_ref[:, col0 + m * LANES:col0 + (m + 1) * LANES] = jnp.where(low, a, b).astype(out_ref.dtype)


def _stack_heads(q_ref, heads):
    return jnp.concatenate([q_ref[:, h * LANES:(h + 1) * LANES] for h in heads], axis=0)


def _softmax_pv(scores, values, sink=None):
    m = scores[0].max(axis=-1, keepdims=True)
    for s in scores[1:]:
        m = jnp.maximum(m, s.max(axis=-1, keepdims=True))
    if sink is not None:
        m = jnp.maximum(m, sink)
    den = None
    acc = None
    for s, v in zip(scores, values):
        e = jnp.exp(s - m)
        d = e.sum(axis=-1, keepdims=True)
        o = _dot(e.astype(BF16), v)
        den = d if den is None else den + d
        acc = o if acc is None else acc + o
    if sink is not None:
        den = den + jnp.exp(sink - m)
    return acc / den


def _sink_column(sink_ref, heads, rows_per_head):
    row = lax.broadcasted_iota(jnp.int32, (len(heads) * rows_per_head, 1), 0)
    col = jnp.full(row.shape, sink_ref[heads[-1]], F32)
    for g in range(len(heads) - 2, -1, -1):
        col = jnp.where(row < (g + 1) * rows_per_head, sink_ref[heads[g]], col)
    return col


def _ctx_even_attn_seq(q_ref, k_ref, v_ref, y_ref):
    rows = q_ref.shape[0]
    low = _low_half(rows)
    grp = B_HEADS // B_KV
    outs = []
    for kv in range(B_KV):
        sl = slice((kv // 2) * LANES, (kv // 2 + 1) * LANES)
        ks = k_ref[:, sl].astype(BF16)
        vs = v_ref[:, sl].astype(BF16)
        heads = list(range(kv * grp, (kv + 1) * grp))
        o = _softmax_pv([_dot_nt(_stack_heads(q_ref, heads), ks)], [vs])
        outs += [o[g * rows:(g + 1) * rows] for g in range(grp)]
    _place_heads(y_ref, 0, outs, lambda hd: (hd // grp) % 2, low)


def _ctx_odd_attn_seq(sink_ref, qc_ref, kc_ref, vc_ref, qd_ref, kd_ref, vd_ref, y_ref):
    rows = qc_ref.shape[0]
    low = _low_half(rows)
    grp = C_HEADS // C_KV
    ks = kc_ref[...].astype(BF16)
    vs = vc_ref[...].astype(BF16)
    outs = []
    for kv in range(C_KV):
        heads = list(range(kv * grp, (kv + 1) * grp))
        sink = _sink_column(sink_ref, heads, rows)
        o = _softmax_pv([_dot_nt(_stack_heads(qc_ref, heads), ks)], [vs], sink)
        outs += [o[g * rows:(g + 1) * rows] for g in range(grp)]
    _place_heads(y_ref, 0, outs, lambda hd: hd // grp, low)
    outs = []
    for hd in range(D_HEADS):
        sl = slice((hd // 2) * LANES, (hd // 2 + 1) * LANES)
        s = _dot_nt(qd_ref[:, hd * LANES:(hd + 1) * LANES], kd_ref[:, sl].astype(BF16))
        outs.append(_softmax_pv([s], [vd_ref[:, sl].astype(BF16)]))
    _place_heads(y_ref, C_HEADS * HEAD_DIM, outs, lambda hd: hd % 2, low)


def _per_sequence(*refs, fn, seq, n_scalar=0):
    for i in range(refs[n_scalar].shape[0] // seq):
        rows = slice(i * seq, (i + 1) * seq)
        fn(*refs[:n_scalar], *[r.at[rows] for r in refs[n_scalar:]])


def _ctx_even_attn(q, k, v, seq):
    rows = q.shape[0]
    blk = lambda wd: pl.BlockSpec((CTX_SEQS * seq, wd), lambda b: (b, 0))
    return pl.pallas_call(
        functools.partial(_per_sequence, fn=_ctx_even_attn_seq, seq=seq),
        out_shape=jax.ShapeDtypeStruct((rows, B_HEADS * HEAD_DIM), BF16),
        grid=(rows // (CTX_SEQS * seq),),
        in_specs=[blk(q.shape[1]), blk(k.shape[1]), blk(v.shape[1])],
        out_specs=blk(B_HEADS * HEAD_DIM),
        compiler_params=_cparams(("parallel",)),
        name="ctx_even_attn",
    )(q, k, v)


def _ctx_odd_attn(sink_all, layer, qc, kc, vc, qd, kd, vd, seq):
    rows = qc.shape[0]
    blk = lambda wd: pl.BlockSpec((CTX_SEQS * seq, wd), lambda b: (b, 0))
    return pl.pallas_call(
        functools.partial(_per_sequence, fn=_ctx_odd_attn_seq, seq=seq, n_scalar=1),
        out_shape=jax.ShapeDtypeStruct((rows, D_MODEL), BF16),
        grid=(rows // (CTX_SEQS * seq),),
        in_specs=[pl.BlockSpec(memory_space=pltpu.SMEM)] + [blk(a.shape[1]) for a in (qc, kc, vc, qd, kd, vd)],
        out_specs=blk(D_MODEL),
        compiler_params=_cparams(("parallel",)),
        name="ctx_odd_attn",
    )(sink_all[layer], qc, kc, vc, qd, kd, vd)


def _pool_kernel(ap_ref, a_ref, an_ref, w_ref, ps_ref, y_ref, *, seq):
    tm = a_ref.shape[0]
    i = pl.program_id(0)
    pos0 = (i * tm) % seq
    a = a_ref[...]
    a_ext = jnp.concatenate([ap_ref[...], a, an_ref[...]], axis=0).astype(BF16)
    ext = tm + 2 * HALO
    r = lax.broadcasted_iota(jnp.int32, (tm, ext), 0)
    c = lax.broadcasted_iota(jnp.int32, (tm, ext), 1)
    pos_c = pos0 - HALO + c
    in_seq = (pos_c >= 0) & (pos_c < seq)
    d = c - HALO - r
    pos_r = pos0 + lax.broadcasted_iota(jnp.int32, (tm, POOL_WIDTH), 0)
    grp = lax.broadcasted_iota(jnp.int32, (tm, POOL_WIDTH), 1) // (POOL_WIDTH // len(POOL_WINDOWS))
    mean = jnp.zeros((tm, POOL_WIDTH), F32)
    for gi, wdw in enumerate(POOL_WINDOWS):
        band = jnp.where(in_seq & (d >= -(wdw // 2)) & (d < wdw - wdw // 2), 1.0, 0.0).astype(BF16)
        lo = jnp.maximum(pos_r - wdw // 2, 0)
        hi = jnp.minimum(pos_r - wdw // 2 + wdw, seq)
        win_mean = _dot(band, a_ext) / (hi - lo).astype(F32)
        mean = jnp.where(grp == gi, win_mean, mean)
    pooled = (mean - a).astype(BF16)
    y_ref[...] = (_dot(pooled, w_ref[...]) * ps_ref[...]).astype(y_ref.dtype)


def _halo_specs(tm, width, n_rows):
    per = tm // HALO
    last = n_rows // HALO - 1
    prev = pl.BlockSpec((HALO, width), lambda i: (jnp.maximum(i * per - 1, 0), 0))
    cur = pl.BlockSpec((tm, width), lambda i: (i, 0))
    nxt = pl.BlockSpec((HALO, width), lambda i: (jnp.minimum((i + 1) * per, last), 0))
    return [prev, cur, nxt]


def _pool_mixer(a, w_bd, pscale_all, layer, seq):
    rows = a.shape[0]
    tm = TM_POOL
    assert seq % tm == 0
    const = lambda i: (0, 0)
    return pl.pallas_call(
        functools.partial(_pool_kernel, seq=seq),
        out_shape=jax.ShapeDtypeStruct((rows, POOL_WIDTH), BF16),
        grid=(rows // tm,),
        in_specs=_halo_specs(tm, POOL_WIDTH, rows) + [
            pl.BlockSpec((POOL_WIDTH, POOL_WIDTH), const),
            _layer_spec((1, POOL_WIDTH), layer),
        ],
        out_specs=pl.BlockSpec((tm, POOL_WIDTH), lambda i: (i, 0)),
        compiler_params=_cparams(("parallel",)),
        name="pool_mixer",
    )(a, a, a, w_bd, pscale_all)


def _flash_kernel(q_ref, k_ref, v_ref, ck_ref, cv_ref, y_ref, m_ref, acc_ref):
    tq = q_ref.shape[0]
    low = _low_half(tq)
    grp = B_HEADS // B_KV
    n_chunks = k_ref.shape[0] // TK_FLASH
    qs = [_stack_heads(q_ref, list(range(j * grp, (j + 1) * grp))) for j in range(2)]
    m_ref[...] = jnp.full(m_ref.shape, NEG, F32)
    acc_ref[...] = jnp.zeros(acc_ref.shape, F32)

    def step(kc, vc):
        lowk = _low_half(vc.shape[0])
        for j in range(2):
            own = lowk if j == 0 else jnp.logical_not(lowk)
            vj = jnp.where(own, vc, jnp.ones_like(vc))
            s = _dot_nt(qs[j], kc)
            m_prev = m_ref[j]
            m_new = jnp.maximum(m_prev, s.max(axis=-1, keepdims=True))
            p = jnp.exp2(s - jnp.concatenate([m_new] * (s.shape[1] // LANES), axis=1))
            acc_ref[j] = jnp.exp2(m_prev - m_new) * acc_ref[j] + _dot(p.astype(BF16), vj)
            m_ref[j] = m_new

    def body(c, carry):
        off = pl.multiple_of(c * TK_FLASH, TK_FLASH)
        step(k_ref[pl.ds(off, TK_FLASH), :], v_ref[pl.ds(off, TK_FLASH), :])
        return carry

    lax.fori_loop(0, n_chunks, body, 0, unroll=4)
    step(ck_ref[...], cv_ref[...])
    outs = []
    for j in range(2):
        acc = acc_ref[j]
        o = acc / pltpu.roll(acc, HEAD_DIM, 1)
        outs += [o[g * tq:(g + 1) * tq] for g in range(grp)]
    _place_heads(y_ref, 0, outs, lambda hd: hd // grp, low)


def _flash_attn(q, k, v, ck_all, cv_all, layer, n_batch):
    rows = q.shape[0]
    seq = rows // n_batch
    past = ck_all.shape[2]
    tq = TQ_FLASH
    nq = seq // tq
    grp = B_HEADS // B_KV
    qw = 2 * grp * LANES
    yw = 2 * grp * HEAD_DIM
    ctx = pl.BlockSpec((None, None, past, LANES), lambda b, p, i: (b, layer, 0, p))
    return pl.pallas_call(
        _flash_kernel,
        out_shape=jax.ShapeDtypeStruct((rows, B_HEADS * HEAD_DIM), BF16),
        grid=(n_batch, B_KV // 2, nq),
        in_specs=[
            pl.BlockSpec((tq, qw), lambda b, p, i: (b * nq + i, p)),
            pl.BlockSpec((seq, LANES), lambda b, p, i: (b, p)),
            pl.BlockSpec((seq, LANES), lambda b, p, i: (b, p)),
            ctx, ctx,
        ],
        out_specs=pl.BlockSpec((tq, yw), lambda b, p, i: (b * nq + i, p)),
        scratch_shapes=[
            pltpu.VMEM((2, grp * tq, LANES), F32),
            pltpu.VMEM((2, grp * tq, LANES), F32),
        ],
        compiler_params=_cparams(("parallel", "parallel", "arbitrary")),
        name="latent_flash_attn",
    )(q, k, v, ck_all, cv_all)


def _window_kernel(sink_ref, q_ref, kp_ref, kc_ref, kn_ref, vp_ref, vc_ref, vn_ref, ck_ref, cv_ref, y_ref,
                   *, n_blocks):
    tq = q_ref.shape[0]
    n = pl.program_id(1)
    low = _low_half(tq)
    grp = C_HEADS // C_KV
    rows = grp * tq
    r = lax.broadcasted_iota(jnp.int32, (rows, C_WINDOW), 0) % tq
    c = lax.broadcasted_iota(jnp.int32, (rows, C_WINDOW), 1)
    prev_ok = (r - (c - C_WINDOW) <= C_WINDOW) & (n > 0)
    next_ok = ((c + tq) - r <= C_WINDOW) & (n < n_blocks - 1)
    r2 = lax.broadcasted_iota(jnp.int32, (rows, tq), 0) % tq
    c2 = lax.broadcasted_iota(jnp.int32, (rows, tq), 1)
    cur_ok = jnp.abs(r2 - c2) <= C_WINDOW
    outs = []
    for kv in range(C_KV):
        heads = list(range(kv * grp, (kv + 1) * grp))
        q = _stack_heads(q_ref, heads)
        sink = _sink_column(sink_ref, heads, tq)
        scores = [
            jnp.where(prev_ok, _dot_nt(q, kp_ref[...]), NEG),
            jnp.where(cur_ok, _dot_nt(q, kc_ref[...]), NEG),
            jnp.where(next_ok, _dot_nt(q, kn_ref[...]), NEG),
            _dot_nt(q, ck_ref[...]),
        ]
        o = _softmax_pv(scores, [vp_ref[...], vc_ref[...], vn_ref[...], cv_ref[...]], sink)
        outs += [o[g * tq:(g + 1) * tq] for g in range(grp)]
    _place_heads(y_ref, 0, outs, lambda hd: hd // grp, low)


def _window_attn(sink_all, q, k, v, ck_all, cv_all, layer, n_batch):
    rows = q.shape[0]
    seq = rows // n_batch
    past = ck_all.shape[2]
    tq = TQ_WIN
    assert tq % C_WINDOW == 0 and seq % tq == 0
    nb = seq // tq
    per = tq // C_WINDOW
    nw = seq // C_WINDOW
    prev = pl.BlockSpec((C_WINDOW, LANES), lambda b, n: (b * nw + jnp.maximum(n * per - 1, 0), 0))
    cur = pl.BlockSpec((tq, LANES), lambda b, n: (b * nb + n, 0))
    nxt = pl.BlockSpec((C_WINDOW, LANES), lambda b, n: (b * nw + jnp.minimum((n + 1) * per, nw - 1), 0))
    ctx = pl.BlockSpec((None, None, past, LANES), lambda b, n: (b, layer, 0, 0))
    return pl.pallas_call(
        functools.partial(_window_kernel, n_blocks=nb),
        out_shape=jax.ShapeDtypeStruct((rows, C_HEADS * HEAD_DIM), BF16),
        grid=(n_batch, nb),
        in_specs=[
            pl.BlockSpec(memory_space=pltpu.SMEM),
            pl.BlockSpec((tq, C_HEADS * LANES), lambda b, n: (b * nb + n, 0)),
            prev, cur, nxt, prev, cur, nxt, ctx, ctx,
        ],
        out_specs=pl.BlockSpec((tq, C_HEADS * HEAD_DIM), lambda b, n: (b * nb + n, 0)),
        compiler_params=_cparams(("parallel", "parallel")),
        name="latent_window_attn",
    )(sink_all[layer], q, k, k, k, v, v, v, ck_all, cv_all)


def _na_window_start(blk, n_rows):
    return jnp.clip(blk * NA_QROWS - NA_ROWS // 2, 0, n_rows - NA_WIN)


def _na_kernel(q_ref, k_ref, v_ref, ck_ref, cv_ref, pair_ref, y_ref, *, n_rows):
    blk = pl.program_id(1)
    tq = NA_QROWS * GRID_W
    win = NA_WIN * GRID_W
    low = _low_half(tq)
    ws = _na_window_start(blk, n_rows)
    start = pl.multiple_of(ws * GRID_W, GRID_W)
    q_row = blk * NA_QROWS + lax.broadcasted_iota(jnp.int32, (tq, win), 0) // GRID_W
    k_row = ws + lax.broadcasted_iota(jnp.int32, (tq, win), 1) // GRID_W
    rs = jnp.clip(q_row - NA_ROWS // 2, 0, n_rows - NA_ROWS)
    row_mask = jnp.where((k_row >= rs) & (k_row < rs + NA_ROWS), 0.0, NEG)
    outs = []
    for hd in range(D_HEADS):
        sl = slice((hd // 2) * LANES, (hd // 2 + 1) * LANES)
        q = q_ref[:, hd * LANES:(hd + 1) * LANES]
        kw = k_ref[pl.ds(start, win), sl]
        vw = v_ref[pl.ds(start, win), sl]
        bias = jnp.concatenate([
            jnp.concatenate([
                pair_ref[hd, jnp.clip(ws + 2 * jp - (blk * NA_QROWS + a) + NA_ROWS, 0, 2 * NA_ROWS)]
                for jp in range(NA_WIN // 2)], axis=1)
            for a in range(NA_QROWS)], axis=0)
        s_loc = _dot_nt(q, kw) + bias + row_mask
        s_ctx = _dot_nt(q, ck_ref[:, sl])
        outs.append(_softmax_pv([s_loc, s_ctx], [vw, cv_ref[:, sl]]))
    _place_heads(y_ref, 0, outs, lambda hd: hd % 2, low)


def _na_attn(q, k, v, ck_all, cv_all, pair_all, layer, n_batch):
    rows = q.shape[0]
    seq = rows // n_batch
    past = ck_all.shape[2]
    n_rows = seq // GRID_W
    n_blk = n_rows // NA_QROWS
    assert n_rows >= NA_WIN and NA_WIN % 2 == 0
    width = D_HEADS * HEAD_DIM
    tq = NA_QROWS * GRID_W
    full = pl.BlockSpec((seq, width), lambda b, r: (b, 0))
    ctx = pl.BlockSpec((None, None, past, width), lambda b, r: (b, layer, 0, 0))
    return pl.pallas_call(
        functools.partial(_na_kernel, n_rows=n_rows),
        out_shape=jax.ShapeDtypeStruct((rows, width), BF16),
        grid=(n_batch, n_blk),
        in_specs=[
            pl.BlockSpec((tq, D_HEADS * LANES), lambda b, r: (b * n_blk + r, 0)),
            full, full, ctx, ctx,
            _layer_spec(pair_all.shape[1:], layer),
        ],
        out_specs=pl.BlockSpec((tq, width), lambda b, r: (b * n_blk + r, 0)),
        compiler_params=_cparams(("parallel", "arbitrary")),
        name="latent_neighbourhood_attn",
    )(q, k, v, ck_all, cv_all, pair_all)


def _na_pair_tiles(rpb_all):
    n_l, n_h, n_y, n_x = rpb_all.shape
    n_tiles = 2 * NA_ROWS + 1
    zero = jnp.zeros((n_l, n_h, 1, n_x), F32)
    rows = jnp.concatenate([zero, rpb_all, zero, zero], axis=2)
    feats = jnp.concatenate([rows[:, :, :-1], rows[:, :, 1:]], axis=-1)
    feat = np.arange(2 * n_x)[:, None, None]
    qc = np.arange(GRID_W)[None, :, None]
    lane = np.arange(2 * GRID_W)[None, None, :]
    kc = lane % GRID_W
    cs = np.clip(qc - NA_COLS // 2, 0, GRID_W - NA_COLS)
    inside = (kc >= cs) & (kc < cs + NA_COLS)
    select = (lane // GRID_W == feat // n_x) & (feat % n_x == kc - qc + NA_COLS - 1) & inside
    iy = np.arange(n_tiles)[:, None, None] - 1 + lane // GRID_W
    valid = inside & (iy >= 0) & (iy < n_y)
    tiles = jnp.einsum('lhef,fqx->lheqx', feats, jnp.asarray(select, F32), precision=lax.Precision.HIGHEST)
    return jnp.where(valid, tiles, NEG)


def _out_proj_kernel(*refs, widths):
    x_ref, mod_ref, g_ref, w_ref = refs[:4]
    parts = refs[4:4 + len(widths)]
    o_ref = refs[4 + len(widths)]
    y = None
    off = 0
    for p_ref, wd in zip(parts, widths):
        t = _dot(p_ref[...], w_ref[off:off + wd, :])
        y = t if y is None else y + t
        off += wd
    o_ref[...] = x_ref[...] + mod_ref[2:3, :] * _rmsnorm(y, g_ref[...])


def _out_proj(x, mods, layer, g_all, w_all, parts, seq_blocks):
    rows = x.shape[0]
    tm = TM_PROJ
    widths = tuple(p.shape[1] for p in parts)
    return pl.pallas_call(
        functools.partial(_out_proj_kernel, widths=widths),
        out_shape=jax.ShapeDtypeStruct((rows, D_MODEL), F32),
        grid=(rows // tm,),
        in_specs=[
            pl.BlockSpec((tm, D_MODEL), lambda i: (i, 0)),
            _mod_spec(layer, seq_blocks),
            _layer_spec((1, D_MODEL), layer),
            _layer_spec((D_MODEL, D_MODEL), layer),
        ] + [pl.BlockSpec((tm, wd), lambda i: (i, 0)) for wd in widths],
        out_specs=pl.BlockSpec((tm, D_MODEL), lambda i: (i, 0)),
        compiler_params=_cparams(("parallel",)),
        name="mixer_out_proj",
    )(x, mods, g_all, w_all, *parts)


def _ffn_kernel(*refs, seq, halo):
    if halo:
        xp_ref, x_ref, xn_ref = refs[:3]
    else:
        x_ref = refs[0]
    mod_ref, gpre_ref, gpost_ref, wi_ref, wc_ref, wo_ref, o_ref, h_ref = refs[-8:]
    tm = x_ref.shape[0]
    pad = HALO if halo else 0
    ext = tm + 2 * pad
    scale, shift = mod_ref[4:5, :], mod_ref[3:4, :]
    g = gpre_ref[...]
    h_ref[pad:pad + tm, :] = _norm_mod(x_ref[...], g, scale, shift).astype(BF16)
    if halo:
        pos0 = (pl.program_id(0) * tm) % seq
        keep_prev = jnp.where(pos0 > 0, 1.0, 0.0)
        keep_next = jnp.where(pos0 + tm < seq, 1.0, 0.0)
        h_ref[0:pad, :] = (_norm_mod(xp_ref[...], g, scale, shift) * keep_prev).astype(BF16)
        h_ref[pad + tm:, :] = (_norm_mod(xn_ref[...], g, scale, shift) * keep_next).astype(BF16)
    sub = lax.broadcasted_iota(jnp.int32, (8, FF_CHUNK), 0)

    def conv_up(c0):
        u = _dot(h_ref[...], wi_ref[:, c0:c0 + FF_CHUNK])
        up = pltpu.roll(u, 1, 0)[pad:pad + tm]
        un = pltpu.roll(u, ext - 1, 0)[pad:pad + tm]
        if not halo:
            up = jnp.concatenate([jnp.where(sub == 0, 0.0, up[:8]), up[8:]], axis=0)
            un = jnp.concatenate([un[:-8], jnp.where(sub == 7, 0.0, un[-8:])], axis=0)
        return (u[pad:pad + tm] * wc_ref[1:2, c0:c0 + FF_CHUNK] + up * wc_ref[0:1, c0:c0 + FF_CHUNK]
                + un * wc_ref[2:3, c0:c0 + FF_CHUNK])

    y = None
    for j in range(D_FF // FF_CHUNK):
        gate = conv_up(j * FF_CHUNK)
        val = conv_up(D_FF + j * FF_CHUNK)
        act = gate * (1.0 / (1.0 + jnp.exp(-gate))) * val
        contrib = _dot(act.astype(BF16), wo_ref[j * FF_CHUNK:(j + 1) * FF_CHUNK, :])
        y = contrib if y is None else y + contrib
    o_ref[...] = x_ref[...] + mod_ref[5:6, :] * _rmsnorm(y, gpost_ref[...])


def _conv_ffn(x, mods, layer, g_pre_all, g_post_all, wi_all, wc_all, wo_all, seq, seq_blocks):
    rows = x.shape[0]
    tm = min(TM_FFN, seq)
    assert seq % tm == 0 and D_FF % FF_CHUNK == 0
    halo = tm < seq
    resident = dict(pipeline_mode=pl.Buffered(1))
    zeros3 = lambda i: (layer, 0, 0)
    x_specs = _halo_specs(tm, D_MODEL, rows) if halo else [pl.BlockSpec((tm, D_MODEL), lambda i: (i, 0))]
    return pl.pallas_call(
        functools.partial(_ffn_kernel, seq=seq, halo=halo),
        out_shape=jax.ShapeDtypeStruct((rows, D_MODEL), F32),
        grid=(rows // tm,),
        in_specs=x_specs + [
            _mod_spec(layer, seq_blocks),
            _layer_spec((1, D_MODEL), layer),
            _layer_spec((1, D_MODEL), layer),
            pl.BlockSpec((None, D_MODEL, 2 * D_FF), zeros3, **resident),
            _layer_spec((3, 2 * D_FF), layer),
            pl.BlockSpec((None, D_FF, D_MODEL), zeros3, **resident),
        ],
        out_specs=pl.BlockSpec((tm, D_MODEL), lambda i: (i, 0)),
        scratch_shapes=[pltpu.VMEM((tm + (2 * HALO if halo else 0), D_MODEL), BF16)],
        compiler_params=_cparams(("parallel",)),
        name="conv_ffn",
    )(*([x] * len(x_specs)), mods, g_pre_all, g_post_all, wi_all, wc_all, wo_all)


def _rope_tables(n_tokens):
    t = jnp.arange(n_tokens)
    row = (t // GRID_W).astype(F32)
    col = (t % GRID_W).astype(F32)
    axis_dim = HEAD_DIM // 2
    inv_freq = ROPE_THETA ** (-jnp.arange(0, axis_dim, 2, dtype=F32) / axis_dim)
    ang = jnp.concatenate([row[:, None] * inv_freq, col[:, None] * inv_freq], axis=-1)
    cos = jnp.repeat(jnp.cos(ang), 2, axis=-1)
    sin = jnp.sin(ang)
    sin = jnp.stack([-sin, sin], axis=-1).reshape(n_tokens, HEAD_DIM)
    return jnp.tile(cos, (1, 2)), jnp.tile(sin, (1, 2))


def _rows3(p):
    return p.reshape(p.shape[0], 1, p.shape[1])


def kernel(x_prompt, x_sample, cache_b_k, cache_b_v, cache_c_k, cache_c_v, cache_d_k, cache_d_v, c, c_ctx,
           w_mod, b_mod, g_mix_pre, g_mix_post, g_ffn_pre, g_ffn_post, w_in_even, w_pool, pool_scale,
           g_q_b, g_k_b, w_in_odd, sink_c, rpb_d, w_mix_out, w_ffn_in, w_ffn_conv, w_ffn_out):
    n_ctx, ctx_seq, _ = x_prompt.shape
    n_lat, lat_seq, _ = x_sample.shape
    past = cache_b_k.shape[2]
    xp = x_prompt.reshape(n_ctx * ctx_seq, D_MODEL)
    xs = x_sample.reshape(n_lat * lat_seq, D_MODEL)
    lat_blocks = lat_seq // TM_PROJ

    cvecs = jnp.concatenate([c_ctx[None, :], c, jnp.zeros((8 - 1 - n_lat, D_MODEL), F32)], axis=0)
    mods = _modulation(cvecs, w_mod, b_mod).reshape(DEPTH, 8, 6, D_MODEL)

    rope_tabs = _rope_tables(lat_seq)
    blk = jnp.arange(LANES) // HEAD_DIM
    e_mat = jnp.where(blk[:, None] == blk[None, :], 1.0 / HEAD_DIM, 0.0).astype(BF16)

    w_even = w_in_even.astype(BF16)
    w_odd = w_in_odd.astype(BF16)
    w_out = w_mix_out.astype(BF16)
    w_fi = w_ffn_in.astype(BF16)
    w_fo = w_ffn_out.astype(BF16)
    slabs = lambda t: t.astype(BF16).reshape(t.shape[0], t.shape[1], past, t.shape[3] * HEAD_DIM)
    cbk, cbv, cck, ccv, cdk, cdv = map(slabs, (cache_b_k, cache_b_v, cache_c_k, cache_c_v, cache_d_k, cache_d_v))
    g_pre, g_post = _rows3(g_mix_pre), _rows3(g_mix_post)
    gf_pre, gf_post = _rows3(g_ffn_pre), _rows3(g_ffn_post)
    pscale = _rows3(pool_scale)
    na_pairs = _na_pair_tiles(rpb_d)

    b_k, b_v, c_k, c_v, d_k, d_v = [], [], [], [], [], []
    for l in range(DEPTH):
        i = l // 2
        if l % 2 == 0:
            gq = jnp.tile(g_q_b[i], 2)[None, :]
            gk = jnp.tile(g_k_b[i], 2)[None, :]
            kvw = B_KV * HEAD_DIM
            widths = (POOL_WIDTH, B_HEADS * LANES, kvw, kvw)
            a_p, q_p, k_p, v_p = _proj_in(xp, mods, l, g_pre, w_even, i, (gq, gk, e_mat), widths,
                                          (F32, BF16, F32, F32),
                                          functools.partial(_proj_even_kernel, q_scale=QK_SCALE), None, None,
                                          "proj_even_ctx")
            a_s, q_s, k_s, v_s = _proj_in(xs, mods, l, g_pre, w_even, i, (gq, gk, e_mat), widths,
                                          (F32, BF16, BF16, BF16),
                                          functools.partial(_proj_even_kernel, q_scale=QK_SCALE * LOG2E),
                                          lat_blocks, rope_tabs,
                                          "proj_even_lat")
            b_k.append(k_p.reshape(n_ctx, ctx_seq, B_KV, HEAD_DIM))
            b_v.append(v_p.reshape(n_ctx, ctx_seq, B_KV, HEAD_DIM))
            w_bd = jax.scipy.linalg.block_diag(*[w_pool[i, g] for g in range(len(POOL_WINDOWS))]).astype(BF16)
            ya_p = _pool_mixer(a_p, w_bd, pscale, i, ctx_seq)
            ya_s = _pool_mixer(a_s, w_bd, pscale, i, lat_seq)
            yb_p = _ctx_even_attn(q_p, k_p, v_p, ctx_seq)
            yb_s = _flash_attn(q_s, k_s, v_s, cbk, cbv, i, n_lat)
            parts_p, parts_s = (ya_p, yb_p), (ya_s, yb_s)
        else:
            ckw, dw = C_KV * HEAD_DIM, D_HEADS * HEAD_DIM
            widths = (C_HEADS * LANES, ckw, ckw, D_HEADS * LANES, dw, dw)
            qc_p, kc_p, vc_p, qd_p, kd_p, vd_p = _proj_in(
                xp, mods, l, g_pre, w_odd, i, (), widths, (BF16, F32, F32, BF16, F32, F32),
                _proj_odd_kernel, None, None, "proj_odd_ctx")
            qc_s, kc_s, vc_s, qd_s, kd_s, vd_s = _proj_in(
                xs, mods, l, g_pre, w_odd, i, (), widths, (BF16,) * 6,
                _proj_odd_kernel, lat_blocks, rope_tabs, "proj_odd_lat")
            c_k.append(kc_p.reshape(n_ctx, ctx_seq, C_KV, HEAD_DIM))
            c_v.append(vc_p.reshape(n_ctx, ctx_seq, C_KV, HEAD_DIM))
            d_k.append(kd_p.reshape(n_ctx, ctx_seq, D_HEADS, HEAD_DIM))
            d_v.append(vd_p.reshape(n_ctx, ctx_seq, D_HEADS, HEAD_DIM))
            y_p = _ctx_odd_attn(sink_c, i, qc_p, kc_p, vc_p, qd_p, kd_p, vd_p, ctx_seq)
            yc_s = _window_attn(sink_c, qc_s, kc_s, vc_s, cck, ccv, i, n_lat)
            yd_s = _na_attn(qd_s, kd_s, vd_s, cdk, cdv, na_pairs, i, n_lat)
            parts_p, parts_s = (y_p,), (yc_s, yd_s)

        xp = _out_proj(xp, mods, l, g_post, w_out, parts_p, None)
        xs = _out_proj(xs, mods, l, g_post, w_out, parts_s, lat_blocks)
        xp = _conv_ffn(xp, mods, l, gf_pre, gf_post, w_fi, w_ffn_conv, w_fo, ctx_seq, None)
        xs = _conv_ffn(xs, mods, l, gf_pre, gf_post, w_fi, w_ffn_conv, w_fo, lat_seq, lat_seq // TM_FFN)

    return (xp.reshape(n_ctx, ctx_seq, D_MODEL), xs.reshape(n_lat, lat_seq, D_MODEL),
            jnp.stack(b_k, axis=1), jnp.stack(b_v, axis=1),
            jnp.stack(c_k, axis=1), jnp.stack(c_v, axis=1),
            jnp.stack(d_k, axis=1), jnp.stack(d_v, axis=1))
```

```python
import functools

import jax
import jax.numpy as jnp
import numpy as np
from jax import lax
from jax.experimental import pallas as pl
from jax.experimental.pallas import tpu as pltpu

F32 = jnp.float32
BF16 = jnp.bfloat16

D_MODEL = 1024
DEPTH = 4
GRID_W = 64
HEAD_DIM = 64
EPS = 1e-6
ROPE_THETA = 10000.0
POOL_WINDOWS = (2, 4, 8, 16)
POOL_WIDTH = 256
B_HEADS, B_KV = 12, 4
C_HEADS, C_KV = 8, 2
D_HEADS = 8
C_WINDOW = 128
NA_ROWS, NA_COLS = 8, 16
D_FF = 2816
QK_SCALE = HEAD_DIM ** -0.5
LOG2E = 1.4426950408889634
NEG = -1e30

LANES = 128
HALO = 16
VMEM_LIMIT = 56 * 1024 * 1024

TM_PROJ = 1024
TM_FFN = 512
TM_POOL = 256
MXU_TILE = 256
_FF_TILES = D_FF // MXU_TILE
FF_CHUNKS = ((0, (_FF_TILES + 1) // 2 * MXU_TILE),
             ((_FF_TILES + 1) // 2 * MXU_TILE, _FF_TILES // 2 * MXU_TILE))
assert D_FF % MXU_TILE == 0 and sum(w for _, w in FF_CHUNKS) == D_FF
TQ_FLASH = 256
TK_FLASH = 512
TQ_WIN = 256
CTX_SEQS = 2
NA_QROWS = 4
NA_WIN = NA_ROWS + NA_QROWS


def _cparams(sem):
    return pltpu.CompilerParams(dimension_semantics=sem, vmem_limit_bytes=VMEM_LIMIT)


def _norm_mod(x, g, scale, shift):
    ms = jnp.mean(x * x, axis=-1, keepdims=True)
    return (x * lax.rsqrt(ms + EPS) * g) * (1.0 + scale) + shift


def _rmsnorm(x, g):
    ms = jnp.mean(x * x, axis=-1, keepdims=True)
    return x * lax.rsqrt(ms + EPS) * g


def _dot(a, b):
    return jnp.dot(a, b, preferred_element_type=F32)


def _dot_nt(a, b):
    return lax.dot_general(a, b, (((1,), (1,)), ((), ())), preferred_element_type=F32)


def _low_half(rows):
    return lax.broadcasted_iota(jnp.int32, (rows, LANES), 1) < HEAD_DIM


def _layer_spec(shape, layer):
    zeros = (0,) * len(shape)
    return pl.BlockSpec((None,) + tuple(shape), lambda *_: (layer,) + zeros)


def _mod_spec(layer, seq_blocks):
    if seq_blocks is None:
        return pl.BlockSpec((None, None, 6, D_MODEL), lambda i: (layer, 0, 0, 0))
    return pl.BlockSpec((None, None, 6, D_MODEL), lambda i: (layer, 1 + i // seq_blocks, 0, 0))


def _mod_kernel(c_ref, w_ref, b_ref, o_ref):
    cv = c_ref[...]
    s = cv * (1.0 / (1.0 + jnp.exp(-cv)))
    o_ref[...] = _dot(s.astype(BF16), w_ref[...].astype(BF16)) + b_ref[...]


def _modulation(cvecs, w_mod, b_mod):
    tn = 1536
    return pl.pallas_call(
        _mod_kernel,
        out_shape=jax.ShapeDtypeStruct((DEPTH, 8, 6 * D_MODEL), F32),
        grid=(DEPTH, 6 * D_MODEL // tn),
        in_specs=[
            pl.BlockSpec((8, D_MODEL), lambda l, j: (0, 0)),
            pl.BlockSpec((None, D_MODEL, tn), lambda l, j: (l, 0, j)),
            pl.BlockSpec((None, 1, tn), lambda l, j: (l, 0, j)),
        ],
        out_specs=pl.BlockSpec((None, 8, tn), lambda l, j: (l, 0, j)),
        compiler_params=_cparams(("parallel", "parallel")),
        name="modulation",
    )(cvecs, w_mod, b_mod.reshape(DEPTH, 1, 6 * D_MODEL))


def _pair_swap(s):
    lane = lax.broadcasted_iota(jnp.int32, s.shape, 1)
    n = s.shape[1]
    return jnp.where(lane % 2 == 0, pltpu.roll(s, n - 1, 1), pltpu.roll(s, 1, 1))


def _store_expanded_q(q_ref, s, slab, kv_half_of, low):
    sr = pltpu.roll(s, HEAD_DIM, 1)
    for half in (0, 1):
        hd = 2 * slab + half
        dst = kv_half_of(hd)
        val = s if dst == half else sr
        keep = low if dst == 0 else jnp.logical_not(low)
        q_ref[:, hd * LANES:(hd + 1) * LANES] = jnp.where(keep, val, 0.0).astype(BF16)


def _proj_even_kernel(*refs, rope, q_scale):
    if rope:
        (x_ref, mod_ref, g_ref, w_ref, gq_ref, gk_ref, e_ref, cos_ref, sin_ref,
         a_ref, q_ref, k_ref, v_ref) = refs
    else:
        (x_ref, mod_ref, g_ref, w_ref, gq_ref, gk_ref, e_ref,
         a_ref, q_ref, k_ref, v_ref) = refs
    rows = x_ref.shape[0]
    h = _norm_mod(x_ref[...], g_ref[...], mod_ref[1:2, :], mod_ref[0:1, :])
    u = _dot(h.astype(BF16), w_ref[...])
    low = _low_half(rows)
    a_ref[...] = u[:, :POOL_WIDTH]

    def headnorm(s, g):
        ms = _dot((s * s).astype(BF16), e_ref[...])
        return s * lax.rsqrt(ms + EPS) * g

    def rot(s):
        if not rope:
            return s
        return s * cos_ref[...] + _pair_swap(s) * sin_ref[...]

    q0 = POOL_WIDTH
    for j in range(B_HEADS // 2):
        s = u[:, q0 + LANES * j:q0 + LANES * (j + 1)]
        s = rot(headnorm(s, gq_ref[...])) * q_scale
        _store_expanded_q(q_ref, s, j, lambda hd: (hd // (B_HEADS // B_KV)) % 2, low)
    k0 = q0 + B_HEADS * HEAD_DIM
    for j in range(B_KV // 2):
        s = headnorm(u[:, k0 + LANES * j:k0 + LANES * (j + 1)], gk_ref[...])
        k_ref[:, LANES * j:LANES * (j + 1)] = rot(s).astype(k_ref.dtype)
    v0 = k0 + B_KV * HEAD_DIM
    v_ref[...] = u[:, v0:v0 + B_KV * HEAD_DIM].astype(v_ref.dtype)


def _proj_odd_kernel(*refs, rope):
    if rope:
        (x_ref, mod_ref, g_ref, w_ref, cos_ref, sin_ref,
         qc_ref, kc_ref, vc_ref, qd_ref, kd_ref, vd_ref) = refs
    else:
        (x_ref, mod_ref, g_ref, w_ref,
         qc_ref, kc_ref, vc_ref, qd_ref, kd_ref, vd_ref) = refs
    rows = x_ref.shape[0]
    h = _norm_mod(x_ref[...], g_ref[...], mod_ref[1:2, :], mod_ref[0:1, :])
    u = _dot(h.astype(BF16), w_ref[...])
    low = _low_half(rows)

    def rot(s):
        if not rope:
            return s
        return s * cos_ref[...] + _pair_swap(s) * sin_ref[...]

    for j in range(C_HEADS // 2):
        s = rot(u[:, LANES * j:LANES * (j + 1)]) * QK_SCALE
        _store_expanded_q(qc_ref, s, j, lambda hd: hd // (C_HEADS // C_KV), low)
    o = C_HEADS * HEAD_DIM
    kc_ref[...] = rot(u[:, o:o + LANES]).astype(kc_ref.dtype)
    o += C_KV * HEAD_DIM
    vc_ref[...] = u[:, o:o + LANES].astype(vc_ref.dtype)
    o += C_KV * HEAD_DIM
    for j in range(D_HEADS // 2):
        s = u[:, o + LANES * j:o + LANES * (j + 1)] * QK_SCALE
        _store_expanded_q(qd_ref, s, j, lambda hd: hd % 2, low)
    o += D_HEADS * HEAD_DIM
    kd_ref[...] = u[:, o:o + D_HEADS * HEAD_DIM].astype(kd_ref.dtype)
    o += D_HEADS * HEAD_DIM
    vd_ref[...] = u[:, o:o + D_HEADS * HEAD_DIM].astype(vd_ref.dtype)


def _proj_in(x, mods, layer, g_all, w_all, w_layer, extras, out_widths, out_dtypes, kernel, seq_blocks,
             rope_tabs, name):
    rows = x.shape[0]
    tm = TM_PROJ
    const = lambda i: (0, 0)
    in_specs = [
        pl.BlockSpec((tm, D_MODEL), lambda i: (i, 0)),
        _mod_spec(layer, seq_blocks),
        _layer_spec((1, D_MODEL), layer),
        _layer_spec(w_all.shape[1:], w_layer),
    ]
    args = [x, mods, g_all, w_all]
    for e in extras:
        in_specs.append(pl.BlockSpec(e.shape, const))
        args.append(e)
    if rope_tabs is not None:
        for t in rope_tabs:
            in_specs.append(pl.BlockSpec((tm, LANES), lambda i: (i % seq_blocks, 0)))
            args.append(t)
    return pl.pallas_call(
        functools.partial(kernel, rope=rope_tabs is not None),
        out_shape=[jax.ShapeDtypeStruct((rows, wd), dt) for wd, dt in zip(out_widths, out_dtypes)],
        grid=(rows // tm,),
        in_specs=in_specs,
        out_specs=[pl.BlockSpec((tm, wd), lambda i: (i, 0)) for wd in out_widths],
        compiler_params=_cparams(("parallel",)),
        name=name,
    )(*args)


def _place_heads(out_ref, col0, heads, kv_half_of, low):
    for m in range(len(heads) // 2):
        a = heads[2 * m]
        if kv_half_of(2 * m) == 1:
            a = pltpu.roll(a, HEAD_DIM, 1)
        b = heads[2 * m + 1]
        if kv_half_of(2 * m + 1) == 0:
            b = pltpu.roll(b, HEAD_DIM, 1)
        out_ref[:, col0 + m * LANES:col0 + (m + 1) * LANES] = jnp.where(low, a, b).astype(out_ref.dtype)


def _stack_heads(q_ref, heads):
    return jnp.concatenate([q_ref[:, h * LANES:(h + 1) * LANES] for h in heads], axis=0)


def _softmax_pv(scores, values, sink=None):
    m = scores[0].max(axis=-1, keepdims=True)
    for s in scores[1:]:
        m = jnp.maximum(m, s.max(axis=-1, keepdims=True))
    if sink is not None:
        m = jnp.maximum(m, sink)
    den = None
    acc = None
    for s, v in zip(scores, values):
        e = jnp.exp(s - m)
        d = e.sum(axis=-1, keepdims=True)
        o = _dot(e.astype(BF16), v)
        den = d if den is None else den + d
        acc = o if acc is None else acc + o
    if sink is not None:
        den = den + jnp.exp(sink - m)
    return acc / den


def _sink_column(sink_ref, heads, rows_per_head):
    row = lax.broadcasted_iota(jnp.int32, (len(heads) * rows_per_head, 1), 0)
    col = jnp.full(row.shape, sink_ref[heads[-1]], F32)
    for g in range(len(heads) - 2, -1, -1):
        col = jnp.where(row < (g + 1) * rows_per_head, sink_ref[heads[g]], col)
    return col


def _ctx_even_attn_seq(q_ref, k_ref, v_ref, y_ref):
    rows = q_ref.shape[0]
    low = _low_half(rows)
    grp = B_HEADS // B_KV
    outs = []
    for kv in range(B_KV):
        sl = slice((kv // 2) * LANES, (kv // 2 + 1) * LANES)
        ks = k_ref[:, sl].astype(BF16)
        vs = v_ref[:, sl].astype(BF16)
        heads = list(range(kv * grp, (kv + 1) * grp))
        o = _softmax_pv([_dot_nt(_stack_heads(q_ref, heads), ks)], [vs])
        outs += [o[g * rows:(g + 1) * rows] for g in range(grp)]
    _place_heads(y_ref, 0, outs, lambda hd: (hd // grp) % 2, low)


def _ctx_odd_attn_seq(sink_ref, qc_ref, kc_ref, vc_ref, qd_ref, kd_ref, vd_ref, y_ref):
    rows = qc_ref.shape[0]
    low = _low_half(rows)
    grp = C_HEADS // C_KV
    ks = kc_ref[...].astype(BF16)
    vs = vc_ref[...].astype(BF16)
    outs = []
    for kv in range(C_KV):
        heads = list(range(kv * grp, (kv + 1) * grp))
        sink = _sink_column(sink_ref, heads, rows)
        o = _softmax_pv([_dot_nt(_stack_heads(qc_ref, heads), ks)], [vs], sink)
        outs += [o[g * rows:(g + 1) * rows] for g in range(grp)]
    _place_heads(y_ref, 0, outs, lambda hd: hd // grp, low)
    outs = []
    for hd in range(D_HEADS):
        sl = slice((hd // 2) * LANES, (hd // 2 + 1) * LANES)
        s = _dot_nt(qd_ref[:, hd * LANES:(hd + 1) * LANES], kd_ref[:, sl].astype(BF16))
        outs.append(_softmax_pv([s], [vd_ref[:, sl].astype(BF16)]))
    _place_heads(y_ref, C_HEADS * HEAD_DIM, outs, lambda hd: hd % 2, low)


def _per_sequence(*refs, fn, seq, n_scalar=0):
    for i in range(refs[n_scalar].shape[0] // seq):
        rows = slice(i * seq, (i + 1) * seq)
        fn(*refs[:n_scalar], *[r.at[rows] for r in refs[n_scalar:]])


def _ctx_even_attn(q, k, v, seq):
    rows = q.shape[0]
    blk = lambda wd: pl.BlockSpec((CTX_SEQS * seq, wd), lambda b: (b, 0))
    return pl.pallas_call(
        functools.partial(_per_sequence, fn=_ctx_even_attn_seq, seq=seq),
        out_shape=jax.ShapeDtypeStruct((rows, B_HEADS * HEAD_DIM), BF16),
        grid=(rows // (CTX_SEQS * seq),),
        in_specs=[blk(q.shape[1]), blk(k.shape[1]), blk(v.shape[1])],
        out_specs=blk(B_HEADS * HEAD_DIM),
        compiler_params=_cparams(("parallel",)),
        name="ctx_even_attn",
    )(q, k, v)


def _ctx_odd_attn(sink_all, layer, qc, kc, vc, qd, kd, vd, seq):
    rows = qc.shape[0]
    blk = lambda wd: pl.BlockSpec((CTX_SEQS * seq, wd), lambda b: (b, 0))
    return pl.pallas_call(
        functools.partial(_per_sequence, fn=_ctx_odd_attn_seq, seq=seq, n_scalar=1),
        out_shape=jax.ShapeDtypeStruct((rows, D_MODEL), BF16),
        grid=(rows // (CTX_SEQS * seq),),
        in_specs=[pl.BlockSpec(memory_space=pltpu.SMEM)] + [blk(a.shape[1]) for a in (qc, kc, vc, qd, kd, vd)],
        out_specs=blk(D_MODEL),
        compiler_params=_cparams(("parallel",)),
        name="ctx_odd_attn",
    )(sink_all[layer], qc, kc, vc, qd, kd, vd)


def _pool_kernel(ap_ref, a_ref, an_ref, w_ref, ps_ref, y_ref, *, seq):
    tm = a_ref.shape[0]
    i = pl.program_id(0)
    pos0 = (i * tm) % seq
    a = a_ref[...]
    a_ext = jnp.concatenate([ap_ref[...], a, an_ref[...]], axis=0).astype(BF16)
    ext = tm + 2 * HALO
    r = lax.broadcasted_iota(jnp.int32, (tm, ext), 0)
    c = lax.broadcasted_iota(jnp.int32, (tm, ext), 1)
    pos_c = pos0 - HALO + c
    in_seq = (pos_c >= 0) & (pos_c < seq)
    d = c - HALO - r
    pos_r = pos0 + lax.broadcasted_iota(jnp.int32, (tm, POOL_WIDTH), 0)
    grp = lax.broadcasted_iota(jnp.int32, (tm, POOL_WIDTH), 1) // (POOL_WIDTH // len(POOL_WINDOWS))
    mean = jnp.zeros((tm, POOL_WIDTH), F32)
    for gi, wdw in enumerate(POOL_WINDOWS):
        band = jnp.where(in_seq & (d >= -(wdw // 2)) & (d < wdw - wdw // 2), 1.0, 0.0).astype(BF16)
        lo = jnp.maximum(pos_r - wdw // 2, 0)
        hi = jnp.minimum(pos_r - wdw // 2 + wdw, seq)
        win_mean = _dot(band, a_ext) / (hi - lo).astype(F32)
        mean = jnp.where(grp == gi, win_mean, mean)
    pooled = (mean - a).astype(BF16)
    y_ref[...] = (_dot(pooled, w_ref[...]) * ps_ref[...]).astype(y_ref.dtype)


def _halo_specs(tm, width, n_rows):
    per = tm // HALO
    last = n_rows // HALO - 1
    prev = pl.BlockSpec((HALO, width), lambda i: (jnp.maximum(i * per - 1, 0), 0))
    cur = pl.BlockSpec((tm, width), lambda i: (i, 0))
    nxt = pl.BlockSpec((HALO, width), lambda i: (jnp.minimum((i + 1) * per, last), 0))
    return [prev, cur, nxt]


def _pool_mixer(a, w_bd, pscale_all, layer, seq):
    rows = a.shape[0]
    tm = TM_POOL
    assert seq % tm == 0
    const = lambda i: (0, 0)
    return pl.pallas_call(
        functools.partial(_pool_kernel, seq=seq),
        out_shape=jax.ShapeDtypeStruct((rows, POOL_WIDTH), BF16),
        grid=(rows // tm,),
        in_specs=_halo_specs(tm, POOL_WIDTH, rows) + [
            pl.BlockSpec((POOL_WIDTH, POOL_WIDTH), const),
            _layer_spec((1, POOL_WIDTH), layer),
        ],
        out_specs=pl.BlockSpec((tm, POOL_WIDTH), lambda i: (i, 0)),
        compiler_params=_cparams(("parallel",)),
        name="pool_mixer",
    )(a, a, a, w_bd, pscale_all)


def _flash_kernel(q_ref, k_ref, v_ref, ck_ref, cv_ref, y_ref, m_ref, acc_ref):
    tq = q_ref.shape[0]
    low = _low_half(tq)
    grp = B_HEADS // B_KV
    n_chunks = k_ref.shape[0] // TK_FLASH
    qs = [_stack_heads(q_ref, list(range(j * grp, (j + 1) * grp))) for j in range(2)]
    m_ref[...] = jnp.full(m_ref.shape, NEG, F32)
    acc_ref[...] = jnp.zeros(acc_ref.shape, F32)

    def step(kc, vc):
        lowk = _low_half(vc.shape[0])
        for j in range(2):
            own = lowk if j == 0 else jnp.logical_not(lowk)
            vj = jnp.where(own, vc, jnp.ones_like(vc))
            s = _dot_nt(qs[j], kc)
            m_prev = m_ref[j]
            m_new = jnp.maximum(m_prev, s.max(axis=-1, keepdims=True))
            p = jnp.exp2(s - jnp.concatenate([m_new] * (s.shape[1] // LANES), axis=1))
            acc_ref[j] = jnp.exp2(m_prev - m_new) * acc_ref[j] + _dot(p.astype(BF16), vj)
            m_ref[j] = m_new

    def body(c, carry):
        off = pl.multiple_of(c * TK_FLASH, TK_FLASH)
        step(k_ref[pl.ds(off, TK_FLASH), :], v_ref[pl.ds(off, TK_FLASH), :])
        return carry

    lax.fori_loop(0, n_chunks, body, 0, unroll=4)
    step(ck_ref[...], cv_ref[...])
    outs = []
    for j in range(2):
        acc = acc_ref[j]
        o = acc / pltpu.roll(acc, HEAD_DIM, 1)
        outs += [o[g * tq:(g + 1) * tq] for g in range(grp)]
    _place_heads(y_ref, 0, outs, lambda hd: hd // grp, low)


def _flash_attn(q, k, v, ck_all, cv_all, layer, n_batch):
    rows = q.shape[0]
    seq = rows // n_batch
    past = ck_all.shape[2]
    tq = TQ_FLASH
    nq = seq // tq
    grp = B_HEADS // B_KV
    qw = 2 * grp * LANES
    yw = 2 * grp * HEAD_DIM
    ctx = pl.BlockSpec((None, None, past, LANES), lambda b, p, i: (b, layer, 0, p))
    return pl.pallas_call(
        _flash_kernel,
        out_shape=jax.ShapeDtypeStruct((rows, B_HEADS * HEAD_DIM), BF16),
        grid=(n_batch, B_KV // 2, nq),
        in_specs=[
            pl.BlockSpec((tq, qw), lambda b, p, i: (b * nq + i, p)),
            pl.BlockSpec((seq, LANES), lambda b, p, i: (b, p)),
            pl.BlockSpec((seq, LANES), lambda b, p, i: (b, p)),
            ctx, ctx,
        ],
        out_specs=pl.BlockSpec((tq, yw), lambda b, p, i: (b * nq + i, p)),
        scratch_shapes=[
            pltpu.VMEM((2, grp * tq, LANES), F32),
            pltpu.VMEM((2, grp * tq, LANES), F32),
        ],
        compiler_params=_cparams(("parallel", "parallel", "arbitrary")),
        name="latent_flash_attn",
    )(q, k, v, ck_all, cv_all)


def _window_kernel(sink_ref, q_ref, kp_ref, kc_ref, kn_ref, vp_ref, vc_ref, vn_ref, ck_ref, cv_ref, y_ref,
                   *, n_blocks):
    tq = q_ref.shape[0]
    n = pl.program_id(1)
    low = _low_half(tq)
    grp = C_HEADS // C_KV
    rows = grp * tq
    r = lax.broadcasted_iota(jnp.int32, (rows, C_WINDOW), 0) % tq
    c = lax.broadcasted_iota(jnp.int32, (rows, C_WINDOW), 1)
    prev_ok = (r - (c - C_WINDOW) <= C_WINDOW) & (n > 0)
    next_ok = ((c + tq) - r <= C_WINDOW) & (n < n_blocks - 1)
    r2 = lax.broadcasted_iota(jnp.int32, (rows, tq), 0) % tq
    c2 = lax.broadcasted_iota(jnp.int32, (rows, tq), 1)
    cur_ok = jnp.abs(r2 - c2) <= C_WINDOW
    outs = []
    for kv in range(C_KV):
        heads = list(range(kv * grp, (kv + 1) * grp))
        q = _stack_heads(q_ref, heads)
        sink = _sink_column(sink_ref, heads, tq)
        scores = [
            jnp.where(prev_ok, _dot_nt(q, kp_ref[...]), NEG),
            jnp.where(cur_ok, _dot_nt(q, kc_ref[...]), NEG),
            jnp.where(next_ok, _dot_nt(q, kn_ref[...]), NEG),
            _dot_nt(q, ck_ref[...]),
        ]
        o = _softmax_pv(scores, [vp_ref[...], vc_ref[...], vn_ref[...], cv_ref[...]], sink)
        outs += [o[g * tq:(g + 1) * tq] for g in range(grp)]
    _place_heads(y_ref, 0, outs, lambda hd: hd // grp, low)


def _window_attn(sink_all, q, k, v, ck_all, cv_all, layer, n_batch):
    rows = q.shape[0]
    seq = rows // n_batch
    past = ck_all.shape[2]
    tq = TQ_WIN
    assert tq % C_WINDOW == 0 and seq % tq == 0
    nb = seq // tq
    per = tq // C_WINDOW
    nw = seq // C_WINDOW
    prev = pl.BlockSpec((C_WINDOW, LANES), lambda b, n: (b * nw + jnp.maximum(n * per - 1, 0), 0))
    cur = pl.BlockSpec((tq, LANES), lambda b, n: (b * nb + n, 0))
    nxt = pl.BlockSpec((C_WINDOW, LANES), lambda b, n: (b * nw + jnp.minimum((n + 1) * per, nw - 1), 0))
    ctx = pl.BlockSpec((None, None, past, LANES), lambda b, n: (b, layer, 0, 0))
    return pl.pallas_call(
        functools.partial(_window_kernel, n_blocks=nb),
        out_shape=jax.ShapeDtypeStruct((rows, C_HEADS * HEAD_DIM), BF16),
        grid=(n_batch, nb),
        in_specs=[
            pl.BlockSpec(memory_space=pltpu.SMEM),
            pl.BlockSpec((tq, C_HEADS * LANES), lambda b, n: (b * nb + n, 0)),
            prev, cur, nxt, prev, cur, nxt, ctx, ctx,
        ],
        out_specs=pl.BlockSpec((tq, C_HEADS * HEAD_DIM), lambda b, n: (b * nb + n, 0)),
        compiler_params=_cparams(("parallel", "parallel")),
        name="latent_window_attn",
    )(sink_all[layer], q, k, k, k, v, v, v, ck_all, cv_all)


def _na_window_start(blk, n_rows):
    return jnp.clip(blk * NA_QROWS - NA_ROWS // 2, 0, n_rows - NA_WIN)


def _na_kernel(q_ref, k_ref, v_ref, ck_ref, cv_ref, pair_ref, y_ref, *, n_rows):
    blk = pl.program_id(1)
    tq = NA_QROWS * GRID_W
    win = NA_WIN * GRID_W
    low = _low_half(tq)
    ws = _na_window_start(blk, n_rows)
    start = pl.multiple_of(ws * GRID_W, GRID_W)
    q_row = blk * NA_QROWS + lax.broadcasted_iota(jnp.int32, (tq, win), 0) // GRID_W
    k_row = ws + lax.broadcasted_iota(jnp.int32, (tq, win), 1) // GRID_W
    rs = jnp.clip(q_row - NA_ROWS // 2, 0, n_rows - NA_ROWS)
    row_mask = jnp.where((k_row >= rs) & (k_row < rs + NA_ROWS), 0.0, NEG)
    outs = []
    for hd in range(D_HEADS):
        sl = slice((hd // 2) * LANES, (hd // 2 + 1) * LANES)
        q = q_ref[:, hd * LANES:(hd + 1) * LANES]
        kw = k_ref[pl.ds(start, win), sl]
        vw = v_ref[pl.ds(start, win), sl]
        bias = jnp.concatenate([
            jnp.concatenate([
                pair_ref[hd, jnp.clip(ws + 2 * jp - (blk * NA_QROWS + a) + NA_ROWS, 0, 2 * NA_ROWS)]
                for jp in range(NA_WIN // 2)], axis=1)
            for a in range(NA_QROWS)], axis=0)
        s_loc = _dot_nt(q, kw) + bias + row_mask
        s_ctx = _dot_nt(q, ck_ref[:, sl])
        outs.append(_softmax_pv([s_loc, s_ctx], [vw, cv_ref[:, sl]]))
    _place_heads(y_ref, 0, outs, lambda hd: hd % 2, low)


def _na_attn(q, k, v, ck_all, cv_all, pair_all, layer, n_batch):
    rows = q.shape[0]
    seq = rows // n_batch
    past = ck_all.shape[2]
    n_rows = seq // GRID_W
    n_blk = n_rows // NA_QROWS
    assert n_rows >= NA_WIN and NA_WIN % 2 == 0
    width = D_HEADS * HEAD_DIM
    tq = NA_QROWS * GRID_W
    full = pl.BlockSpec((seq, width), lambda b, r: (b, 0))
    ctx = pl.BlockSpec((None, None, past, width), lambda b, r: (b, layer, 0, 0))
    return pl.pallas_call(
        functools.partial(_na_kernel, n_rows=n_rows),
        out_shape=jax.ShapeDtypeStruct((rows, width), BF16),
        grid=(n_batch, n_blk),
        in_specs=[
            pl.BlockSpec((tq, D_HEADS * LANES), lambda b, r: (b * n_blk + r, 0)),
            full, full, ctx, ctx,
            _layer_spec(pair_all.shape[1:], layer),
        ],
        out_specs=pl.BlockSpec((tq, width), lambda b, r: (b * n_blk + r, 0)),
        compiler_params=_cparams(("parallel", "arbitrary")),
        name="latent_neighbourhood_attn",
    )(q, k, v, ck_all, cv_all, pair_all)


def _na_pair_tiles(rpb_all):
    n_l, n_h, n_y, n_x = rpb_all.shape
    n_tiles = 2 * NA_ROWS + 1
    zero = jnp.zeros((n_l, n_h, 1, n_x), F32)
    rows = jnp.concatenate([zero, rpb_all, zero, zero], axis=2)
    feats = jnp.concatenate([rows[:, :, :-1], rows[:, :, 1:]], axis=-1)
    feat = np.arange(2 * n_x)[:, None, None]
    qc = np.arange(GRID_W)[None, :, None]
    lane = np.arange(2 * GRID_W)[None, None, :]
    kc = lane % GRID_W
    cs = np.clip(qc - NA_COLS // 2, 0, GRID_W - NA_COLS)
    inside = (kc >= cs) & (kc < cs + NA_COLS)
    select = (lane // GRID_W == feat // n_x) & (feat % n_x == kc - qc + NA_COLS - 1) & inside
    iy = np.arange(n_tiles)[:, None, None] - 1 + lane // GRID_W
    valid = inside & (iy >= 0) & (iy < n_y)
    tiles = jnp.einsum('lhef,fqx->lheqx', feats, jnp.asarray(select, F32), precision=lax.Precision.HIGHEST)
    return jnp.where(valid, tiles, NEG)


def _mixer_ffn_kernel(*refs, seq, halo, widths):
    per = 3 if halo else 1
    n_in = per * (1 + len(widths))
    x_refs = refs[:per]
    part_refs = [refs[per * (1 + k):per * (2 + k)] for k in range(len(widths))]
    (mod_ref, gmix_ref, wmix_ref, gpre_ref, gpost_ref, wi_ref, wc_ref, wo_ref, o_ref, y_ref, h_ref) = refs[n_in:]
    tm = x_refs[per // 2].shape[0]
    pad = HALO if halo else 0
    ext = tm + 2 * pad
    pieces = [(pad, tm)] if not halo else [(0, pad), (pad, tm), (pad + tm, pad)]
    keep = [None]
    if halo:
        pos0 = (pl.program_id(0) * tm) % seq
        keep = [jnp.where(pos0 > 0, 1.0, 0.0), None, jnp.where(pos0 + tm < seq, 1.0, 0.0)]

    off = 0
    for p_refs, wd in zip(part_refs, widths):
        for (r0, n), p_ref in zip(pieces, p_refs):
            y_ref[r0:r0 + n, off:off + wd] = p_ref[...]
        off += wd
    t = _dot(y_ref[...], wmix_ref[...])
    scale, shift = mod_ref[4:5, :], mod_ref[3:4, :]
    for (r0, n), x_ref, kp in zip(pieces, x_refs, keep):
        x1 = x_ref[...] + mod_ref[2:3, :] * _rmsnorm(t[r0:r0 + n], gmix_ref[...])
        h = _norm_mod(x1, gpre_ref[...], scale, shift)
        if kp is None:
            o_ref[...] = x1
        else:
            h = h * kp
        h_ref[r0:r0 + n, :] = h.astype(BF16)

    def conv_up(c0, width):
        u = _dot(h_ref[...], wi_ref[:, c0:c0 + width])
        up = pltpu.roll(u, 1, 0)[pad:pad + tm]
        un = pltpu.roll(u, ext - 1, 0)[pad:pad + tm]
        if not halo:
            sub = lax.broadcasted_iota(jnp.int32, (8, width), 0)
            up = jnp.concatenate([jnp.where(sub == 0, 0.0, up[:8]), up[8:]], axis=0)
            un = jnp.concatenate([un[:-8], jnp.where(sub == 7, 0.0, un[-8:])], axis=0)
        return (u[pad:pad + tm] * wc_ref[1:2, c0:c0 + width] + up * wc_ref[0:1, c0:c0 + width]
                + un * wc_ref[2:3, c0:c0 + width])

    y = None
    for c0, width in FF_CHUNKS:
        gate = conv_up(c0, width)
        val = conv_up(D_FF + c0, width)
        act = gate * (1.0 / (1.0 + jnp.exp(-gate))) * val
        contrib = _dot(act.astype(BF16), wo_ref[c0:c0 + width, :])
        y = contrib if y is None else y + contrib
    o_ref[...] += mod_ref[5:6, :] * _rmsnorm(y, gpost_ref[...])


def _mixer_ffn(x, parts, mods, layer, g_mix_all, w_mix_all, g_pre_all, g_post_all, wi_all, wc_all, wo_all,
               seq, seq_blocks):
    rows = x.shape[0]
    tm = min(TM_FFN, seq)
    assert seq % tm == 0
    halo = tm < seq
    widths = tuple(p.shape[1] for p in parts)
    assert sum(widths) == D_MODEL
    resident = dict(pipeline_mode=pl.Buffered(1))
    zeros3 = lambda i: (layer, 0, 0)
    row_specs = lambda wd: (_halo_specs(tm, wd, rows) if halo else [pl.BlockSpec((tm, wd), lambda i: (i, 0))])
    in_specs, args = [], []
    for arr in (x,) + tuple(parts):
        specs = row_specs(arr.shape[1])
        in_specs += specs
        args += [arr] * len(specs)
    in_specs += [
        _mod_spec(layer, seq_blocks),
        _layer_spec((1, D_MODEL), layer),
        _layer_spec((D_MODEL, D_MODEL), layer),
        _layer_spec((1, D_MODEL), layer),
        _layer_spec((1, D_MODEL), layer),
        pl.BlockSpec((None, D_MODEL, 2 * D_FF), zeros3, **resident),
        _layer_spec((3, 2 * D_FF), layer),
        pl.BlockSpec((None, D_FF, D_MODEL), zeros3, **resident),
    ]
    args += [mods, g_mix_all, w_mix_all, g_pre_all, g_post_all, wi_all, wc_all, wo_all]
    ext = tm + (2 * HALO if halo else 0)
    return pl.pallas_call(
        functools.partial(_mixer_ffn_kernel, seq=seq, halo=halo, widths=widths),
        out_shape=jax.ShapeDtypeStruct((rows, D_MODEL), F32),
        grid=(rows // tm,),
        in_specs=in_specs,
        out_specs=pl.BlockSpec((tm, D_MODEL), lambda i: (i, 0)),
        scratch_shapes=[pltpu.VMEM((ext, D_MODEL), BF16), pltpu.VMEM((ext, D_MODEL), BF16)],
        compiler_params=_cparams(("parallel",)),
        name="mixer_ffn",
    )(*args)


def _rope_tables(n_tokens):
    t = jnp.arange(n_tokens)
    row = (t // GRID_W).astype(F32)
    col = (t % GRID_W).astype(F32)
    axis_dim = HEAD_DIM // 2
    inv_freq = ROPE_THETA ** (-jnp.arange(0, axis_dim, 2, dtype=F32) / axis_dim)
    ang = jnp.concatenate([row[:, None] * inv_freq, col[:, None] * inv_freq], axis=-1)
    cos = jnp.repeat(jnp.cos(ang), 2, axis=-1)
    sin = jnp.sin(ang)
    sin = jnp.stack([-sin, sin], axis=-1).reshape(n_tokens, HEAD_DIM)
    return jnp.tile(cos, (1, 2)), jnp.tile(sin, (1, 2))


def _rows3(p):
    return p.reshape(p.shape[0], 1, p.shape[1])


def kernel(x_prompt, x_sample, cache_b_k, cache_b_v, cache_c_k, cache_c_v, cache_d_k, cache_d_v, c, c_ctx,
           w_mod, b_mod, g_mix_pre, g_mix_post, g_ffn_pre, g_ffn_post, w_in_even, w_pool, pool_scale,
           g_q_b, g_k_b, w_in_odd, sink_c, rpb_d, w_mix_out, w_ffn_in, w_ffn_conv, w_ffn_out):
    n_ctx, ctx_seq, _ = x_prompt.shape
    n_lat, lat_seq, _ = x_sample.shape
    past = cache_b_k.shape[2]
    xp = x_prompt.reshape(n_ctx * ctx_seq, D_MODEL)
    xs = x_sample.reshape(n_lat * lat_seq, D_MODEL)
    lat_blocks = lat_seq // TM_PROJ

    cvecs = jnp.concatenate([c_ctx[None, :], c, jnp.zeros((8 - 1 - n_lat, D_MODEL), F32)], axis=0)
    mods = _modulation(cvecs, w_mod, b_mod).reshape(DEPTH, 8, 6, D_MODEL)

    rope_tabs = _rope_tables(lat_seq)
    blk = jnp.arange(LANES) // HEAD_DIM
    e_mat = jnp.where(blk[:, None] == blk[None, :], 1.0 / HEAD_DIM, 0.0).astype(BF16)

    w_even = w_in_even.astype(BF16)
    w_odd = w_in_odd.astype(BF16)
    w_out = w_mix_out.astype(BF16)
    w_fi = w_ffn_in.astype(BF16)
    w_fo = w_ffn_out.astype(BF16)
    slabs = lambda t: t.astype(BF16).reshape(t.shape[0], t.shape[1], past, t.shape[3] * HEAD_DIM)
    cbk, cbv, cck, ccv, cdk, cdv = map(slabs, (cache_b_k, cache_b_v, cache_c_k, cache_c_v, cache_d_k, cache_d_v))
    g_pre, g_post = _rows3(g_mix_pre), _rows3(g_mix_post)
    gf_pre, gf_post = _rows3(g_ffn_pre), _rows3(g_ffn_post)
    pscale = _rows3(pool_scale)
    na_pairs = _na_pair_tiles(rpb_d)

    b_k, b_v, c_k, c_v, d_k, d_v = [], [], [], [], [], []
    for l in range(DEPTH):
        i = l // 2
        if l % 2 == 0:
            gq = jnp.tile(g_q_b[i], 2)[None, :]
            gk = jnp.tile(g_k_b[i], 2)[None, :]
            kvw = B_KV * HEAD_DIM
            widths = (POOL_WIDTH, B_HEADS * LANES, kvw, kvw)
            a_p, q_p, k_p, v_p = _proj_in(xp, mods, l, g_pre, w_even, i, (gq, gk, e_mat), widths,
                                          (F32, BF16, F32, F32),
                                          functools.partial(_proj_even_kernel, q_scale=QK_SCALE), None, None,
                                          "proj_even_ctx")
            a_s, q_s, k_s, v_s = _proj_in(xs, mods, l, g_pre, w_even, i, (gq, gk, e_mat), widths,
                                          (F32, BF16, BF16, BF16),
                                          functools.partial(_proj_even_kernel, q_scale=QK_SCALE * LOG2E),
                                          lat_blocks, rope_tabs,
                                          "proj_even_lat")
            b_k.append(k_p.reshape(n_ctx, ctx_seq, B_KV, HEAD_DIM))
            b_v.append(v_p.reshape(n_ctx, ctx_seq, B_KV, HEAD_DIM))
            w_bd = jax.scipy.linalg.block_diag(*[w_pool[i, g] for g in range(len(POOL_WINDOWS))]).astype(BF16)
            ya_p = _pool_mixer(a_p, w_bd, pscale, i, ctx_seq)
            ya_s = _pool_mixer(a_s, w_bd, pscale, i, lat_seq)
            yb_p = _ctx_even_attn(q_p, k_p, v_p, ctx_seq)
            yb_s = _flash_attn(q_s, k_s, v_s, cbk, cbv, i, n_lat)
            parts_p, parts_s = (ya_p, yb_p), (ya_s, yb_s)
        else:
            ckw, dw = C_KV * HEAD_DIM, D_HEADS * HEAD_DIM
            widths = (C_HEADS * LANES, ckw, ckw, D_HEADS * LANES, dw, dw)
            qc_p, kc_p, vc_p, qd_p, kd_p, vd_p = _proj_in(
                xp, mods, l, g_pre, w_odd, i, (), widths, (BF16, F32, F32, BF16, F32, F32),
                _proj_odd_kernel, None, None, "proj_odd_ctx")
            qc_s, kc_s, vc_s, qd_s, kd_s, vd_s = _proj_in(
                xs, mods, l, g_pre, w_odd, i, (), widths, (BF16,) * 6,
                _proj_odd_kernel, lat_blocks, rope_tabs, "proj_odd_lat")
            c_k.append(kc_p.reshape(n_ctx, ctx_seq, C_KV, HEAD_DIM))
            c_v.append(vc_p.reshape(n_ctx, ctx_seq, C_KV, HEAD_DIM))
            d_k.append(kd_p.reshape(n_ctx, ctx_seq, D_HEADS, HEAD_DIM))
            d_v.append(vd_p.reshape(n_ctx, ctx_seq, D_HEADS, HEAD_DIM))
            y_p = _ctx_odd_attn(sink_c, i, qc_p, kc_p, vc_p, qd_p, kd_p, vd_p, ctx_seq)
            yc_s = _window_attn(sink_c, qc_s, kc_s, vc_s, cck, ccv, i, n_lat)
            yd_s = _na_attn(qd_s, kd_s, vd_s, cdk, cdv, na_pairs, i, n_lat)
            parts_p, parts_s = (y_p,), (yc_s, yd_s)

        xp = _mixer_ffn(xp, parts_p, mods, l, g_post, w_out, gf_pre, gf_post, w_fi, w_ffn_conv, w_fo,
                        ctx_seq, None)
        xs = _mixer_ffn(xs, parts_s, mods, l, g_post, w_out, gf_pre, gf_post, w_fi, w_ffn_conv, w_fo,
                        lat_seq, lat_seq // TM_FFN)

    return (xp.reshape(n_ctx, ctx_seq, D_MODEL), xs.reshape(n_lat, lat_seq, D_MODEL),
            jnp.stack(b_k, axis=1), jnp.stack(b_v, axis=1),
            jnp.stack(c_k, axis=1), jnp.stack(c_v, axis=1),
            jnp.stack(d_k, axis=1), jnp.stack(d_v, axis=1))
```

```python
import functools

import jax
import jax.numpy as jnp
import numpy as np
from jax import lax
from jax.experimental import pallas as pl
from jax.experimental.pallas import tpu as pltpu

F32 = jnp.float32
BF16 = jnp.bfloat16

D_MODEL = 1024
DEPTH = 4
GRID_W = 64
HEAD_DIM = 64
EPS = 1e-6
ROPE_THETA = 10000.0
POOL_WINDOWS = (2, 4, 8, 16)
POOL_WIDTH = 256
B_HEADS, B_KV = 12, 4
C_HEADS, C_KV = 8, 2
D_HEADS = 8
C_WINDOW = 128
NA_ROWS, NA_COLS = 8, 16
D_FF = 2816
QK_SCALE = HEAD_DIM ** -0.5
LOG2E = 1.4426950408889634
NEG = -1e30

LANES = 128
HALO = 16
VMEM_LIMIT = 56 * 1024 * 1024

TM_PROJ = 1024
TM_FFN = 512
TM_POOL = 256
MXU_TILE = 256
_FF_TILES = D_FF // MXU_TILE
FF_CHUNKS = ((0, (_FF_TILES + 1) // 2 * MXU_TILE),
             ((_FF_TILES + 1) // 2 * MXU_TILE, _FF_TILES // 2 * MXU_TILE))
assert D_FF % MXU_TILE == 0 and sum(w for _, w in FF_CHUNKS) == D_FF
TQ_FLASH = 256
TK_FLASH = 512
TQ_WIN = 256
CTX_SEQS = 2
NA_QROWS = 4
NA_WIN = NA_ROWS + NA_QROWS


def _cparams(sem):
    return pltpu.CompilerParams(dimension_semantics=sem, vmem_limit_bytes=VMEM_LIMIT)


def _norm_mod(x, g, scale, shift):
    ms = jnp.mean(x * x, axis=-1, keepdims=True)
    return (x * lax.rsqrt(ms + EPS) * g) * (1.0 + scale) + shift


def _rmsnorm(x, g):
    ms = jnp.mean(x * x, axis=-1, keepdims=True)
    return x * lax.rsqrt(ms + EPS) * g


def _dot(a, b):
    return jnp.dot(a, b, preferred_element_type=F32)


def _dot_nt(a, b):
    return lax.dot_general(a, b, (((1,), (1,)), ((), ())), preferred_element_type=F32)


def _low_half(rows):
    return lax.broadcasted_iota(jnp.int32, (rows, LANES), 1) < HEAD_DIM


def _layer_spec(shape, layer):
    zeros = (0,) * len(shape)
    return pl.BlockSpec((None,) + tuple(shape), lambda *_: (layer,) + zeros)


def _mod_spec(layer, seq_blocks):
    if seq_blocks is None:
        return pl.BlockSpec((None, None, 6, D_MODEL), lambda i: (layer, 0, 0, 0))
    return pl.BlockSpec((None, None, 6, D_MODEL), lambda i: (layer, 1 + i // seq_blocks, 0, 0))


def _mod_kernel(c_ref, w_ref, b_ref, o_ref):
    cv = c_ref[...]
    s = cv * (1.0 / (1.0 + jnp.exp(-cv)))
    o_ref[...] = _dot(s.astype(BF16), w_ref[...].astype(BF16)) + b_ref[...]


def _modulation(cvecs, w_mod, b_mod):
    tn = 1536
    return pl.pallas_call(
        _mod_kernel,
        out_shape=jax.ShapeDtypeStruct((DEPTH, 8, 6 * D_MODEL), F32),
        grid=(DEPTH, 6 * D_MODEL // tn),
        in_specs=[
            pl.BlockSpec((8, D_MODEL), lambda l, j: (0, 0)),
            pl.BlockSpec((None, D_MODEL, tn), lambda l, j: (l, 0, j)),
            pl.BlockSpec((None, 1, tn), lambda l, j: (l, 0, j)),
        ],
        out_specs=pl.BlockSpec((None, 8, tn), lambda l, j: (l, 0, j)),
        compiler_params=_cparams(("parallel", "parallel")),
        name="modulation",
    )(cvecs, w_mod, b_mod.reshape(DEPTH, 1, 6 * D_MODEL))


def _pair_swap(s):
    lane = lax.broadcasted_iota(jnp.int32, s.shape, 1)
    n = s.shape[1]
    return jnp.where(lane % 2 == 0, pltpu.roll(s, n - 1, 1), pltpu.roll(s, 1, 1))


def _store_expanded_q(q_ref, s, slab, kv_half_of, low):
    sr = pltpu.roll(s, HEAD_DIM, 1)
    for half in (0, 1):
        hd = 2 * slab + half
        dst = kv_half_of(hd)
        val = s if dst == half else sr
        keep = low if dst == 0 else jnp.logical_not(low)
        q_ref[:, hd * LANES:(hd + 1) * LANES] = jnp.where(keep, val, 0.0).astype(BF16)


def _proj_even_kernel(*refs, rope, q_scale):
    if rope:
        (x_ref, mod_ref, g_ref, w_ref, gq_ref, gk_ref, e_ref, cos_ref, sin_ref,
         a_ref, q_ref, k_ref, v_ref) = refs
    else:
        (x_ref, mod_ref, g_ref, w_ref, gq_ref, gk_ref, e_ref,
         a_ref, q_ref, k_ref, v_ref) = refs
    rows = x_ref.shape[0]
    h = _norm_mod(x_ref[...], g_ref[...], mod_ref[1:2, :], mod_ref[0:1, :])
    u = _dot(h.astype(BF16), w_ref[...])
    low = _low_half(rows)
    a_ref[...] = u[:, :POOL_WIDTH]

    def headnorm(s, g):
        ms = _dot((s * s).astype(BF16), e_ref[...])
        return s * lax.rsqrt(ms + EPS) * g

    def rot(s):
        if not rope:
            return s
        return s * cos_ref[...] + _pair_swap(s) * sin_ref[...]

    q0 = POOL_WIDTH
    for j in range(B_HEADS // 2):
        s = u[:, q0 + LANES * j:q0 + LANES * (j + 1)]
        s = rot(headnorm(s, gq_ref[...])) * q_scale
        _store_expanded_q(q_ref, s, j, lambda hd: (hd // (B_HEADS // B_KV)) % 2, low)
    k0 = q0 + B_HEADS * HEAD_DIM
    for j in range(B_KV // 2):
        s = headnorm(u[:, k0 + LANES * j:k0 + LANES * (j + 1)], gk_ref[...])
        k_ref[:, LANES * j:LANES * (j + 1)] = rot(s).astype(k_ref.dtype)
    v0 = k0 + B_KV * HEAD_DIM
    v_ref[...] = u[:, v0:v0 + B_KV * HEAD_DIM].astype(v_ref.dtype)


def _proj_odd_kernel(*refs, rope):
    if rope:
        (x_ref, mod_ref, g_ref, w_ref, cos_ref, sin_ref,
         qc_ref, kc_ref, vc_ref, qd_ref, kd_ref, vd_ref) = refs
    else:
        (x_ref, mod_ref, g_ref, w_ref,
         qc_ref, kc_ref, vc_ref, qd_ref, kd_ref, vd_ref) = refs
    rows = x_ref.shape[0]
    h = _norm_mod(x_ref[...], g_ref[...], mod_ref[1:2, :], mod_ref[0:1, :])
    u = _dot(h.astype(BF16), w_ref[...])
    low = _low_half(rows)

    def rot(s):
        if not rope:
            return s
        return s * cos_ref[...] + _pair_swap(s) * sin_ref[...]

    for j in range(C_HEADS // 2):
        s = rot(u[:, LANES * j:LANES * (j + 1)]) * QK_SCALE
        _store_expanded_q(qc_ref, s, j, lambda hd: hd // (C_HEADS // C_KV), low)
    o = C_HEADS * HEAD_DIM
    kc_ref[...] = rot(u[:, o:o + LANES]).astype(kc_ref.dtype)
    o += C_KV * HEAD_DIM
    vc_ref[...] = u[:, o:o + LANES].astype(vc_ref.dtype)
    o += C_KV * HEAD_DIM
    for j in range(D_HEADS // 2):
        s = u[:, o + LANES * j:o + LANES * (j + 1)] * QK_SCALE
        _store_expanded_q(qd_ref, s, j, lambda hd: hd % 2, low)
    o += D_HEADS * HEAD_DIM
    kd_ref[...] = u[:, o:o + D_HEADS * HEAD_DIM].astype(kd_ref.dtype)
    o += D_HEADS * HEAD_DIM
    vd_ref[...] = u[:, o:o + D_HEADS * HEAD_DIM].astype(vd_ref.dtype)


def _proj_in(x, mods, layer, g_all, w_all, w_layer, extras, out_widths, out_dtypes, kernel, seq_blocks,
             rope_tabs, name):
    rows = x.shape[0]
    tm = TM_PROJ
    const = lambda i: (0, 0)
    in_specs = [
        pl.BlockSpec((tm, D_MODEL), lambda i: (i, 0)),
        _mod_spec(layer, seq_blocks),
        _layer_spec((1, D_MODEL), layer),
        _layer_spec(w_all.shape[1:], w_layer),
    ]
    args = [x, mods, g_all, w_all]
    for e in extras:
        in_specs.append(pl.BlockSpec(e.shape, const))
        args.append(e)
    if rope_tabs is not None:
        for t in rope_tabs:
            in_specs.append(pl.BlockSpec((tm, LANES), lambda i: (i % seq_blocks, 0)))
            args.append(t)
    return pl.pallas_call(
        functools.partial(kernel, rope=rope_tabs is not None),
        out_shape=[jax.ShapeDtypeStruct((rows, wd), dt) for wd, dt in zip(out_widths, out_dtypes)],
        grid=(rows // tm,),
        in_specs=in_specs,
        out_specs=[pl.BlockSpec((tm, wd), lambda i: (i, 0)) for wd in out_widths],
        compiler_params=_cparams(("parallel",)),
        name=name,
    )(*args)


def _place_heads(out_ref, col0, heads, kv_half_of, low):
    for m in range(len(heads) // 2):
        a = heads[2 * m]
        if kv_half_of(2 * m) == 1:
            a = pltpu.roll(a, HEAD_DIM, 1)
        b = heads[2 * m + 1]
        if kv_half_of(2 * m + 1) == 0:
            b = pltpu.roll(b, HEAD_DIM, 1)
        out_ref[:, col0 + m * LANES:col0 + (m + 1) * LANES] = jnp.where(low, a, b).astype(out_ref.dtype)


def _stack_heads(q_ref, heads):
    return jnp.concatenate([q_ref[:, h * LANES:(h + 1) * LANES] for h in heads], axis=0)


def _softmax_pv(scores, values, sink=None):
    s = scores[0] if len(scores) == 1 else jnp.concatenate(scores, axis=1)
    v = values[0] if len(values) == 1 else jnp.concatenate(values, axis=0)
    m = jnp.broadcast_to(s.max(axis=-1, keepdims=True), (s.shape[0], LANES))
    if sink is not None:
        m = jnp.maximum(m, sink)
    e = jnp.exp(s - jnp.concatenate([m] * (s.shape[1] // LANES), axis=1))
    acc = _dot(e.astype(BF16), jnp.concatenate([v, jnp.ones_like(v)], axis=1))
    den = acc[:, LANES:]
    if sink is not None:
        den = den + jnp.exp(sink - m)
    return acc[:, :LANES] / den


def _sink_column(sink_ref, heads, rows_per_head):
    row = lax.broadcasted_iota(jnp.int32, (len(heads) * rows_per_head, LANES), 0)
    col = jnp.full(row.shape, sink_ref[heads[-1]], F32)
    for g in range(len(heads) - 2, -1, -1):
        col = jnp.where(row < (g + 1) * rows_per_head, sink_ref[heads[g]], col)
    return col


def _ctx_even_attn_seq(q_ref, k_ref, v_ref, y_ref):
    rows = q_ref.shape[0]
    low = _low_half(rows)
    grp = B_HEADS // B_KV
    outs = []
    for kv in range(B_KV):
        sl = slice((kv // 2) * LANES, (kv // 2 + 1) * LANES)
        ks = k_ref[:, sl].astype(BF16)
        vs = v_ref[:, sl].astype(BF16)
        heads = list(range(kv * grp, (kv + 1) * grp))
        o = _softmax_pv([_dot_nt(_stack_heads(q_ref, heads), ks)], [vs])
        outs += [o[g * rows:(g + 1) * rows] for g in range(grp)]
    _place_heads(y_ref, 0, outs, lambda hd: (hd // grp) % 2, low)


def _ctx_odd_attn_seq(sink_ref, qc_ref, kc_ref, vc_ref, qd_ref, kd_ref, vd_ref, y_ref):
    rows = qc_ref.shape[0]
    low = _low_half(rows)
    grp = C_HEADS // C_KV
    ks = kc_ref[...].astype(BF16)
    vs = vc_ref[...].astype(BF16)
    outs = []
    for kv in range(C_KV):
        heads = list(range(kv * grp, (kv + 1) * grp))
        sink = _sink_column(sink_ref, heads, rows)
        o = _softmax_pv([_dot_nt(_stack_heads(qc_ref, heads), ks)], [vs], sink)
        outs += [o[g * rows:(g + 1) * rows] for g in range(grp)]
    _place_heads(y_ref, 0, outs, lambda hd: hd // grp, low)
    outs = []
    for hd in range(D_HEADS):
        sl = slice((hd // 2) * LANES, (hd // 2 + 1) * LANES)
        s = _dot_nt(qd_ref[:, hd * LANES:(hd + 1) * LANES], kd_ref[:, sl].astype(BF16))
        outs.append(_softmax_pv([s], [vd_ref[:, sl].astype(BF16)]))
    _place_heads(y_ref, C_HEADS * HEAD_DIM, outs, lambda hd: hd % 2, low)


def _per_sequence(*refs, fn, seq, n_scalar=0):
    for i in range(refs[n_scalar].shape[0] // seq):
        rows = slice(i * seq, (i + 1) * seq)
        fn(*refs[:n_scalar], *[r.at[rows] for r in refs[n_scalar:]])


def _ctx_even_attn(q, k, v, seq):
    rows = q.shape[0]
    blk = lambda wd: pl.BlockSpec((CTX_SEQS * seq, wd), lambda b: (b, 0))
    return pl.pallas_call(
        functools.partial(_per_sequence, fn=_ctx_even_attn_seq, seq=seq),
        out_shape=jax.ShapeDtypeStruct((rows, B_HEADS * HEAD_DIM), BF16),
        grid=(rows // (CTX_SEQS * seq),),
        in_specs=[blk(q.shape[1]), blk(k.shape[1]), blk(v.shape[1])],
        out_specs=blk(B_HEADS * HEAD_DIM),
        compiler_params=_cparams(("parallel",)),
        name="ctx_even_attn",
    )(q, k, v)


def _ctx_odd_attn(sink_all, layer, qc, kc, vc, qd, kd, vd, seq):
    rows = qc.shape[0]
    blk = lambda wd: pl.BlockSpec((CTX_SEQS * seq, wd), lambda b: (b, 0))
    return pl.pallas_call(
        functools.partial(_per_sequence, fn=_ctx_odd_attn_seq, seq=seq, n_scalar=1),
        out_shape=jax.ShapeDtypeStruct((rows, D_MODEL), BF16),
        grid=(rows // (CTX_SEQS * seq),),
        in_specs=[pl.BlockSpec(memory_space=pltpu.SMEM)] + [blk(a.shape[1]) for a in (qc, kc, vc, qd, kd, vd)],
        out_specs=blk(D_MODEL),
        compiler_params=_cparams(("parallel",)),
        name="ctx_odd_attn",
    )(sink_all[layer], qc, kc, vc, qd, kd, vd)


def _pool_kernel(ap_ref, a_ref, an_ref, w_ref, ps_ref, y_ref, *, seq):
    tm = a_ref.shape[0]
    i = pl.program_id(0)
    pos0 = (i * tm) % seq
    a = a_ref[...]
    a_ext = jnp.concatenate([ap_ref[...], a, an_ref[...]], axis=0).astype(BF16)
    ext = tm + 2 * HALO
    r = lax.broadcasted_iota(jnp.int32, (tm, ext), 0)
    c = lax.broadcasted_iota(jnp.int32, (tm, ext), 1)
    pos_c = pos0 - HALO + c
    in_seq = (pos_c >= 0) & (pos_c < seq)
    d = c - HALO - r
    pos_r = pos0 + lax.broadcasted_iota(jnp.int32, (tm, POOL_WIDTH), 0)
    grp = lax.broadcasted_iota(jnp.int32, (tm, POOL_WIDTH), 1) // (POOL_WIDTH // len(POOL_WINDOWS))
    mean = jnp.zeros((tm, POOL_WIDTH), F32)
    for gi, wdw in enumerate(POOL_WINDOWS):
        band = jnp.where(in_seq & (d >= -(wdw // 2)) & (d < wdw - wdw // 2), 1.0, 0.0).astype(BF16)
        lo = jnp.maximum(pos_r - wdw // 2, 0)
        hi = jnp.minimum(pos_r - wdw // 2 + wdw, seq)
        win_mean = _dot(band, a_ext) / (hi - lo).astype(F32)
        mean = jnp.where(grp == gi, win_mean, mean)
    pooled = (mean - a).astype(BF16)
    y_ref[...] = (_dot(pooled, w_ref[...]) * ps_ref[...]).astype(y_ref.dtype)


def _halo_specs(tm, width, n_rows):
    per = tm // HALO
    last = n_rows // HALO - 1
    prev = pl.BlockSpec((HALO, width), lambda i: (jnp.maximum(i * per - 1, 0), 0))
    cur = pl.BlockSpec((tm, width), lambda i: (i, 0))
    nxt = pl.BlockSpec((HALO, width), lambda i: (jnp.minimum((i + 1) * per, last), 0))
    return [prev, cur, nxt]


def _pool_mixer(a, w_bd, pscale_all, layer, seq):
    rows = a.shape[0]
    tm = TM_POOL
    assert seq % tm == 0
    const = lambda i: (0, 0)
    return pl.pallas_call(
        functools.partial(_pool_kernel, seq=seq),
        out_shape=jax.ShapeDtypeStruct((rows, POOL_WIDTH), BF16),
        grid=(rows // tm,),
        in_specs=_halo_specs(tm, POOL_WIDTH, rows) + [
            pl.BlockSpec((POOL_WIDTH, POOL_WIDTH), const),
            _layer_spec((1, POOL_WIDTH), layer),
        ],
        out_specs=pl.BlockSpec((tm, POOL_WIDTH), lambda i: (i, 0)),
        compiler_params=_cparams(("parallel",)),
        name="pool_mixer",
    )(a, a, a, w_bd, pscale_all)


def _flash_kernel(q_ref, k_ref, v_ref, ck_ref, cv_ref, y_ref, m_ref, acc_ref):
    tq = q_ref.shape[0]
    low = _low_half(tq)
    grp = B_HEADS // B_KV
    n_chunks = k_ref.shape[0] // TK_FLASH
    qs = [_stack_heads(q_ref, list(range(j * grp, (j + 1) * grp))) for j in range(2)]
    m_ref[...] = jnp.full(m_ref.shape, NEG, F32)
    acc_ref[...] = jnp.zeros(acc_ref.shape, F32)

    def step(kc, vc):
        lowk = _low_half(vc.shape[0])
        for j in range(2):
            own = lowk if j == 0 else jnp.logical_not(lowk)
            vj = jnp.where(own, vc, jnp.ones_like(vc))
            s = _dot_nt(qs[j], kc)
            m_prev = m_ref[j]
            m_new = jnp.maximum(m_prev, s.max(axis=-1, keepdims=True))
            p = jnp.exp2(s - jnp.concatenate([m_new] * (s.shape[1] // LANES), axis=1))
            acc_ref[j] = jnp.exp2(m_prev - m_new) * acc_ref[j] + _dot(p.astype(BF16), vj)
            m_ref[j] = m_new

    def body(c, carry):
        off = pl.multiple_of(c * TK_FLASH, TK_FLASH)
        step(k_ref[pl.ds(off, TK_FLASH), :], v_ref[pl.ds(off, TK_FLASH), :])
        return carry

    lax.fori_loop(0, n_chunks, body, 0, unroll=4)
    step(ck_ref[...], cv_ref[...])
    outs = []
    for j in range(2):
        acc = acc_ref[j]
        o = acc / pltpu.roll(acc, HEAD_DIM, 1)
        outs += [o[g * tq:(g + 1) * tq] for g in range(grp)]
    _place_heads(y_ref, 0, outs, lambda hd: hd // grp, low)


def _flash_attn(q, k, v, ck_all, cv_all, layer, n_batch):
    rows = q.shape[0]
    seq = rows // n_batch
    past = ck_all.shape[2]
    tq = TQ_FLASH
    nq = seq // tq
    grp = B_HEADS // B_KV
    qw = 2 * grp * LANES
    yw = 2 * grp * HEAD_DIM
    ctx = pl.BlockSpec((None, None, past, LANES), lambda b, p, i: (b, layer, 0, p))
    return pl.pallas_call(
        _flash_kernel,
        out_shape=jax.ShapeDtypeStruct((rows, B_HEADS * HEAD_DIM), BF16),
        grid=(n_batch, B_KV // 2, nq),
        in_specs=[
            pl.BlockSpec((tq, qw), lambda b, p, i: (b * nq + i, p)),
            pl.BlockSpec((seq, LANES), lambda b, p, i: (b, p)),
            pl.BlockSpec((seq, LANES), lambda b, p, i: (b, p)),
            ctx, ctx,
        ],
        out_specs=pl.BlockSpec((tq, yw), lambda b, p, i: (b * nq + i, p)),
        scratch_shapes=[
            pltpu.VMEM((2, grp * tq, LANES), F32),
            pltpu.VMEM((2, grp * tq, LANES), F32),
        ],
        compiler_params=_cparams(("parallel", "parallel", "arbitrary")),
        name="latent_flash_attn",
    )(q, k, v, ck_all, cv_all)


def _window_kernel(sink_ref, q_ref, kp_ref, kc_ref, kn_ref, vp_ref, vc_ref, vn_ref, ck_ref, cv_ref, y_ref,
                   *, n_blocks):
    tq = q_ref.shape[0]
    n = pl.program_id(1)
    low = _low_half(tq)
    grp = C_HEADS // C_KV
    n_loc = tq + 2 * C_WINDOW
    n_keys = n_loc + ck_ref.shape[0]
    r = lax.broadcasted_iota(jnp.int32, (tq, n_keys), 0)
    c = lax.broadcasted_iota(jnp.int32, (tq, n_keys), 1)
    ok = ((jnp.abs(r - (c - C_WINDOW)) <= C_WINDOW)
          & ((c >= C_WINDOW) | (n > 0)) & ((c < C_WINDOW + tq) | (n < n_blocks - 1)))
    mask = jnp.where(ok | (c >= n_loc), 0.0, NEG)
    mask = jnp.concatenate([mask] * grp, axis=0)
    k_all = jnp.concatenate([kp_ref[...], kc_ref[...], kn_ref[...], ck_ref[...]], axis=0)
    v_all = jnp.concatenate([vp_ref[...], vc_ref[...], vn_ref[...], cv_ref[...]], axis=0)
    outs = []
    for kv in range(C_KV):
        heads = list(range(kv * grp, (kv + 1) * grp))
        q = _stack_heads(q_ref, heads)
        sink = _sink_column(sink_ref, heads, tq)
        o = _softmax_pv([_dot_nt(q, k_all) + mask], [v_all], sink)
        outs += [o[g * tq:(g + 1) * tq] for g in range(grp)]
    _place_heads(y_ref, 0, outs, lambda hd: hd // grp, low)


def _window_attn(sink_all, q, k, v, ck_all, cv_all, layer, n_batch):
    rows = q.shape[0]
    seq = rows // n_batch
    past = ck_all.shape[2]
    tq = TQ_WIN
    assert tq % C_WINDOW == 0 and seq % tq == 0
    nb = seq // tq
    per = tq // C_WINDOW
    nw = seq // C_WINDOW
    prev = pl.BlockSpec((C_WINDOW, LANES), lambda b, n: (b * nw + jnp.maximum(n * per - 1, 0), 0))
    cur = pl.BlockSpec((tq, LANES), lambda b, n: (b * nb + n, 0))
    nxt = pl.BlockSpec((C_WINDOW, LANES), lambda b, n: (b * nw + jnp.minimum((n + 1) * per, nw - 1), 0))
    ctx = pl.BlockSpec((None, None, past, LANES), lambda b, n: (b, layer, 0, 0))
    return pl.pallas_call(
        functools.partial(_window_kernel, n_blocks=nb),
        out_shape=jax.ShapeDtypeStruct((rows, C_HEADS * HEAD_DIM), BF16),
        grid=(n_batch, nb),
        in_specs=[
            pl.BlockSpec(memory_space=pltpu.SMEM),
            pl.BlockSpec((tq, C_HEADS * LANES), lambda b, n: (b * nb + n, 0)),
            prev, cur, nxt, prev, cur, nxt, ctx, ctx,
        ],
        out_specs=pl.BlockSpec((tq, C_HEADS * HEAD_DIM), lambda b, n: (b * nb + n, 0)),
        compiler_params=_cparams(("parallel", "parallel")),
        name="latent_window_attn",
    )(sink_all[layer], q, k, k, k, v, v, v, ck_all, cv_all)


def _na_window_start(blk, n_rows):
    return jnp.clip(blk * NA_QROWS - NA_ROWS // 2, 0, n_rows - NA_WIN)


def _na_kernel(q_ref, k_ref, v_ref, ck_ref, cv_ref, pair_ref, y_ref, *, n_rows):
    blk = pl.program_id(1)
    tq = NA_QROWS * GRID_W
    win = NA_WIN * GRID_W
    low = _low_half(tq)
    ws = _na_window_start(blk, n_rows)
    start = pl.multiple_of(ws * GRID_W, GRID_W)
    q_row = blk * NA_QROWS + lax.broadcasted_iota(jnp.int32, (tq, win), 0) // GRID_W
    k_row = ws + lax.broadcasted_iota(jnp.int32, (tq, win), 1) // GRID_W
    rs = jnp.clip(q_row - NA_ROWS // 2, 0, n_rows - NA_ROWS)
    row_mask = jnp.where((k_row >= rs) & (k_row < rs + NA_ROWS), 0.0, NEG)
    outs = []
    for hd in range(D_HEADS):
        sl = slice((hd // 2) * LANES, (hd // 2 + 1) * LANES)
        q = q_ref[:, hd * LANES:(hd + 1) * LANES]
        kw = k_ref[pl.ds(start, win), sl]
        vw = v_ref[pl.ds(start, win), sl]
        bias = jnp.concatenate([
            jnp.concatenate([
                pair_ref[hd, jnp.clip(ws + 2 * jp - (blk * NA_QROWS + a) + NA_ROWS, 0, 2 * NA_ROWS)]
                for jp in range(NA_WIN // 2)], axis=1)
            for a in range(NA_QROWS)], axis=0)
        s_loc = _dot_nt(q, kw) + bias + row_mask
        s_ctx = _dot_nt(q, ck_ref[:, sl])
        outs.append(_softmax_pv([s_loc, s_ctx], [vw, cv_ref[:, sl]]))
    _place_heads(y_ref, 0, outs, lambda hd: hd % 2, low)


def _na_attn(q, k, v, ck_all, cv_all, pair_all, layer, n_batch):
    rows = q.shape[0]
    seq = rows // n_batch
    past = ck_all.shape[2]
    n_rows = seq // GRID_W
    n_blk = n_rows // NA_QROWS
    assert n_rows >= NA_WIN and NA_WIN % 2 == 0
    width = D_HEADS * HEAD_DIM
    tq = NA_QROWS * GRID_W
    full = pl.BlockSpec((seq, width), lambda b, r: (b, 0))
    ctx = pl.BlockSpec((None, None, past, width), lambda b, r: (b, layer, 0, 0))
    return pl.pallas_call(
        functools.partial(_na_kernel, n_rows=n_rows),
        out_shape=jax.ShapeDtypeStruct((rows, width), BF16),
        grid=(n_batch, n_blk),
        in_specs=[
            pl.BlockSpec((tq, D_HEADS * LANES), lambda b, r: (b * n_blk + r, 0)),
            full, full, ctx, ctx,
            _layer_spec(pair_all.shape[1:], layer),
        ],
        out_specs=pl.BlockSpec((tq, width), lambda b, r: (b * n_blk + r, 0)),
        compiler_params=_cparams(("parallel", "arbitrary")),
        name="latent_neighbourhood_attn",
    )(q, k, v, ck_all, cv_all, pair_all)


def _na_pair_tiles(rpb_all):
    n_l, n_h, n_y, n_x = rpb_all.shape
    n_tiles = 2 * NA_ROWS + 1
    zero = jnp.zeros((n_l, n_h, 1, n_x), F32)
    rows = jnp.concatenate([zero, rpb_all, zero, zero], axis=2)
    feats = jnp.concatenate([rows[:, :, :-1], rows[:, :, 1:]], axis=-1)
    feat = np.arange(2 * n_x)[:, None, None]
    qc = np.arange(GRID_W)[None, :, None]
    lane = np.arange(2 * GRID_W)[None, None, :]
    kc = lane % GRID_W
    cs = np.clip(qc - NA_COLS // 2, 0, GRID_W - NA_COLS)
    inside = (kc >= cs) & (kc < cs + NA_COLS)
    select = (lane // GRID_W == feat // n_x) & (feat % n_x == kc - qc + NA_COLS - 1) & inside
    iy = np.arange(n_tiles)[:, None, None] - 1 + lane // GRID_W
    valid = inside & (iy >= 0) & (iy < n_y)
    tiles = jnp.einsum('lhef,fqx->lheqx', feats, jnp.asarray(select, F32), precision=lax.Precision.HIGHEST)
    return jnp.where(valid, tiles, NEG)


def _mixer_ffn_kernel(*refs, seq, halo, widths):
    per = 3 if halo else 1
    n_in = per * (1 + len(widths))
    x_refs = refs[:per]
    part_refs = [refs[per * (1 + k):per * (2 + k)] for k in range(len(widths))]
    (mod_ref, gmix_ref, wmix_ref, gpre_ref, gpost_ref, wi_ref, wc_ref, wo_ref, o_ref, y_ref, h_ref) = refs[n_in:]
    tm = x_refs[per // 2].shape[0]
    pad = HALO if halo else 0
    ext = tm + 2 * pad
    pieces = [(pad, tm)] if not halo else [(0, pad), (pad, tm), (pad + tm, pad)]
    keep = [None]
    if halo:
        pos0 = (pl.program_id(0) * tm) % seq
        keep = [jnp.where(pos0 > 0, 1.0, 0.0), None, jnp.where(pos0 + tm < seq, 1.0, 0.0)]

    off = 0
    for p_refs, wd in zip(part_refs, widths):
        for (r0, n), p_ref in zip(pieces, p_refs):
            y_ref[r0:r0 + n, off:off + wd] = p_ref[...]
        off += wd
    t = _dot(y_ref[...], wmix_ref[...])
    scale, shift = mod_ref[4:5, :], mod_ref[3:4, :]
    for (r0, n), x_ref, kp in zip(pieces, x_refs, keep):
        x1 = x_ref[...] + mod_ref[2:3, :] * _rmsnorm(t[r0:r0 + n], gmix_ref[...])
        h = _norm_mod(x1, gpre_ref[...], scale, shift)
        if kp is None:
            o_ref[...] = x1
        else:
            h = h * kp
        h_ref[r0:r0 + n, :] = h.astype(BF16)

    def conv_up(c0, width):
        u = _dot(h_ref[...], wi_ref[:, c0:c0 + width])
        up = pltpu.roll(u, 1, 0)[pad:pad + tm]
        un = pltpu.roll(u, ext - 1, 0)[pad:pad + tm]
        if not halo:
            sub = lax.broadcasted_iota(jnp.int32, (8, width), 0)
            up = jnp.concatenate([jnp.where(sub == 0, 0.0, up[:8]), up[8:]], axis=0)
            un = jnp.concatenate([un[:-8], jnp.where(sub == 7, 0.0, un[-8:])], axis=0)
        return (u[pad:pad + tm] * wc_ref[1:2, c0:c0 + width] + up * wc_ref[0:1, c0:c0 + width]
                + un * wc_ref[2:3, c0:c0 + width])

    y = None
    for c0, width in FF_CHUNKS:
        gate = conv_up(c0, width)
        val = conv_up(D_FF + c0, width)
        act = gate * (1.0 / (1.0 + jnp.exp(-gate))) * val
        contrib = _dot(act.astype(BF16), wo_ref[c0:c0 + width, :])
        y = contrib if y is None else y + contrib
    o_ref[...] += mod_ref[5:6, :] * _rmsnorm(y, gpost_ref[...])


def _mixer_ffn(x, parts, mods, layer, g_mix_all, w_mix_all, g_pre_all, g_post_all, wi_all, wc_all, wo_all,
               seq, seq_blocks):
    rows = x.shape[0]
    tm = min(TM_FFN, seq)
    assert seq % tm == 0
    halo = tm < seq
    widths = tuple(p.shape[1] for p in parts)
    assert sum(widths) == D_MODEL
    resident = dict(pipeline_mode=pl.Buffered(1))
    zeros3 = lambda i: (layer, 0, 0)
    row_specs = lambda wd: (_halo_specs(tm, wd, rows) if halo else [pl.BlockSpec((tm, wd), lambda i: (i, 0))])
    in_specs, args = [], []
    for arr in (x,) + tuple(parts):
        specs = row_specs(arr.shape[1])
        in_specs += specs
        args += [arr] * len(specs)
    in_specs += [
        _mod_spec(layer, seq_blocks),
        _layer_spec((1, D_MODEL), layer),
        _layer_spec((D_MODEL, D_MODEL), layer),
        _layer_spec((1, D_MODEL), layer),
        _layer_spec((1, D_MODEL), layer),
        pl.BlockSpec((None, D_MODEL, 2 * D_FF), zeros3, **resident),
        _layer_spec((3, 2 * D_FF), layer),
        pl.BlockSpec((None, D_FF, D_MODEL), zeros3, **resident),
    ]
    args += [mods, g_mix_all, w_mix_all, g_pre_all, g_post_all, wi_all, wc_all, wo_all]
    ext = tm + (2 * HALO if halo else 0)
    return pl.pallas_call(
        functools.partial(_mixer_ffn_kernel, seq=seq, halo=halo, widths=widths),
        out_shape=jax.ShapeDtypeStruct((rows, D_MODEL), F32),
        grid=(rows // tm,),
        in_specs=in_specs,
        out_specs=pl.BlockSpec((tm, D_MODEL), lambda i: (i, 0)),
        scratch_shapes=[pltpu.VMEM((ext, D_MODEL), BF16), pltpu.VMEM((ext, D_MODEL), BF16)],
        compiler_params=_cparams(("parallel",)),
        name="mixer_ffn",
    )(*args)


def _rope_tables(n_tokens):
    t = jnp.arange(n_tokens)
    row = (t // GRID_W).astype(F32)
    col = (t % GRID_W).astype(F32)
    axis_dim = HEAD_DIM // 2
    inv_freq = ROPE_THETA ** (-jnp.arange(0, axis_dim, 2, dtype=F32) / axis_dim)
    ang = jnp.concatenate([row[:, None] * inv_freq, col[:, None] * inv_freq], axis=-1)
    cos = jnp.repeat(jnp.cos(ang), 2, axis=-1)
    sin = jnp.sin(ang)
    sin = jnp.stack([-sin, sin], axis=-1).reshape(n_tokens, HEAD_DIM)
    return jnp.tile(cos, (1, 2)), jnp.tile(sin, (1, 2))


def _rows3(p):
    return p.reshape(p.shape[0], 1, p.shape[1])


def kernel(x_prompt, x_sample, cache_b_k, cache_b_v, cache_c_k, cache_c_v, cache_d_k, cache_d_v, c, c_ctx,
           w_mod, b_mod, g_mix_pre, g_mix_post, g_ffn_pre, g_ffn_post, w_in_even, w_pool, pool_scale,
           g_q_b, g_k_b, w_in_odd, sink_c, rpb_d, w_mix_out, w_ffn_in, w_ffn_conv, w_ffn_out):
    n_ctx, ctx_seq, _ = x_prompt.shape
    n_lat, lat_seq, _ = x_sample.shape
    past = cache_b_k.shape[2]
    xp = x_prompt.reshape(n_ctx * ctx_seq, D_MODEL)
    xs = x_sample.reshape(n_lat * lat_seq, D_MODEL)
    lat_blocks = lat_seq // TM_PROJ

    cvecs = jnp.concatenate([c_ctx[None, :], c, jnp.zeros((8 - 1 - n_lat, D_MODEL), F32)], axis=0)
    mods = _modulation(cvecs, w_mod, b_mod).reshape(DEPTH, 8, 6, D_MODEL)

    rope_tabs = _rope_tables(lat_seq)
    blk = jnp.arange(LANES) // HEAD_DIM
    e_mat = jnp.where(blk[:, None] == blk[None, :], 1.0 / HEAD_DIM, 0.0).astype(BF16)

    w_even = w_in_even.astype(BF16)
    w_odd = w_in_odd.astype(BF16)
    w_out = w_mix_out.astype(BF16)
    w_fi = w_ffn_in.astype(BF16)
    w_fo = w_ffn_out.astype(BF16)
    slabs = lambda t: t.astype(BF16).reshape(t.shape[0], t.shape[1], past, t.shape[3] * HEAD_DIM)
    cbk, cbv, cck, ccv, cdk, cdv = map(slabs, (cache_b_k, cache_b_v, cache_c_k, cache_c_v, cache_d_k, cache_d_v))
    g_pre, g_post = _rows3(g_mix_pre), _rows3(g_mix_post)
    gf_pre, gf_post = _rows3(g_ffn_pre), _rows3(g_ffn_post)
    pscale = _rows3(pool_scale)
    na_pairs = _na_pair_tiles(rpb_d)

    b_k, b_v, c_k, c_v, d_k, d_v = [], [], [], [], [], []
    for l in range(DEPTH):
        i = l // 2
        if l % 2 == 0:
            gq = jnp.tile(g_q_b[i], 2)[None, :]
            gk = jnp.tile(g_k_b[i], 2)[None, :]
            kvw = B_KV * HEAD_DIM
            widths = (POOL_WIDTH, B_HEADS * LANES, kvw, kvw)
            a_p, q_p, k_p, v_p = _proj_in(xp, mods, l, g_pre, w_even, i, (gq, gk, e_mat), widths,
                                          (F32, BF16, F32, F32),
                                          functools.partial(_proj_even_kernel, q_scale=QK_SCALE), None, None,
                                          "proj_even_ctx")
            a_s, q_s, k_s, v_s = _proj_in(xs, mods, l, g_pre, w_even, i, (gq, gk, e_mat), widths,
                                          (F32, BF16, BF16, BF16),
                                          functools.partial(_proj_even_kernel, q_scale=QK_SCALE * LOG2E),
                                          lat_blocks, rope_tabs,
                                          "proj_even_lat")
            b_k.append(k_p.reshape(n_ctx, ctx_seq, B_KV, HEAD_DIM))
            b_v.append(v_p.reshape(n_ctx, ctx_seq, B_KV, HEAD_DIM))
            w_bd = jax.scipy.linalg.block_diag(*[w_pool[i, g] for g in range(len(POOL_WINDOWS))]).astype(BF16)
            ya_p = _pool_mixer(a_p, w_bd, pscale, i, ctx_seq)
            ya_s = _pool_mixer(a_s, w_bd, pscale, i, lat_seq)
            yb_p = _ctx_even_attn(q_p, k_p, v_p, ctx_seq)
            yb_s = _flash_attn(q_s, k_s, v_s, cbk, cbv, i, n_lat)
            parts_p, parts_s = (ya_p, yb_p), (ya_s, yb_s)
        else:
            ckw, dw = C_KV * HEAD_DIM, D_HEADS * HEAD_DIM
            widths = (C_HEADS * LANES, ckw, ckw, D_HEADS * LANES, dw, dw)
            qc_p, kc_p, vc_p, qd_p, kd_p, vd_p = _proj_in(
                xp, mods, l, g_pre, w_odd, i, (), widths, (BF16, F32, F32, BF16, F32, F32),
                _proj_odd_kernel, None, None, "proj_odd_ctx")
            qc_s, kc_s, vc_s, qd_s, kd_s, vd_s = _proj_in(
                xs, mods, l, g_pre, w_odd, i, (), widths, (BF16,) * 6,
                _proj_odd_kernel, lat_blocks, rope_tabs, "proj_odd_lat")
            c_k.append(kc_p.reshape(n_ctx, ctx_seq, C_KV, HEAD_DIM))
            c_v.append(vc_p.reshape(n_ctx, ctx_seq, C_KV, HEAD_DIM))
            d_k.append(kd_p.reshape(n_ctx, ctx_seq, D_HEADS, HEAD_DIM))
            d_v.append(vd_p.reshape(n_ctx, ctx_seq, D_HEADS, HEAD_DIM))
            y_p = _ctx_odd_attn(sink_c, i, qc_p, kc_p, vc_p, qd_p, kd_p, vd_p, ctx_seq)
            yc_s = _window_attn(sink_c, qc_s, kc_s, vc_s, cck, ccv, i, n_lat)
            yd_s = _na_attn(qd_s, kd_s, vd_s, cdk, cdv, na_pairs, i, n_lat)
            parts_p, parts_s = (y_p,), (yc_s, yd_s)

        xp = _mixer_ffn(xp, parts_p, mods, l, g_post, w_out, gf_pre, gf_post, w_fi, w_ffn_conv, w_fo,
                        ctx_seq, None)
        xs = _mixer_ffn(xs, parts_s, mods, l, g_post, w_out, gf_pre, gf_post, w_fi, w_ffn_conv, w_fo,
                        lat_seq, lat_seq // TM_FFN)

    return (xp.reshape(n_ctx, ctx_seq, D_MODEL), xs.reshape(n_lat, lat_seq, D_MODEL),
            jnp.stack(b_k, axis=1), jnp.stack(b_v, axis=1),
            jnp.stack(c_k, axis=1), jnp.stack(c_v, axis=1),
            jnp.stack(d_k, axis=1), jnp.stack(d_v, axis=1))
```

```python
import functools

import jax
import jax.numpy as jnp
import numpy as np
from jax import lax
from jax.experimental import pallas as pl
from jax.experimental.pallas import tpu as pltpu

F32 = jnp.float32
BF16 = jnp.bfloat16

D_MODEL = 1024
DEPTH = 4
GRID_W = 64
HEAD_DIM = 64
EPS = 1e-6
ROPE_THETA = 10000.0
POOL_WINDOWS = (2, 4, 8, 16)
POOL_WIDTH = 256
B_HEADS, B_KV = 12, 4
C_HEADS, C_KV = 8, 2
D_HEADS = 8
C_WINDOW = 128
NA_ROWS, NA_COLS = 8, 16
D_FF = 2816
QK_SCALE = HEAD_DIM ** -0.5
LOG2E = 1.4426950408889634
NEG = -1e30

LANES = 128
HALO = 16
VMEM_LIMIT = 56 * 1024 * 1024

TM_PROJ = 1024
TM_FFN = 512
TM_POOL = 256
MXU_TILE = 256
_FF_TILES = D_FF // MXU_TILE
FF_CHUNKS = ((0, (_FF_TILES + 1) // 2 * MXU_TILE),
             ((_FF_TILES + 1) // 2 * MXU_TILE, _FF_TILES // 2 * MXU_TILE))
assert D_FF % MXU_TILE == 0 and sum(w for _, w in FF_CHUNKS) == D_FF
TQ_FLASH = 256
TK_FLASH = 512
TQ_WIN = 256
CTX_SEQS = 2
NA_QROWS = 4
NA_WIN = NA_ROWS + NA_QROWS


def _cparams(sem):
    return pltpu.CompilerParams(dimension_semantics=sem, vmem_limit_bytes=VMEM_LIMIT)


def _norm_mod(x, g, scale, shift):
    ms = jnp.mean(x * x, axis=-1, keepdims=True)
    return (x * lax.rsqrt(ms + EPS) * g) * (1.0 + scale) + shift


def _rmsnorm(x, g):
    ms = jnp.mean(x * x, axis=-1, keepdims=True)
    return x * lax.rsqrt(ms + EPS) * g


def _dot(a, b):
    return jnp.dot(a, b, preferred_element_type=F32)


def _dot_nt(a, b):
    return lax.dot_general(a, b, (((1,), (1,)), ((), ())), preferred_element_type=F32)


def _low_half(rows):
    return lax.broadcasted_iota(jnp.int32, (rows, LANES), 1) < HEAD_DIM


def _layer_spec(shape, layer):
    zeros = (0,) * len(shape)
    return pl.BlockSpec((None,) + tuple(shape), lambda *_: (layer,) + zeros)


def _mod_spec(layer, seq_blocks):
    if seq_blocks is None:
        return pl.BlockSpec((None, None, 6, D_MODEL), lambda i: (layer, 0, 0, 0))
    return pl.BlockSpec((None, None, 6, D_MODEL), lambda i: (layer, 1 + i // seq_blocks, 0, 0))


def _mod_kernel(c_ref, w_ref, b_ref, o_ref):
    cv = c_ref[...]
    s = cv * (1.0 / (1.0 + jnp.exp(-cv)))
    o_ref[...] = _dot(s.astype(BF16), w_ref[...].astype(BF16)) + b_ref[...]


def _modulation(cvecs, w_mod, b_mod):
    tn = 1536
    return pl.pallas_call(
        _mod_kernel,
        out_shape=jax.ShapeDtypeStruct((DEPTH, 8, 6 * D_MODEL), F32),
        grid=(DEPTH, 6 * D_MODEL // tn),
        in_specs=[
            pl.BlockSpec((8, D_MODEL), lambda l, j: (0, 0)),
            pl.BlockSpec((None, D_MODEL, tn), lambda l, j: (l, 0, j)),
            pl.BlockSpec((None, 1, tn), lambda l, j: (l, 0, j)),
        ],
        out_specs=pl.BlockSpec((None, 8, tn), lambda l, j: (l, 0, j)),
        compiler_params=_cparams(("parallel", "parallel")),
        name="modulation",
    )(cvecs, w_mod, b_mod.reshape(DEPTH, 1, 6 * D_MODEL))


def _pair_swap(s):
    lane = lax.broadcasted_iota(jnp.int32, s.shape, 1)
    n = s.shape[1]
    return jnp.where(lane % 2 == 0, pltpu.roll(s, n - 1, 1), pltpu.roll(s, 1, 1))


def _store_expanded_q(q_ref, s, slab, kv_half_of, low):
    sr = pltpu.roll(s, HEAD_DIM, 1)
    for half in (0, 1):
        hd = 2 * slab + half
        dst = kv_half_of(hd)
        val = s if dst == half else sr
        keep = low if dst == 0 else jnp.logical_not(low)
        q_ref[:, hd * LANES:(hd + 1) * LANES] = jnp.where(keep, val, 0.0).astype(BF16)


def _proj_even_kernel(*refs, rope, q_scale):
    if rope:
        (x_ref, mod_ref, g_ref, w_ref, gq_ref, gk_ref, e_ref, cos_ref, sin_ref,
         a_ref, q_ref, k_ref, v_ref) = refs
    else:
        (x_ref, mod_ref, g_ref, w_ref, gq_ref, gk_ref, e_ref,
         a_ref, q_ref, k_ref, v_ref) = refs
    rows = x_ref.shape[0]
    h = _norm_mod(x_ref[...], g_ref[...], mod_ref[1:2, :], mod_ref[0:1, :])
    u = _dot(h.astype(BF16), w_ref[...])
    low = _low_half(rows)
    a_ref[...] = u[:, :POOL_WIDTH]

    def headnorm(s, g):
        ms = _dot((s * s).astype(BF16), e_ref[...])
        return s * lax.rsqrt(ms + EPS) * g

    def rot(s):
        if not rope:
            return s
        return s * cos_ref[...] + _pair_swap(s) * sin_ref[...]

    q0 = POOL_WIDTH
    for j in range(B_HEADS // 2):
        s = u[:, q0 + LANES * j:q0 + LANES * (j + 1)]
        s = rot(headnorm(s, gq_ref[...])) * q_scale
        _store_expanded_q(q_ref, s, j, lambda hd: (hd // (B_HEADS // B_KV)) % 2, low)
    k0 = q0 + B_HEADS * HEAD_DIM
    for j in range(B_KV // 2):
        s = headnorm(u[:, k0 + LANES * j:k0 + LANES * (j + 1)], gk_ref[...])
        k_ref[:, LANES * j:LANES * (j + 1)] = rot(s).astype(k_ref.dtype)
    v0 = k0 + B_KV * HEAD_DIM
    v_ref[...] = u[:, v0:v0 + B_KV * HEAD_DIM].astype(v_ref.dtype)


def _proj_odd_kernel(*refs, rope):
    if rope:
        (x_ref, mod_ref, g_ref, w_ref, cos_ref, sin_ref,
         qc_ref, kc_ref, vc_ref, qd_ref, kd_ref, vd_ref) = refs
    else:
        (x_ref, mod_ref, g_ref, w_ref,
         qc_ref, kc_ref, vc_ref, qd_ref, kd_ref, vd_ref) = refs
    rows = x_ref.shape[0]
    h = _norm_mod(x_ref[...], g_ref[...], mod_ref[1:2, :], mod_ref[0:1, :])
    u = _dot(h.astype(BF16), w_ref[...])
    low = _low_half(rows)

    def rot(s):
        if not rope:
            return s
        return s * cos_ref[...] + _pair_swap(s) * sin_ref[...]

    for j in range(C_HEADS // 2):
        s = rot(u[:, LANES * j:LANES * (j + 1)]) * QK_SCALE
        _store_expanded_q(qc_ref, s, j, lambda hd: hd // (C_HEADS // C_KV), low)
    o = C_HEADS * HEAD_DIM
    kc_ref[...] = rot(u[:, o:o + LANES]).astype(kc_ref.dtype)
    o += C_KV * HEAD_DIM
    vc_ref[...] = u[:, o:o + LANES].astype(vc_ref.dtype)
    o += C_KV * HEAD_DIM
    for j in range(D_HEADS // 2):
        s = u[:, o + LANES * j:o + LANES * (j + 1)] * QK_SCALE
        _store_expanded_q(qd_ref, s, j, lambda hd: hd % 2, low)
    o += D_HEADS * HEAD_DIM
    kd_ref[...] = u[:, o:o + D_HEADS * HEAD_DIM].astype(kd_ref.dtype)
    o += D_HEADS * HEAD_DIM
    vd_ref[...] = u[:, o:o + D_HEADS * HEAD_DIM].astype(vd_ref.dtype)


def _proj_in(x, mods, layer, g_all, w_all, w_layer, extras, out_widths, out_dtypes, kernel, seq_blocks,
             rope_tabs, name):
    rows = x.shape[0]
    tm = TM_PROJ
    const = lambda i: (0, 0)
    in_specs = [
        pl.BlockSpec((tm, D_MODEL), lambda i: (i, 0)),
        _mod_spec(layer, seq_blocks),
        _layer_spec((1, D_MODEL), layer),
        _layer_spec(w_all.shape[1:], w_layer),
    ]
    args = [x, mods, g_all, w_all]
    for e in extras:
        in_specs.append(pl.BlockSpec(e.shape, const))
        args.append(e)
    if rope_tabs is not None:
        for t in rope_tabs:
            in_specs.append(pl.BlockSpec((tm, LANES), lambda i: (i % seq_blocks, 0)))
            args.append(t)
    return pl.pallas_call(
        functools.partial(kernel, rope=rope_tabs is not None),
        out_shape=[jax.ShapeDtypeStruct((rows, wd), dt) for wd, dt in zip(out_widths, out_dtypes)],
        grid=(rows // tm,),
        in_specs=in_specs,
        out_specs=[pl.BlockSpec((tm, wd), lambda i: (i, 0)) for wd in out_widths],
        compiler_params=_cparams(("parallel",)),
        name=name,
    )(*args)


def _place_heads(out_ref, col0, heads, kv_half_of, low):
    for m in range(len(heads) // 2):
        a = heads[2 * m]
        if kv_half_of(2 * m) == 1:
            a = pltpu.roll(a, HEAD_DIM, 1)
        b = heads[2 * m + 1]
        if kv_half_of(2 * m + 1) == 0:
            b = pltpu.roll(b, HEAD_DIM, 1)
        out_ref[:, col0 + m * LANES:col0 + (m + 1) * LANES] = jnp.where(low, a, b).astype(out_ref.dtype)


def _stack_heads(q_ref, heads):
    return jnp.concatenate([q_ref[:, h * LANES:(h + 1) * LANES] for h in heads], axis=0)


def _softmax_pv(scores, values, sink=None):
    s = scores[0] if len(scores) == 1 else jnp.concatenate(scores, axis=1)
    v = values[0] if len(values) == 1 else jnp.concatenate(values, axis=0)
    m = jnp.broadcast_to(s.max(axis=-1, keepdims=True), (s.shape[0], LANES))
    if sink is not None:
        m = jnp.maximum(m, sink)
    e = jnp.exp(s - jnp.concatenate([m] * (s.shape[1] // LANES), axis=1))
    acc = _dot(e.astype(BF16), jnp.concatenate([v, jnp.ones_like(v)], axis=1))
    den = acc[:, LANES:]
    if sink is not None:
        den = den + jnp.exp(sink - m)
    return acc[:, :LANES] / den


def _sink_column(sink_ref, heads, rows_per_head):
    row = lax.broadcasted_iota(jnp.int32, (len(heads) * rows_per_head, LANES), 0)
    col = jnp.full(row.shape, sink_ref[heads[-1]], F32)
    for g in range(len(heads) - 2, -1, -1):
        col = jnp.where(row < (g + 1) * rows_per_head, sink_ref[heads[g]], col)
    return col


def _ctx_even_attn_seq(q_ref, k_ref, v_ref, y_ref):
    rows = q_ref.shape[0]
    low = _low_half(rows)
    grp = B_HEADS // B_KV
    outs = []
    for kv in range(B_KV):
        sl = slice((kv // 2) * LANES, (kv // 2 + 1) * LANES)
        ks = k_ref[:, sl].astype(BF16)
        vs = v_ref[:, sl].astype(BF16)
        heads = list(range(kv * grp, (kv + 1) * grp))
        o = _softmax_pv([_dot_nt(_stack_heads(q_ref, heads), ks)], [vs])
        outs += [o[g * rows:(g + 1) * rows] for g in range(grp)]
    _place_heads(y_ref, 0, outs, lambda hd: (hd // grp) % 2, low)


def _ctx_odd_attn_seq(sink_ref, qc_ref, kc_ref, vc_ref, qd_ref, kd_ref, vd_ref, y_ref):
    rows = qc_ref.shape[0]
    low = _low_half(rows)
    grp = C_HEADS // C_KV
    ks = kc_ref[...].astype(BF16)
    vs = vc_ref[...].astype(BF16)
    outs = []
    for kv in range(C_KV):
        heads = list(range(kv * grp, (kv + 1) * grp))
        sink = _sink_column(sink_ref, heads, rows)
        o = _softmax_pv([_dot_nt(_stack_heads(qc_ref, heads), ks)], [vs], sink)
        outs += [o[g * rows:(g + 1) * rows] for g in range(grp)]
    _place_heads(y_ref, 0, outs, lambda hd: hd // grp, low)
    outs = []
    for hd in range(D_HEADS):
        sl = slice((hd // 2) * LANES, (hd // 2 + 1) * LANES)
        s = _dot_nt(qd_ref[:, hd * LANES:(hd + 1) * LANES], kd_ref[:, sl].astype(BF16))
        outs.append(_softmax_pv([s], [vd_ref[:, sl].astype(BF16)]))
    _place_heads(y_ref, C_HEADS * HEAD_DIM, outs, lambda hd: hd % 2, low)


def _per_sequence(*refs, fn, seq, n_scalar=0):
    for i in range(refs[n_scalar].shape[0] // seq):
        rows = slice(i * seq, (i + 1) * seq)
        fn(*refs[:n_scalar], *[r.at[rows] for r in refs[n_scalar:]])


def _ctx_even_attn(q, k, v, seq):
    rows = q.shape[0]
    blk = lambda wd: pl.BlockSpec((CTX_SEQS * seq, wd), lambda b: (b, 0))
    return pl.pallas_call(
        functools.partial(_per_sequence, fn=_ctx_even_attn_seq, seq=seq),
        out_shape=jax.ShapeDtypeStruct((rows, B_HEADS * HEAD_DIM), BF16),
        grid=(rows // (CTX_SEQS * seq),),
        in_specs=[blk(q.shape[1]), blk(k.shape[1]), blk(v.shape[1])],
        out_specs=blk(B_HEADS * HEAD_DIM),
        compiler_params=_cparams(("parallel",)),
        name="ctx_even_attn",
    )(q, k, v)


def _ctx_odd_attn(sink_all, layer, qc, kc, vc, qd, kd, vd, seq):
    rows = qc.shape[0]
    blk = lambda wd: pl.BlockSpec((CTX_SEQS * seq, wd), lambda b: (b, 0))
    return pl.pallas_call(
        functools.partial(_per_sequence, fn=_ctx_odd_attn_seq, seq=seq, n_scalar=1),
        out_shape=jax.ShapeDtypeStruct((rows, D_MODEL), BF16),
        grid=(rows // (CTX_SEQS * seq),),
        in_specs=[pl.BlockSpec(memory_space=pltpu.SMEM)] + [blk(a.shape[1]) for a in (qc, kc, vc, qd, kd, vd)],
        out_specs=blk(D_MODEL),
        compiler_params=_cparams(("parallel",)),
        name="ctx_odd_attn",
    )(sink_all[layer], qc, kc, vc, qd, kd, vd)


def _pool_kernel(ap_ref, a_ref, an_ref, w_ref, ps_ref, y_ref, *, seq):
    tm = a_ref.shape[0]
    i = pl.program_id(0)
    pos0 = (i * tm) % seq
    a = a_ref[...]
    a_ext = jnp.concatenate([ap_ref[...], a, an_ref[...]], axis=0).astype(BF16)
    ext = tm + 2 * HALO
    r = lax.broadcasted_iota(jnp.int32, (tm, ext), 0)
    c = lax.broadcasted_iota(jnp.int32, (tm, ext), 1)
    pos_c = pos0 - HALO + c
    in_seq = (pos_c >= 0) & (pos_c < seq)
    d = c - HALO - r
    pos_r = pos0 + lax.broadcasted_iota(jnp.int32, (tm, POOL_WIDTH), 0)
    grp = lax.broadcasted_iota(jnp.int32, (tm, POOL_WIDTH), 1) // (POOL_WIDTH // len(POOL_WINDOWS))
    mean = jnp.zeros((tm, POOL_WIDTH), F32)
    for gi, wdw in enumerate(POOL_WINDOWS):
        band = jnp.where(in_seq & (d >= -(wdw // 2)) & (d < wdw - wdw // 2), 1.0, 0.0).astype(BF16)
        lo = jnp.maximum(pos_r - wdw // 2, 0)
        hi = jnp.minimum(pos_r - wdw // 2 + wdw, seq)
        win_mean = _dot(band, a_ext) / (hi - lo).astype(F32)
        mean = jnp.where(grp == gi, win_mean, mean)
    pooled = (mean - a).astype(BF16)
    y_ref[...] = (_dot(pooled, w_ref[...]) * ps_ref[...]).astype(y_ref.dtype)


def _halo_specs(tm, width, n_rows):
    per = tm // HALO
    last = n_rows // HALO - 1
    prev = pl.BlockSpec((HALO, width), lambda i: (jnp.maximum(i * per - 1, 0), 0))
    cur = pl.BlockSpec((tm, width), lambda i: (i, 0))
    nxt = pl.BlockSpec((HALO, width), lambda i: (jnp.minimum((i + 1) * per, last), 0))
    return [prev, cur, nxt]


def _pool_mixer(a, w_bd, pscale_all, layer, seq):
    rows = a.shape[0]
    tm = TM_POOL
    assert seq % tm == 0
    const = lambda i: (0, 0)
    return pl.pallas_call(
        functools.partial(_pool_kernel, seq=seq),
        out_shape=jax.ShapeDtypeStruct((rows, POOL_WIDTH), BF16),
        grid=(rows // tm,),
        in_specs=_halo_specs(tm, POOL_WIDTH, rows) + [
            pl.BlockSpec((POOL_WIDTH, POOL_WIDTH), const),
            _layer_spec((1, POOL_WIDTH), layer),
        ],
        out_specs=pl.BlockSpec((tm, POOL_WIDTH), lambda i: (i, 0)),
        compiler_params=_cparams(("parallel",)),
        name="pool_mixer",
    )(a, a, a, w_bd, pscale_all)


def _flash_kernel(q_ref, k_ref, v_ref, ck_ref, cv_ref, y_ref, m_ref, acc_ref):
    tq = q_ref.shape[0]
    low = _low_half(tq)
    grp = B_HEADS // B_KV
    n_chunks = k_ref.shape[0] // TK_FLASH
    qs = [_stack_heads(q_ref, list(range(j * grp, (j + 1) * grp))) for j in range(2)]
    def step(kc, vc, first=False):
        lowk = _low_half(vc.shape[0])
        for j in range(2):
            own = lowk if j == 0 else jnp.logical_not(lowk)
            vj = jnp.where(own, vc, jnp.ones_like(vc))
            s = _dot_nt(qs[j], kc)
            m_new = s.max(axis=-1, keepdims=True)
            m_new = jnp.broadcast_to(m_new, (s.shape[0], LANES)) if first else jnp.maximum(m_ref[j], m_new)
            p = jnp.exp2((s - jnp.concatenate([m_new] * (s.shape[1] // LANES), axis=1)).astype(BF16))
            pv = _dot(p, vj)
            acc_ref[j] = pv if first else jnp.exp2(m_ref[j] - m_new) * acc_ref[j] + pv
            m_ref[j] = m_new

    def body(c, carry):
        off = pl.multiple_of(c * TK_FLASH, TK_FLASH)
        step(k_ref[pl.ds(off, TK_FLASH), :], v_ref[pl.ds(off, TK_FLASH), :])
        return carry

    step(ck_ref[...], cv_ref[...], first=True)
    lax.fori_loop(0, n_chunks, body, 0, unroll=4)
    outs = []
    for j in range(2):
        acc = acc_ref[j]
        o = acc / pltpu.roll(acc, HEAD_DIM, 1)
        outs += [o[g * tq:(g + 1) * tq] for g in range(grp)]
    _place_heads(y_ref, 0, outs, lambda hd: hd // grp, low)


def _flash_attn(q, k, v, ck_all, cv_all, layer, n_batch):
    rows = q.shape[0]
    seq = rows // n_batch
    past = ck_all.shape[2]
    tq = TQ_FLASH
    nq = seq // tq
    grp = B_HEADS // B_KV
    qw = 2 * grp * LANES
    yw = 2 * grp * HEAD_DIM
    ctx = pl.BlockSpec((None, None, past, LANES), lambda b, p, i: (b, layer, 0, p))
    return pl.pallas_call(
        _flash_kernel,
        out_shape=jax.ShapeDtypeStruct((rows, B_HEADS * HEAD_DIM), BF16),
        grid=(n_batch, B_KV // 2, nq),
        in_specs=[
            pl.BlockSpec((tq, qw), lambda b, p, i: (b * nq + i, p)),
            pl.BlockSpec((seq, LANES), lambda b, p, i: (b, p)),
            pl.BlockSpec((seq, LANES), lambda b, p, i: (b, p)),
            ctx, ctx,
        ],
        out_specs=pl.BlockSpec((tq, yw), lambda b, p, i: (b * nq + i, p)),
        scratch_shapes=[
            pltpu.VMEM((2, grp * tq, LANES), F32),
            pltpu.VMEM((2, grp * tq, LANES), F32),
        ],
        compiler_params=_cparams(("parallel", "parallel", "arbitrary")),
        name="latent_flash_attn",
    )(q, k, v, ck_all, cv_all)


def _window_kernel(sink_ref, q_ref, kp_ref, kc_ref, kn_ref, vp_ref, vc_ref, vn_ref, ck_ref, cv_ref, y_ref,
                   *, n_blocks):
    tq = q_ref.shape[0]
    n = pl.program_id(1)
    low = _low_half(tq)
    grp = C_HEADS // C_KV
    n_loc = tq + 2 * C_WINDOW
    n_keys = n_loc + ck_ref.shape[0]
    r = lax.broadcasted_iota(jnp.int32, (tq, n_keys), 0)
    c = lax.broadcasted_iota(jnp.int32, (tq, n_keys), 1)
    ok = ((jnp.abs(r - (c - C_WINDOW)) <= C_WINDOW)
          & ((c >= C_WINDOW) | (n > 0)) & ((c < C_WINDOW + tq) | (n < n_blocks - 1)))
    mask = jnp.where(ok | (c >= n_loc), 0.0, NEG)
    mask = jnp.concatenate([mask] * grp, axis=0)
    k_all = jnp.concatenate([kp_ref[...], kc_ref[...], kn_ref[...], ck_ref[...]], axis=0)
    v_all = jnp.concatenate([vp_ref[...], vc_ref[...], vn_ref[...], cv_ref[...]], axis=0)
    outs = []
    for kv in range(C_KV):
        heads = list(range(kv * grp, (kv + 1) * grp))
        q = _stack_heads(q_ref, heads)
        sink = _sink_column(sink_ref, heads, tq)
        o = _softmax_pv([_dot_nt(q, k_all) + mask], [v_all], sink)
        outs += [o[g * tq:(g + 1) * tq] for g in range(grp)]
    _place_heads(y_ref, 0, outs, lambda hd: hd // grp, low)


def _window_attn(sink_all, q, k, v, ck_all, cv_all, layer, n_batch):
    rows = q.shape[0]
    seq = rows // n_batch
    past = ck_all.shape[2]
    tq = TQ_WIN
    assert tq % C_WINDOW == 0 and seq % tq == 0
    nb = seq // tq
    per = tq // C_WINDOW
    nw = seq // C_WINDOW
    prev = pl.BlockSpec((C_WINDOW, LANES), lambda b, n: (b * nw + jnp.maximum(n * per - 1, 0), 0))
    cur = pl.BlockSpec((tq, LANES), lambda b, n: (b * nb + n, 0))
    nxt = pl.BlockSpec((C_WINDOW, LANES), lambda b, n: (b * nw + jnp.minimum((n + 1) * per, nw - 1), 0))
    ctx = pl.BlockSpec((None, None, past, LANES), lambda b, n: (b, layer, 0, 0))
    return pl.pallas_call(
        functools.partial(_window_kernel, n_blocks=nb),
        out_shape=jax.ShapeDtypeStruct((rows, C_HEADS * HEAD_DIM), BF16),
        grid=(n_batch, nb),
        in_specs=[
            pl.BlockSpec(memory_space=pltpu.SMEM),
            pl.BlockSpec((tq, C_HEADS * LANES), lambda b, n: (b * nb + n, 0)),
            prev, cur, nxt, prev, cur, nxt, ctx, ctx,
        ],
        out_specs=pl.BlockSpec((tq, C_HEADS * HEAD_DIM), lambda b, n: (b * nb + n, 0)),
        compiler_params=_cparams(("parallel", "parallel")),
        name="latent_window_attn",
    )(sink_all[layer], q, k, k, k, v, v, v, ck_all, cv_all)


def _na_window_start(blk, n_rows):
    return jnp.clip(blk * NA_QROWS - NA_ROWS // 2, 0, n_rows - NA_WIN)


def _na_kernel(q_ref, k_ref, v_ref, ck_ref, cv_ref, pair_ref, y_ref, *, n_rows):
    blk = pl.program_id(1)
    tq = NA_QROWS * GRID_W
    win = NA_WIN * GRID_W
    low = _low_half(tq)
    ws = _na_window_start(blk, n_rows)
    start = pl.multiple_of(ws * GRID_W, GRID_W)
    q_row = blk * NA_QROWS + lax.broadcasted_iota(jnp.int32, (tq, win), 0) // GRID_W
    k_row = ws + lax.broadcasted_iota(jnp.int32, (tq, win), 1) // GRID_W
    rs = jnp.clip(q_row - NA_ROWS // 2, 0, n_rows - NA_ROWS)
    row_mask = jnp.where((k_row >= rs) & (k_row < rs + NA_ROWS), 0.0, NEG)
    outs = []
    for hd in range(D_HEADS):
        sl = slice((hd // 2) * LANES, (hd // 2 + 1) * LANES)
        q = q_ref[:, hd * LANES:(hd + 1) * LANES]
        kw = k_ref[pl.ds(start, win), sl]
        vw = v_ref[pl.ds(start, win), sl]
        bias = jnp.concatenate([
            jnp.concatenate([
                pair_ref[hd, jnp.clip(ws + 2 * jp - (blk * NA_QROWS + a) + NA_ROWS, 0, 2 * NA_ROWS)]
                for jp in range(NA_WIN // 2)], axis=1)
            for a in range(NA_QROWS)], axis=0)
        s_loc = _dot_nt(q, kw) + bias + row_mask
        s_ctx = _dot_nt(q, ck_ref[:, sl])
        outs.append(_softmax_pv([s_loc, s_ctx], [vw, cv_ref[:, sl]]))
    _place_heads(y_ref, 0, outs, lambda hd: hd % 2, low)


def _na_attn(q, k, v, ck_all, cv_all, pair_all, layer, n_batch):
    rows = q.shape[0]
    seq = rows // n_batch
    past = ck_all.shape[2]
    n_rows = seq // GRID_W
    n_blk = n_rows // NA_QROWS
    assert n_rows >= NA_WIN and NA_WIN % 2 == 0
    width = D_HEADS * HEAD_DIM
    tq = NA_QROWS * GRID_W
    full = pl.BlockSpec((seq, width), lambda b, r: (b, 0))
    ctx = pl.BlockSpec((None, None, past, width), lambda b, r: (b, layer, 0, 0))
    return pl.pallas_call(
        functools.partial(_na_kernel, n_rows=n_rows),
        out_shape=jax.ShapeDtypeStruct((rows, width), BF16),
        grid=(n_batch, n_blk),
        in_specs=[
            pl.BlockSpec((tq, D_HEADS * LANES), lambda b, r: (b * n_blk + r, 0)),
            full, full, ctx, ctx,
            _layer_spec(pair_all.shape[1:], layer),
        ],
        out_specs=pl.BlockSpec((tq, width), lambda b, r: (b * n_blk + r, 0)),
        compiler_params=_cparams(("parallel", "arbitrary")),
        name="latent_neighbourhood_attn",
    )(q, k, v, ck_all, cv_all, pair_all)


def _na_pair_tiles(rpb_all):
    n_l, n_h, n_y, n_x = rpb_all.shape
    n_tiles = 2 * NA_ROWS + 1
    zero = jnp.zeros((n_l, n_h, 1, n_x), F32)
    rows = jnp.concatenate([zero, rpb_all, zero, zero], axis=2)
    feats = jnp.concatenate([rows[:, :, :-1], rows[:, :, 1:]], axis=-1)
    feat = np.arange(2 * n_x)[:, None, None]
    qc = np.arange(GRID_W)[None, :, None]
    lane = np.arange(2 * GRID_W)[None, None, :]
    kc = lane % GRID_W
    cs = np.clip(qc - NA_COLS // 2, 0, GRID_W - NA_COLS)
    inside = (kc >= cs) & (kc < cs + NA_COLS)
    select = (lane // GRID_W == feat // n_x) & (feat % n_x == kc - qc + NA_COLS - 1) & inside
    iy = np.arange(n_tiles)[:, None, None] - 1 + lane // GRID_W
    valid = inside & (iy >= 0) & (iy < n_y)
    tiles = jnp.einsum('lhef,fqx->lheqx', feats, jnp.asarray(select, F32), precision=lax.Precision.HIGHEST)
    return jnp.where(valid, tiles, NEG)


def _mixer_ffn_kernel(*refs, seq, halo, widths):
    per = 3 if halo else 1
    n_in = per * (1 + len(widths))
    x_refs = refs[:per]
    part_refs = [refs[per * (1 + k):per * (2 + k)] for k in range(len(widths))]
    (mod_ref, gmix_ref, wmix_ref, gpre_ref, gpost_ref, wi_ref, wc_ref, wo_ref, o_ref, y_ref, h_ref) = refs[n_in:]
    tm = x_refs[per // 2].shape[0]
    pad = HALO if halo else 0
    ext = tm + 2 * pad
    pieces = [(pad, tm)] if not halo else [(0, pad), (pad, tm), (pad + tm, pad)]
    keep = [None]
    if halo:
        pos0 = (pl.program_id(0) * tm) % seq
        keep = [jnp.where(pos0 > 0, 1.0, 0.0), None, jnp.where(pos0 + tm < seq, 1.0, 0.0)]

    off = 0
    for p_refs, wd in zip(part_refs, widths):
        for (r0, n), p_ref in zip(pieces, p_refs):
            y_ref[r0:r0 + n, off:off + wd] = p_ref[...]
        off += wd
    t = _dot(y_ref[...], wmix_ref[...])
    scale, shift = mod_ref[4:5, :], mod_ref[3:4, :]
    for (r0, n), x_ref, kp in zip(pieces, x_refs, keep):
        x1 = x_ref[...] + mod_ref[2:3, :] * _rmsnorm(t[r0:r0 + n], gmix_ref[...])
        h = _norm_mod(x1, gpre_ref[...], scale, shift)
        if kp is None:
            o_ref[...] = x1
        else:
            h = h * kp
        h_ref[r0:r0 + n, :] = h.astype(BF16)

    def conv_up(c0, width):
        u = _dot(h_ref[...], wi_ref[:, c0:c0 + width])
        up = pltpu.roll(u, 1, 0)[pad:pad + tm]
        un = pltpu.roll(u, ext - 1, 0)[pad:pad + tm]
        if not halo:
            sub = lax.broadcasted_iota(jnp.int32, (8, width), 0)
            up = jnp.concatenate([jnp.where(sub == 0, 0.0, up[:8]), up[8:]], axis=0)
            un = jnp.concatenate([un[:-8], jnp.where(sub == 7, 0.0, un[-8:])], axis=0)
        return (u[pad:pad + tm] * wc_ref[1:2, c0:c0 + width] + up * wc_ref[0:1, c0:c0 + width]
                + un * wc_ref[2:3, c0:c0 + width])

    y = None
    for c0, width in FF_CHUNKS:
        gate = conv_up(c0, width)
        val = conv_up(D_FF + c0, width)
        act = gate * (1.0 / (1.0 + jnp.exp(-gate))) * val
        contrib = _dot(act.astype(BF16), wo_ref[c0:c0 + width, :])
        y = contrib if y is None else y + contrib
    o_ref[...] += mod_ref[5:6, :] * _rmsnorm(y, gpost_ref[...])


def _mixer_ffn(x, parts, mods, layer, g_mix_all, w_mix_all, g_pre_all, g_post_all, wi_all, wc_all, wo_all,
               seq, seq_blocks):
    rows = x.shape[0]
    tm = min(TM_FFN, seq)
    assert seq % tm == 0
    halo = tm < seq
    widths = tuple(p.shape[1] for p in parts)
    assert sum(widths) == D_MODEL
    resident = dict(pipeline_mode=pl.Buffered(1))
    zeros3 = lambda i: (layer, 0, 0)
    row_specs = lambda wd: (_halo_specs(tm, wd, rows) if halo else [pl.BlockSpec((tm, wd), lambda i: (i, 0))])
    in_specs, args = [], []
    for arr in (x,) + tuple(parts):
        specs = row_specs(arr.shape[1])
        in_specs += specs
        args += [arr] * len(specs)
    in_specs += [
        _mod_spec(layer, seq_blocks),
        _layer_spec((1, D_MODEL), layer),
        _layer_spec((D_MODEL, D_MODEL), layer),
        _layer_spec((1, D_MODEL), layer),
        _layer_spec((1, D_MODEL), layer),
        pl.BlockSpec((None, D_MODEL, 2 * D_FF), zeros3, **resident),
        _layer_spec((3, 2 * D_FF), layer),
        pl.BlockSpec((None, D_FF, D_MODEL), zeros3, **resident),
    ]
    args += [mods, g_mix_all, w_mix_all, g_pre_all, g_post_all, wi_all, wc_all, wo_all]
    ext = tm + (2 * HALO if halo else 0)
    return pl.pallas_call(
        functools.partial(_mixer_ffn_kernel, seq=seq, halo=halo, widths=widths),
        out_shape=jax.ShapeDtypeStruct((rows, D_MODEL), F32),
        grid=(rows // tm,),
        in_specs=in_specs,
        out_specs=pl.BlockSpec((tm, D_MODEL), lambda i: (i, 0)),
        scratch_shapes=[pltpu.VMEM((ext, D_MODEL), BF16), pltpu.VMEM((ext, D_MODEL), BF16)],
        compiler_params=_cparams(("parallel",)),
        name="mixer_ffn",
    )(*args)


def _rope_tables(n_tokens):
    t = jnp.arange(n_tokens)
    row = (t // GRID_W).astype(F32)
    col = (t % GRID_W).astype(F32)
    axis_dim = HEAD_DIM // 2
    inv_freq = ROPE_THETA ** (-jnp.arange(0, axis_dim, 2, dtype=F32) / axis_dim)
    ang = jnp.concatenate([row[:, None] * inv_freq, col[:, None] * inv_freq], axis=-1)
    cos = jnp.repeat(jnp.cos(ang), 2, axis=-1)
    sin = jnp.sin(ang)
    sin = jnp.stack([-sin, sin], axis=-1).reshape(n_tokens, HEAD_DIM)
    return jnp.tile(cos, (1, 2)), jnp.tile(sin, (1, 2))


def _rows3(p):
    return p.reshape(p.shape[0], 1, p.shape[1])


def kernel(x_prompt, x_sample, cache_b_k, cache_b_v, cache_c_k, cache_c_v, cache_d_k, cache_d_v, c, c_ctx,
           w_mod, b_mod, g_mix_pre, g_mix_post, g_ffn_pre, g_ffn_post, w_in_even, w_pool, pool_scale,
           g_q_b, g_k_b, w_in_odd, sink_c, rpb_d, w_mix_out, w_ffn_in, w_ffn_conv, w_ffn_out):
    n_ctx, ctx_seq, _ = x_prompt.shape
    n_lat, lat_seq, _ = x_sample.shape
    past = cache_b_k.shape[2]
    xp = x_prompt.reshape(n_ctx * ctx_seq, D_MODEL)
    xs = x_sample.reshape(n_lat * lat_seq, D_MODEL)
    lat_blocks = lat_seq // TM_PROJ

    cvecs = jnp.concatenate([c_ctx[None, :], c, jnp.zeros((8 - 1 - n_lat, D_MODEL), F32)], axis=0)
    mods = _modulation(cvecs, w_mod, b_mod).reshape(DEPTH, 8, 6, D_MODEL)

    rope_tabs = _rope_tables(lat_seq)
    blk = jnp.arange(LANES) // HEAD_DIM
    e_mat = jnp.where(blk[:, None] == blk[None, :], 1.0 / HEAD_DIM, 0.0).astype(BF16)

    w_even = w_in_even.astype(BF16)
    w_odd = w_in_odd.astype(BF16)
    w_out = w_mix_out.astype(BF16)
    w_fi = w_ffn_in.astype(BF16)
    w_fo = w_ffn_out.astype(BF16)
    slabs = lambda t: t.astype(BF16).reshape(t.shape[0], t.shape[1], past, t.shape[3] * HEAD_DIM)
    cbk, cbv, cck, ccv, cdk, cdv = map(slabs, (cache_b_k, cache_b_v, cache_c_k, cache_c_v, cache_d_k, cache_d_v))
    g_pre, g_post = _rows3(g_mix_pre), _rows3(g_mix_post)
    gf_pre, gf_post = _rows3(g_ffn_pre), _rows3(g_ffn_post)
    pscale = _rows3(pool_scale)
    na_pairs = _na_pair_tiles(rpb_d)

    b_k, b_v, c_k, c_v, d_k, d_v = [], [], [], [], [], []
    for l in range(DEPTH):
        i = l // 2
        if l % 2 == 0:
            gq = jnp.tile(g_q_b[i], 2)[None, :]
            gk = jnp.tile(g_k_b[i], 2)[None, :]
            kvw = B_KV * HEAD_DIM
            widths = (POOL_WIDTH, B_HEADS * LANES, kvw, kvw)
            a_p, q_p, k_p, v_p = _proj_in(xp, mods, l, g_pre, w_even, i, (gq, gk, e_mat), widths,
                                          (F32, BF16, F32, F32),
                                          functools.partial(_proj_even_kernel, q_scale=QK_SCALE), None, None,
                                          "proj_even_ctx")
            a_s, q_s, k_s, v_s = _proj_in(xs, mods, l, g_pre, w_even, i, (gq, gk, e_mat), widths,
                                          (F32, BF16, BF16, BF16),
                                          functools.partial(_proj_even_kernel, q_scale=QK_SCALE * LOG2E),
                                          lat_blocks, rope_tabs,
                                          "proj_even_lat")
            b_k.append(k_p.reshape(n_ctx, ctx_seq, B_KV, HEAD_DIM))
            b_v.append(v_p.reshape(n_ctx, ctx_seq, B_KV, HEAD_DIM))
            w_bd = jax.scipy.linalg.block_diag(*[w_pool[i, g] for g in range(len(POOL_WINDOWS))]).astype(BF16)
            ya_p = _pool_mixer(a_p, w_bd, pscale, i, ctx_seq)
            ya_s = _pool_mixer(a_s, w_bd, pscale, i, lat_seq)
            yb_p = _ctx_even_attn(q_p, k_p, v_p, ctx_seq)
            yb_s = _flash_attn(q_s, k_s, v_s, cbk, cbv, i, n_lat)
            parts_p, parts_s = (ya_p, yb_p), (ya_s, yb_s)
        else:
            ckw, dw = C_KV * HEAD_DIM, D_HEADS * HEAD_DIM
            widths = (C_HEADS * LANES, ckw, ckw, D_HEADS * LANES, dw, dw)
            qc_p, kc_p, vc_p, qd_p, kd_p, vd_p = _proj_in(
                xp, mods, l, g_pre, w_odd, i, (), widths, (BF16, F32, F32, BF16, F32, F32),
                _proj_odd_kernel, None, None, "proj_odd_ctx")
            qc_s, kc_s, vc_s, qd_s, kd_s, vd_s = _proj_in(
                xs, mods, l, g_pre, w_odd, i, (), widths, (BF16,) * 6,
                _proj_odd_kernel, lat_blocks, rope_tabs, "proj_odd_lat")
            c_k.append(kc_p.reshape(n_ctx, ctx_seq, C_KV, HEAD_DIM))
            c_v.append(vc_p.reshape(n_ctx, ctx_seq, C_KV, HEAD_DIM))
            d_k.append(kd_p.reshape(n_ctx, ctx_seq, D_HEADS, HEAD_DIM))
            d_v.append(vd_p.reshape(n_ctx, ctx_seq, D_HEADS, HEAD_DIM))
            y_p = _ctx_odd_attn(sink_c, i, qc_p, kc_p, vc_p, qd_p, kd_p, vd_p, ctx_seq)
            yc_s = _window_attn(sink_c, qc_s, kc_s, vc_s, cck, ccv, i, n_lat)
            yd_s = _na_attn(qd_s, kd_s, vd_s, cdk, cdv, na_pairs, i, n_lat)
            parts_p, parts_s = (y_p,), (yc_s, yd_s)

        xp = _mixer_ffn(xp, parts_p, mods, l, g_post, w_out, gf_pre, gf_post, w_fi, w_ffn_conv, w_fo,
                        ctx_seq, None)
        xs = _mixer_ffn(xs, parts_s, mods, l, g_post, w_out, gf_pre, gf_post, w_fi, w_ffn_conv, w_fo,
                        lat_seq, lat_seq // TM_FFN)

    return (xp.reshape(n_ctx, ctx_seq, D_MODEL), xs.reshape(n_lat, lat_seq, D_MODEL),
            jnp.stack(b_k, axis=1), jnp.stack(b_v, axis=1),
            jnp.stack(c_k, axis=1), jnp.stack(c_v, axis=1),
            jnp.stack(d_k, axis=1), jnp.stack(d_v, axis=1))
```

```python
import functools

import jax
import jax.numpy as jnp
import numpy as np
from jax import lax
from jax.experimental import pallas as pl
from jax.experimental.pallas import tpu as pltpu

F32 = jnp.float32
BF16 = jnp.bfloat16

D_MODEL = 1024
DEPTH = 4
GRID_W = 64
HEAD_DIM = 64
EPS = 1e-6
ROPE_THETA = 10000.0
POOL_WINDOWS = (2, 4, 8, 16)
POOL_WIDTH = 256
B_HEADS, B_KV = 12, 4
C_HEADS, C_KV = 8, 2
D_HEADS = 8
C_WINDOW = 128
NA_ROWS, NA_COLS = 8, 16
D_FF = 2816
QK_SCALE = HEAD_DIM ** -0.5
LOG2E = 1.4426950408889634
NEG = -1e30

LANES = 128
HALO = 16
VMEM_LIMIT = 56 * 1024 * 1024

TM_PROJ = 1024
TM_FFN = 512
TM_POOL = 256
MXU_TILE = 256
FF_SPLIT = 2
_FF_BOUNDS = [round(i * (D_FF // MXU_TILE) / FF_SPLIT) * MXU_TILE for i in range(FF_SPLIT + 1)]
FF_CHUNKS = tuple((a, b - a) for a, b in zip(_FF_BOUNDS[:-1], _FF_BOUNDS[1:]))
assert D_FF % MXU_TILE == 0 and sum(w for _, w in FF_CHUNKS) == D_FF
TQ_FLASH = 256
TK_FLASH = 512
TQ_WIN = 256
CTX_SEQS = 2
NA_QROWS = 4
NA_WIN = NA_ROWS + NA_QROWS


def _cparams(sem):
    return pltpu.CompilerParams(dimension_semantics=sem, vmem_limit_bytes=VMEM_LIMIT)


def _norm_mod(x, g, scale, shift):
    ms = jnp.mean(x * x, axis=-1, keepdims=True)
    return (x * lax.rsqrt(ms + EPS) * g) * (1.0 + scale) + shift


def _rmsnorm(x, g):
    ms = jnp.mean(x * x, axis=-1, keepdims=True)
    return x * lax.rsqrt(ms + EPS) * g


def _dot(a, b):
    return jnp.dot(a, b, preferred_element_type=F32)


def _dot_nt(a, b):
    return lax.dot_general(a, b, (((1,), (1,)), ((), ())), preferred_element_type=F32)


def _low_half(rows):
    return lax.broadcasted_iota(jnp.int32, (rows, LANES), 1) < HEAD_DIM


def _layer_spec(shape, layer):
    zeros = (0,) * len(shape)
    return pl.BlockSpec((None,) + tuple(shape), lambda *_: (layer,) + zeros)


def _mod_spec(layer, seq_blocks):
    if seq_blocks is None:
        return pl.BlockSpec((None, None, 6, D_MODEL), lambda i: (layer, 0, 0, 0))
    return pl.BlockSpec((None, None, 6, D_MODEL), lambda i: (layer, 1 + i // seq_blocks, 0, 0))


def _mod_kernel(c_ref, w_ref, b_ref, o_ref):
    cv = c_ref[...]
    s = cv * (1.0 / (1.0 + jnp.exp(-cv)))
    o_ref[...] = _dot(s.astype(BF16), w_ref[...].astype(BF16)) + b_ref[...]


def _modulation(cvecs, w_mod, b_mod):
    tn = 1536
    return pl.pallas_call(
        _mod_kernel,
        out_shape=jax.ShapeDtypeStruct((DEPTH, 8, 6 * D_MODEL), F32),
        grid=(DEPTH, 6 * D_MODEL // tn),
        in_specs=[
            pl.BlockSpec((8, D_MODEL), lambda l, j: (0, 0)),
            pl.BlockSpec((None, D_MODEL, tn), lambda l, j: (l, 0, j)),
            pl.BlockSpec((None, 1, tn), lambda l, j: (l, 0, j)),
        ],
        out_specs=pl.BlockSpec((None, 8, tn), lambda l, j: (l, 0, j)),
        compiler_params=_cparams(("parallel", "parallel")),
        name="modulation",
    )(cvecs, w_mod, b_mod.reshape(DEPTH, 1, 6 * D_MODEL))


def _pair_swap(s):
    lane = lax.broadcasted_iota(jnp.int32, s.shape, 1)
    n = s.shape[1]
    return jnp.where(lane % 2 == 0, pltpu.roll(s, n - 1, 1), pltpu.roll(s, 1, 1))


def _store_expanded_q(q_ref, s, slab, kv_half_of, low):
    sr = pltpu.roll(s, HEAD_DIM, 1)
    for half in (0, 1):
        hd = 2 * slab + half
        dst = kv_half_of(hd)
        val = s if dst == half else sr
        keep = low if dst == 0 else jnp.logical_not(low)
        q_ref[:, hd * LANES:(hd + 1) * LANES] = jnp.where(keep, val, 0.0).astype(BF16)


def _proj_columns(hb, w_ref):
    full = _dot(hb, w_ref[...])
    return lambda c0, width: full[:, c0:c0 + width]


def _proj_even_kernel(*refs, rope, q_scale):
    if rope:
        (x_ref, mod_ref, g_ref, w_ref, gq_ref, gk_ref, e_ref, cos_ref, sin_ref,
         a_ref, q_ref, k_ref, v_ref) = refs
    else:
        (x_ref, mod_ref, g_ref, w_ref, gq_ref, gk_ref, e_ref,
         a_ref, q_ref, k_ref, v_ref) = refs
    rows = x_ref.shape[0]
    h = _norm_mod(x_ref[...], g_ref[...], mod_ref[1:2, :], mod_ref[0:1, :])
    u = _proj_columns(h.astype(BF16), w_ref)
    low = _low_half(rows)
    a_ref[...] = u(0, POOL_WIDTH)

    def headnorm(s, g):
        ms = _dot((s * s).astype(BF16), e_ref[...])
        return s * lax.rsqrt(ms + EPS) * g

    def rot(s):
        if not rope:
            return s
        return s * cos_ref[...] + _pair_swap(s) * sin_ref[...]

    q0 = POOL_WIDTH
    for j in range(B_HEADS // 2):
        s = rot(headnorm(u(q0 + LANES * j, LANES), gq_ref[...])) * q_scale
        _store_expanded_q(q_ref, s, j, lambda hd: (hd // (B_HEADS // B_KV)) % 2, low)
    k0 = q0 + B_HEADS * HEAD_DIM
    for j in range(B_KV // 2):
        s = headnorm(u(k0 + LANES * j, LANES), gk_ref[...])
        k_ref[:, LANES * j:LANES * (j + 1)] = rot(s).astype(k_ref.dtype)
    v0 = k0 + B_KV * HEAD_DIM
    v_ref[...] = u(v0, B_KV * HEAD_DIM).astype(v_ref.dtype)


def _proj_odd_kernel(*refs, rope):
    if rope:
        (x_ref, mod_ref, g_ref, w_ref, cos_ref, sin_ref,
         qc_ref, kc_ref, vc_ref, qd_ref, kd_ref, vd_ref) = refs
    else:
        (x_ref, mod_ref, g_ref, w_ref,
         qc_ref, kc_ref, vc_ref, qd_ref, kd_ref, vd_ref) = refs
    rows = x_ref.shape[0]
    h = _norm_mod(x_ref[...], g_ref[...], mod_ref[1:2, :], mod_ref[0:1, :])
    u = _proj_columns(h.astype(BF16), w_ref)
    low = _low_half(rows)

    def rot(s):
        if not rope:
            return s
        return s * cos_ref[...] + _pair_swap(s) * sin_ref[...]

    for j in range(C_HEADS // 2):
        s = rot(u(LANES * j, LANES)) * QK_SCALE
        _store_expanded_q(qc_ref, s, j, lambda hd: hd // (C_HEADS // C_KV), low)
    o = C_HEADS * HEAD_DIM
    kc_ref[...] = rot(u(o, LANES)).astype(kc_ref.dtype)
    o += C_KV * HEAD_DIM
    vc_ref[...] = u(o, LANES).astype(vc_ref.dtype)
    o += C_KV * HEAD_DIM
    for j in range(D_HEADS // 2):
        s = u(o + LANES * j, LANES) * QK_SCALE
        _store_expanded_q(qd_ref, s, j, lambda hd: hd % 2, low)
    o += D_HEADS * HEAD_DIM
    kd_ref[...] = u(o, D_HEADS * HEAD_DIM).astype(kd_ref.dtype)
    o += D_HEADS * HEAD_DIM
    vd_ref[...] = u(o, D_HEADS * HEAD_DIM).astype(vd_ref.dtype)


def _proj_in(x, mods, layer, g_all, w_all, w_layer, extras, out_widths, out_dtypes, kernel, seq_blocks,
             rope_tabs, name):
    rows = x.shape[0]
    tm = TM_PROJ
    const = lambda i: (0, 0)
    in_specs = [
        pl.BlockSpec((tm, D_MODEL), lambda i: (i, 0)),
        _mod_spec(layer, seq_blocks),
        _layer_spec((1, D_MODEL), layer),
        _layer_spec(w_all.shape[1:], w_layer),
    ]
    args = [x, mods, g_all, w_all]
    for e in extras:
        in_specs.append(pl.BlockSpec(e.shape, const))
        args.append(e)
    if rope_tabs is not None:
        for t in rope_tabs:
            in_specs.append(pl.BlockSpec((tm, LANES), lambda i: (i % seq_blocks, 0)))
            args.append(t)
    return pl.pallas_call(
        functools.partial(kernel, rope=rope_tabs is not None),
        out_shape=[jax.ShapeDtypeStruct((rows, wd), dt) for wd, dt in zip(out_widths, out_dtypes)],
        grid=(rows // tm,),
        in_specs=in_specs,
        out_specs=[pl.BlockSpec((tm, wd), lambda i: (i, 0)) for wd in out_widths],
        compiler_params=_cparams(("parallel",)),
        name=name,
    )(*args)


def _place_heads(out_ref, col0, heads, kv_half_of, low):
    for m in range(len(heads) // 2):
        a = heads[2 * m]
        if kv_half_of(2 * m) == 1:
            a = pltpu.roll(a, HEAD_DIM, 1)
        b = heads[2 * m + 1]
        if kv_half_of(2 * m + 1) == 0:
            b = pltpu.roll(b, HEAD_DIM, 1)
        out_ref[:, col0 + m * LANES:col0 + (m + 1) * LANES] = jnp.where(low, a, b).astype(out_ref.dtype)


def _stack_heads(q_ref, heads):
    return jnp.concatenate([q_ref[:, h * LANES:(h + 1) * LANES] for h in heads], axis=0)


def _softmax_pv(scores, values, sink=None):
    s = scores[0] if len(scores) == 1 else jnp.concatenate(scores, axis=1)
    v = values[0] if len(values) == 1 else jnp.concatenate(values, axis=0)
    m = jnp.broadcast_to(s.max(axis=-1, keepdims=True), (s.shape[0], LANES))
    if sink is not None:
        m = jnp.maximum(m, sink)
    e = jnp.exp(s - jnp.concatenate([m] * (s.shape[1] // LANES), axis=1))
    acc = _dot(e.astype(BF16), jnp.concatenate([v, jnp.ones_like(v)], axis=1))
    den = acc[:, LANES:]
    if sink is not None:
        den = den + jnp.exp(sink - m)
    return acc[:, :LANES] / den


def _sink_column(sink_ref, heads, rows_per_head):
    row = lax.broadcasted_iota(jnp.int32, (len(heads) * rows_per_head, LANES), 0)
    col = jnp.full(row.shape, sink_ref[heads[-1]], F32)
    for g in range(len(heads) - 2, -1, -1):
        col = jnp.where(row < (g + 1) * rows_per_head, sink_ref[heads[g]], col)
    return col


def _ctx_even_attn_seq(q_ref, k_ref, v_ref, y_ref):
    rows = q_ref.shape[0]
    low = _low_half(rows)
    grp = B_HEADS // B_KV
    outs = []
    for kv in range(B_KV):
        sl = slice((kv // 2) * LANES, (kv // 2 + 1) * LANES)
        ks = k_ref[:, sl].astype(BF16)
        vs = v_ref[:, sl].astype(BF16)
        heads = list(range(kv * grp, (kv + 1) * grp))
        o = _softmax_pv([_dot_nt(_stack_heads(q_ref, heads), ks)], [vs])
        outs += [o[g * rows:(g + 1) * rows] for g in range(grp)]
    _place_heads(y_ref, 0, outs, lambda hd: (hd // grp) % 2, low)


def _ctx_odd_attn_seq(sink_ref, qc_ref, kc_ref, vc_ref, qd_ref, kd_ref, vd_ref, y_ref):
    rows = qc_ref.shape[0]
    low = _low_half(rows)
    grp = C_HEADS // C_KV
    ks = kc_ref[...].astype(BF16)
    vs = vc_ref[...].astype(BF16)
    outs = []
    for kv in range(C_KV):
        heads = list(range(kv * grp, (kv + 1) * grp))
        sink = _sink_column(sink_ref, heads, rows)
        o = _softmax_pv([_dot_nt(_stack_heads(qc_ref, heads), ks)], [vs], sink)
        outs += [o[g * rows:(g + 1) * rows] for g in range(grp)]
    _place_heads(y_ref, 0, outs, lambda hd: hd // grp, low)
    outs = []
    for hd in range(D_HEADS):
        sl = slice((hd // 2) * LANES, (hd // 2 + 1) * LANES)
        s = _dot_nt(qd_ref[:, hd * LANES:(hd + 1) * LANES], kd_ref[:, sl].astype(BF16))
        outs.append(_softmax_pv([s], [vd_ref[:, sl].astype(BF16)]))
    _place_heads(y_ref, C_HEADS * HEAD_DIM, outs, lambda hd: hd % 2, low)


def _per_sequence(*refs, fn, seq, n_scalar=0):
    for i in range(refs[n_scalar].shape[0] // seq):
        rows = slice(i * seq, (i + 1) * seq)
        fn(*refs[:n_scalar], *[r.at[rows] for r in refs[n_scalar:]])


def _ctx_even_attn(q, k, v, seq):
    rows = q.shape[0]
    blk = lambda wd: pl.BlockSpec((CTX_SEQS * seq, wd), lambda b: (b, 0))
    return pl.pallas_call(
        functools.partial(_per_sequence, fn=_ctx_even_attn_seq, seq=seq),
        out_shape=jax.ShapeDtypeStruct((rows, B_HEADS * HEAD_DIM), BF16),
        grid=(rows // (CTX_SEQS * seq),),
        in_specs=[blk(q.shape[1]), blk(k.shape[1]), blk(v.shape[1])],
        out_specs=blk(B_HEADS * HEAD_DIM),
        compiler_params=_cparams(("parallel",)),
        name="ctx_even_attn",
    )(q, k, v)


def _ctx_odd_attn(sink_all, layer, qc, kc, vc, qd, kd, vd, seq):
    rows = qc.shape[0]
    blk = lambda wd: pl.BlockSpec((CTX_SEQS * seq, wd), lambda b: (b, 0))
    return pl.pallas_call(
        functools.partial(_per_sequence, fn=_ctx_odd_attn_seq, seq=seq, n_scalar=1),
        out_shape=jax.ShapeDtypeStruct((rows, D_MODEL), BF16),
        grid=(rows // (CTX_SEQS * seq),),
        in_specs=[pl.BlockSpec(memory_space=pltpu.SMEM)] + [blk(a.shape[1]) for a in (qc, kc, vc, qd, kd, vd)],
        out_specs=blk(D_MODEL),
        compiler_params=_cparams(("parallel",)),
        name="ctx_odd_attn",
    )(sink_all[layer], qc, kc, vc, qd, kd, vd)


def _pool_kernel(ap_ref, a_ref, an_ref, w_ref, ps_ref, y_ref, *, seq):
    tm = a_ref.shape[0]
    i = pl.program_id(0)
    pos0 = (i * tm) % seq
    a = a_ref[...]
    a_ext = jnp.concatenate([ap_ref[...], a, an_ref[...]], axis=0).astype(BF16)
    ext = tm + 2 * HALO
    r = lax.broadcasted_iota(jnp.int32, (tm, ext), 0)
    c = lax.broadcasted_iota(jnp.int32, (tm, ext), 1)
    pos_c = pos0 - HALO + c
    in_seq = (pos_c >= 0) & (pos_c < seq)
    d = c - HALO - r
    pos_r = pos0 + lax.broadcasted_iota(jnp.int32, (tm, POOL_WIDTH), 0)
    grp = lax.broadcasted_iota(jnp.int32, (tm, POOL_WIDTH), 1) // (POOL_WIDTH // len(POOL_WINDOWS))
    mean = jnp.zeros((tm, POOL_WIDTH), F32)
    for gi, wdw in enumerate(POOL_WINDOWS):
        band = jnp.where(in_seq & (d >= -(wdw // 2)) & (d < wdw - wdw // 2), 1.0, 0.0).astype(BF16)
        lo = jnp.maximum(pos_r - wdw // 2, 0)
        hi = jnp.minimum(pos_r - wdw // 2 + wdw, seq)
        win_mean = _dot(band, a_ext) / (hi - lo).astype(F32)
        mean = jnp.where(grp == gi, win_mean, mean)
    pooled = (mean - a).astype(BF16)
    y_ref[...] = (_dot(pooled, w_ref[...]) * ps_ref[...]).astype(y_ref.dtype)


def _halo_specs(tm, width, n_rows):
    per = tm // HALO
    last = n_rows // HALO - 1
    prev = pl.BlockSpec((HALO, width), lambda i: (jnp.maximum(i * per - 1, 0), 0))
    cur = pl.BlockSpec((tm, width), lambda i: (i, 0))
    nxt = pl.BlockSpec((HALO, width), lambda i: (jnp.minimum((i + 1) * per, last), 0))
    return [prev, cur, nxt]


def _pool_mixer(a, w_bd, pscale_all, layer, seq):
    rows = a.shape[0]
    tm = TM_POOL
    assert seq % tm == 0
    const = lambda i: (0, 0)
    return pl.pallas_call(
        functools.partial(_pool_kernel, seq=seq),
        out_shape=jax.ShapeDtypeStruct((rows, POOL_WIDTH), BF16),
        grid=(rows // tm,),
        in_specs=_halo_specs(tm, POOL_WIDTH, rows) + [
            pl.BlockSpec((POOL_WIDTH, POOL_WIDTH), const),
            _layer_spec((1, POOL_WIDTH), layer),
        ],
        out_specs=pl.BlockSpec((tm, POOL_WIDTH), lambda i: (i, 0)),
        compiler_params=_cparams(("parallel",)),
        name="pool_mixer",
    )(a, a, a, w_bd, pscale_all)


def _flash_kernel(q_ref, k_ref, v_ref, ck_ref, cv_ref, y_ref, m_ref, acc_ref):
    tq = q_ref.shape[0]
    low = _low_half(tq)
    grp = B_HEADS // B_KV
    n_chunks = k_ref.shape[0] // TK_FLASH
    qs = [_stack_heads(q_ref, list(range(j * grp, (j + 1) * grp))) for j in range(2)]
    def step(kc, vc, first=False):
        lowk = _low_half(vc.shape[0])
        for j in range(2):
            own = lowk if j == 0 else jnp.logical_not(lowk)
            vj = jnp.where(own, vc, jnp.ones_like(vc))
            s = _dot_nt(qs[j], kc)
            m_new = s.max(axis=-1, keepdims=True)
            m_new = jnp.broadcast_to(m_new, (s.shape[0], LANES)) if first else jnp.maximum(m_ref[j], m_new)
            p = jnp.exp2((s - jnp.concatenate([m_new] * (s.shape[1] // LANES), axis=1)).astype(BF16))
            pv = _dot(p, vj)
            acc_ref[j] = pv if first else jnp.exp2(m_ref[j] - m_new) * acc_ref[j] + pv
            m_ref[j] = m_new

    def body(c, carry):
        off = pl.multiple_of(c * TK_FLASH, TK_FLASH)
        step(k_ref[pl.ds(off, TK_FLASH), :], v_ref[pl.ds(off, TK_FLASH), :])
        return carry

    step(ck_ref[...], cv_ref[...], first=True)
    lax.fori_loop(0, n_chunks, body, 0, unroll=4)
    outs = []
    for j in range(2):
        acc = acc_ref[j]
        o = acc / pltpu.roll(acc, HEAD_DIM, 1)
        outs += [o[g * tq:(g + 1) * tq] for g in range(grp)]
    _place_heads(y_ref, 0, outs, lambda hd: hd // grp, low)


def _flash_attn(q, k, v, ck_all, cv_all, layer, n_batch):
    rows = q.shape[0]
    seq = rows // n_batch
    past = ck_all.shape[2]
    tq = TQ_FLASH
    nq = seq // tq
    grp = B_HEADS // B_KV
    qw = 2 * grp * LANES
    yw = 2 * grp * HEAD_DIM
    ctx = pl.BlockSpec((None, None, past, LANES), lambda b, p, i: (b, layer, 0, p))
    return pl.pallas_call(
        _flash_kernel,
        out_shape=jax.ShapeDtypeStruct((rows, B_HEADS * HEAD_DIM), BF16),
        grid=(n_batch, B_KV // 2, nq),
        in_specs=[
            pl.BlockSpec((tq, qw), lambda b, p, i: (b * nq + i, p)),
            pl.BlockSpec((seq, LANES), lambda b, p, i: (b, p)),
            pl.BlockSpec((seq, LANES), lambda b, p, i: (b, p)),
            ctx, ctx,
        ],
        out_specs=pl.BlockSpec((tq, yw), lambda b, p, i: (b * nq + i, p)),
        scratch_shapes=[
            pltpu.VMEM((2, grp * tq, LANES), F32),
            pltpu.VMEM((2, grp * tq, LANES), F32),
        ],
        compiler_params=_cparams(("parallel", "parallel", "arbitrary")),
        name="latent_flash_attn",
    )(q, k, v, ck_all, cv_all)


def _window_kernel(sink_ref, q_ref, kp_ref, kc_ref, kn_ref, vp_ref, vc_ref, vn_ref, ck_ref, cv_ref, y_ref,
                   *, n_blocks):
    tq = q_ref.shape[0]
    n = pl.program_id(1)
    low = _low_half(tq)
    grp = C_HEADS // C_KV
    n_loc = tq + 2 * C_WINDOW
    n_keys = n_loc + ck_ref.shape[0]
    r = lax.broadcasted_iota(jnp.int32, (tq, n_keys), 0)
    c = lax.broadcasted_iota(jnp.int32, (tq, n_keys), 1)
    ok = ((jnp.abs(r - (c - C_WINDOW)) <= C_WINDOW)
          & ((c >= C_WINDOW) | (n > 0)) & ((c < C_WINDOW + tq) | (n < n_blocks - 1)))
    mask = jnp.where(ok | (c >= n_loc), 0.0, NEG)
    mask = jnp.concatenate([mask] * grp, axis=0)
    k_all = jnp.concatenate([kp_ref[...], kc_ref[...], kn_ref[...], ck_ref[...]], axis=0)
    v_all = jnp.concatenate([vp_ref[...], vc_ref[...], vn_ref[...], cv_ref[...]], axis=0)
    outs = []
    for kv in range(C_KV):
        heads = list(range(kv * grp, (kv + 1) * grp))
        q = _stack_heads(q_ref, heads)
        sink = _sink_column(sink_ref, heads, tq)
        o = _softmax_pv([_dot_nt(q, k_all) + mask], [v_all], sink)
        outs += [o[g * tq:(g + 1) * tq] for g in range(grp)]
    _place_heads(y_ref, 0, outs, lambda hd: hd // grp, low)


def _window_attn(sink_all, q, k, v, ck_all, cv_all, layer, n_batch):
    rows = q.shape[0]
    seq = rows // n_batch
    past = ck_all.shape[2]
    tq = TQ_WIN
    assert tq % C_WINDOW == 0 and seq % tq == 0
    nb = seq // tq
    per = tq // C_WINDOW
    nw = seq // C_WINDOW
    prev = pl.BlockSpec((C_WINDOW, LANES), lambda b, n: (b * nw + jnp.maximum(n * per - 1, 0), 0))
    cur = pl.BlockSpec((tq, LANES), lambda b, n: (b * nb + n, 0))
    nxt = pl.BlockSpec((C_WINDOW, LANES), lambda b, n: (b * nw + jnp.minimum((n + 1) * per, nw - 1), 0))
    ctx = pl.BlockSpec((None, None, past, LANES), lambda b, n: (b, layer, 0, 0))
    return pl.pallas_call(
        functools.partial(_window_kernel, n_blocks=nb),
        out_shape=jax.ShapeDtypeStruct((rows, C_HEADS * HEAD_DIM), BF16),
        grid=(n_batch, nb),
        in_specs=[
            pl.BlockSpec(memory_space=pltpu.SMEM),
            pl.BlockSpec((tq, C_HEADS * LANES), lambda b, n: (b * nb + n, 0)),
            prev, cur, nxt, prev, cur, nxt, ctx, ctx,
        ],
        out_specs=pl.BlockSpec((tq, C_HEADS * HEAD_DIM), lambda b, n: (b * nb + n, 0)),
        compiler_params=_cparams(("parallel", "parallel")),
        name="latent_window_attn",
    )(sink_all[layer], q, k, k, k, v, v, v, ck_all, cv_all)


def _na_window_start(blk, n_rows):
    return jnp.clip(blk * NA_QROWS - NA_ROWS // 2, 0, n_rows - NA_WIN)


def _na_kernel(q_ref, k_ref, v_ref, ck_ref, cv_ref, pair_ref, y_ref, *, n_rows):
    blk = pl.program_id(1)
    tq = NA_QROWS * GRID_W
    win = NA_WIN * GRID_W
    low = _low_half(tq)
    ws = _na_window_start(blk, n_rows)
    start = pl.multiple_of(ws * GRID_W, GRID_W)
    q_row = blk * NA_QROWS + lax.broadcasted_iota(jnp.int32, (tq, win), 0) // GRID_W
    k_row = ws + lax.broadcasted_iota(jnp.int32, (tq, win), 1) // GRID_W
    rs = jnp.clip(q_row - NA_ROWS // 2, 0, n_rows - NA_ROWS)
    row_mask = jnp.where((k_row >= rs) & (k_row < rs + NA_ROWS), 0.0, NEG)
    outs = []
    for hd in range(D_HEADS):
        sl = slice((hd // 2) * LANES, (hd // 2 + 1) * LANES)
        q = q_ref[:, hd * LANES:(hd + 1) * LANES]
        kw = k_ref[pl.ds(start, win), sl]
        vw = v_ref[pl.ds(start, win), sl]
        bias = jnp.concatenate([
            jnp.concatenate([
                pair_ref[hd, jnp.clip(ws + 2 * jp - (blk * NA_QROWS + a) + NA_ROWS, 0, 2 * NA_ROWS)]
                for jp in range(NA_WIN // 2)], axis=1)
            for a in range(NA_QROWS)], axis=0)
        s_loc = _dot_nt(q, kw) + bias + row_mask
        s_ctx = _dot_nt(q, ck_ref[:, sl])
        outs.append(_softmax_pv([s_loc, s_ctx], [vw, cv_ref[:, sl]]))
    _place_heads(y_ref, 0, outs, lambda hd: hd % 2, low)


def _na_attn(q, k, v, ck_all, cv_all, pair_all, layer, n_batch):
    rows = q.shape[0]
    seq = rows // n_batch
    past = ck_all.shape[2]
    n_rows = seq // GRID_W
    n_blk = n_rows // NA_QROWS
    assert n_rows >= NA_WIN and NA_WIN % 2 == 0
    width = D_HEADS * HEAD_DIM
    tq = NA_QROWS * GRID_W
    full = pl.BlockSpec((seq, width), lambda b, r: (b, 0))
    ctx = pl.BlockSpec((None, None, past, width), lambda b, r: (b, layer, 0, 0))
    return pl.pallas_call(
        functools.partial(_na_kernel, n_rows=n_rows),
        out_shape=jax.ShapeDtypeStruct((rows, width), BF16),
        grid=(n_batch, n_blk),
        in_specs=[
            pl.BlockSpec((tq, D_HEADS * LANES), lambda b, r: (b * n_blk + r, 0)),
            full, full, ctx, ctx,
            _layer_spec(pair_all.shape[1:], layer),
        ],
        out_specs=pl.BlockSpec((tq, width), lambda b, r: (b * n_blk + r, 0)),
        compiler_params=_cparams(("parallel", "arbitrary")),
        name="latent_neighbourhood_attn",
    )(q, k, v, ck_all, cv_all, pair_all)


def _na_pair_tiles(rpb_all):
    n_l, n_h, n_y, n_x = rpb_all.shape
    n_tiles = 2 * NA_ROWS + 1
    zero = jnp.zeros((n_l, n_h, 1, n_x), F32)
    rows = jnp.concatenate([zero, rpb_all, zero, zero], axis=2)
    feats = jnp.concatenate([rows[:, :, :-1], rows[:, :, 1:]], axis=-1)
    feat = np.arange(2 * n_x)[:, None, None]
    qc = np.arange(GRID_W)[None, :, None]
    lane = np.arange(2 * GRID_W)[None, None, :]
    kc = lane % GRID_W
    cs = np.clip(qc - NA_COLS // 2, 0, GRID_W - NA_COLS)
    inside = (kc >= cs) & (kc < cs + NA_COLS)
    select = (lane // GRID_W == feat // n_x) & (feat % n_x == kc - qc + NA_COLS - 1) & inside
    iy = np.arange(n_tiles)[:, None, None] - 1 + lane // GRID_W
    valid = inside & (iy >= 0) & (iy < n_y)
    tiles = jnp.einsum('lhef,fqx->lheqx', feats, jnp.asarray(select, F32), precision=lax.Precision.HIGHEST)
    return jnp.where(valid, tiles, NEG)


def _mixer_ffn_kernel(*refs, seq, halo, widths):
    per = 3 if halo else 1
    n_in = per * (1 + len(widths))
    x_refs = refs[:per]
    part_refs = [refs[per * (1 + k):per * (2 + k)] for k in range(len(widths))]
    (mod_ref, gmix_ref, wmix_ref, gpre_ref, gpost_ref, wi_ref, wc_ref, wo_ref, o_ref, y_ref, h_ref) = refs[n_in:]
    tm = x_refs[per // 2].shape[0]
    pad = HALO if halo else 0
    ext = tm + 2 * pad
    pieces = [(pad, tm)] if not halo else [(0, pad), (pad, tm), (pad + tm, pad)]
    keep = [None]
    if halo:
        pos0 = (pl.program_id(0) * tm) % seq
        keep = [jnp.where(pos0 > 0, 1.0, 0.0), None, jnp.where(pos0 + tm < seq, 1.0, 0.0)]

    off = 0
    for p_refs, wd in zip(part_refs, widths):
        for (r0, n), p_ref in zip(pieces, p_refs):
            y_ref[r0:r0 + n, off:off + wd] = p_ref[...]
        off += wd
    t = _dot(y_ref[...], wmix_ref[...])
    scale, shift = mod_ref[4:5, :], mod_ref[3:4, :]
    for (r0, n), x_ref, kp in zip(pieces, x_refs, keep):
        x1 = x_ref[...] + mod_ref[2:3, :] * _rmsnorm(t[r0:r0 + n], gmix_ref[...])
        h = _norm_mod(x1, gpre_ref[...], scale, shift)
        if kp is None:
            o_ref[...] = x1
        else:
            h = h * kp
        h_ref[r0:r0 + n, :] = h.astype(BF16)

    def conv_up(c0, width):
        u = _dot(h_ref[...], wi_ref[:, c0:c0 + width])
        up = pltpu.roll(u, 1, 0)[pad:pad + tm]
        un = pltpu.roll(u, ext - 1, 0)[pad:pad + tm]
        if not halo:
            sub = lax.broadcasted_iota(jnp.int32, (8, width), 0)
            ups, uns = [], []
            for s0 in range(0, tm, seq):
                ups += [jnp.where(sub == 0, 0.0, up[s0:s0 + 8]), up[s0 + 8:s0 + seq]]
                uns += [un[s0:s0 + seq - 8], jnp.where(sub == 7, 0.0, un[s0 + seq - 8:s0 + seq])]
            up = jnp.concatenate(ups, axis=0)
            un = jnp.concatenate(uns, axis=0)
        return (u[pad:pad + tm] * wc_ref[1:2, c0:c0 + width] + up * wc_ref[0:1, c0:c0 + width]
                + un * wc_ref[2:3, c0:c0 + width])

    y = None
    for c0, width in FF_CHUNKS:
        gate = conv_up(c0, width)
        val = conv_up(D_FF + c0, width)
        act = gate * (1.0 / (1.0 + jnp.exp(-gate))) * val
        contrib = _dot(act.astype(BF16), wo_ref[c0:c0 + width, :])
        y = contrib if y is None else y + contrib
    o_ref[...] += mod_ref[5:6, :] * _rmsnorm(y, gpost_ref[...])


def _mixer_ffn(x, parts, mods, layer, g_mix_all, w_mix_all, g_pre_all, g_post_all, wi_all, wc_all, wo_all,
               seq, seq_blocks):
    rows = x.shape[0]
    tm = TM_FFN
    assert seq % tm == 0 or tm % seq == 0
    halo = tm < seq
    widths = tuple(p.shape[1] for p in parts)
    assert sum(widths) == D_MODEL
    resident = dict(pipeline_mode=pl.Buffered(1))
    zeros3 = lambda i: (layer, 0, 0)
    row_specs = lambda wd: (_halo_specs(tm, wd, rows) if halo else [pl.BlockSpec((tm, wd), lambda i: (i, 0))])
    in_specs, args = [], []
    for arr in (x,) + tuple(parts):
        specs = row_specs(arr.shape[1])
        in_specs += specs
        args += [arr] * len(specs)
    in_specs += [
        _mod_spec(layer, seq_blocks),
        _layer_spec((1, D_MODEL), layer),
        _layer_spec((D_MODEL, D_MODEL), layer),
        _layer_spec((1, D_MODEL), layer),
        _layer_spec((1, D_MODEL), layer),
        pl.BlockSpec((None, D_MODEL, 2 * D_FF), zeros3, **resident),
        _layer_spec((3, 2 * D_FF), layer),
        pl.BlockSpec((None, D_FF, D_MODEL), zeros3, **resident),
    ]
    args += [mods, g_mix_all, w_mix_all, g_pre_all, g_post_all, wi_all, wc_all, wo_all]
    ext = tm + (2 * HALO if halo else 0)
    return pl.pallas_call(
        functools.partial(_mixer_ffn_kernel, seq=seq, halo=halo, widths=widths),
        out_shape=jax.ShapeDtypeStruct((rows, D_MODEL), F32),
        grid=(rows // tm,),
        in_specs=in_specs,
        out_specs=pl.BlockSpec((tm, D_MODEL), lambda i: (i, 0)),
        scratch_shapes=[pltpu.VMEM((ext, D_MODEL), BF16), pltpu.VMEM((ext, D_MODEL), BF16)],
        compiler_params=_cparams(("parallel",)),
        name="mixer_ffn",
    )(*args)


def _rope_tables(n_tokens):
    t = jnp.arange(n_tokens)
    row = (t // GRID_W).astype(F32)
    col = (t % GRID_W).astype(F32)
    axis_dim = HEAD_DIM // 2
    inv_freq = ROPE_THETA ** (-jnp.arange(0, axis_dim, 2, dtype=F32) / axis_dim)
    ang = jnp.concatenate([row[:, None] * inv_freq, col[:, None] * inv_freq], axis=-1)
    cos = jnp.repeat(jnp.cos(ang), 2, axis=-1)
    sin = jnp.sin(ang)
    sin = jnp.stack([-sin, sin], axis=-1).reshape(n_tokens, HEAD_DIM)
    return jnp.tile(cos, (1, 2)), jnp.tile(sin, (1, 2))


def _rows3(p):
    return p.reshape(p.shape[0], 1, p.shape[1])


def kernel(x_prompt, x_sample, cache_b_k, cache_b_v, cache_c_k, cache_c_v, cache_d_k, cache_d_v, c, c_ctx,
           w_mod, b_mod, g_mix_pre, g_mix_post, g_ffn_pre, g_ffn_post, w_in_even, w_pool, pool_scale,
           g_q_b, g_k_b, w_in_odd, sink_c, rpb_d, w_mix_out, w_ffn_in, w_ffn_conv, w_ffn_out):
    n_ctx, ctx_seq, _ = x_prompt.shape
    n_lat, lat_seq, _ = x_sample.shape
    past = cache_b_k.shape[2]
    xp = x_prompt.reshape(n_ctx * ctx_seq, D_MODEL)
    xs = x_sample.reshape(n_lat * lat_seq, D_MODEL)
    lat_blocks = lat_seq // TM_PROJ

    cvecs = jnp.concatenate([c_ctx[None, :], c, jnp.zeros((8 - 1 - n_lat, D_MODEL), F32)], axis=0)
    mods = _modulation(cvecs, w_mod, b_mod).reshape(DEPTH, 8, 6, D_MODEL)

    rope_tabs = _rope_tables(lat_seq)
    blk = jnp.arange(LANES) // HEAD_DIM
    e_mat = jnp.where(blk[:, None] == blk[None, :], 1.0 / HEAD_DIM, 0.0).astype(BF16)

    w_even = w_in_even.astype(BF16)
    w_odd = w_in_odd.astype(BF16)
    w_out = w_mix_out.astype(BF16)
    w_fi = w_ffn_in.astype(BF16)
    w_fo = w_ffn_out.astype(BF16)
    slabs = lambda t: t.astype(BF16).reshape(t.shape[0], t.shape[1], past, t.shape[3] * HEAD_DIM)
    cbk, cbv, cck, ccv, cdk, cdv = map(slabs, (cache_b_k, cache_b_v, cache_c_k, cache_c_v, cache_d_k, cache_d_v))
    g_pre, g_post = _rows3(g_mix_pre), _rows3(g_mix_post)
    gf_pre, gf_post = _rows3(g_ffn_pre), _rows3(g_ffn_post)
    pscale = _rows3(pool_scale)
    na_pairs = _na_pair_tiles(rpb_d)

    b_k, b_v, c_k, c_v, d_k, d_v = [], [], [], [], [], []
    for l in range(DEPTH):
        i = l // 2
        if l % 2 == 0:
            gq = jnp.tile(g_q_b[i], 2)[None, :]
            gk = jnp.tile(g_k_b[i], 2)[None, :]
            kvw = B_KV * HEAD_DIM
            widths = (POOL_WIDTH, B_HEADS * LANES, kvw, kvw)
            a_p, q_p, k_p, v_p = _proj_in(xp, mods, l, g_pre, w_even, i, (gq, gk, e_mat), widths,
                                          (F32, BF16, F32, F32),
                                          functools.partial(_proj_even_kernel, q_scale=QK_SCALE), None, None,
                                          "proj_even_ctx")
            a_s, q_s, k_s, v_s = _proj_in(xs, mods, l, g_pre, w_even, i, (gq, gk, e_mat), widths,
                                          (F32, BF16, BF16, BF16),
                                          functools.partial(_proj_even_kernel, q_scale=QK_SCALE * LOG2E),
                                          lat_blocks, rope_tabs,
                                          "proj_even_lat")
            b_k.append(k_p.reshape(n_ctx, ctx_seq, B_KV, HEAD_DIM))
            b_v.append(v_p.reshape(n_ctx, ctx_seq, B_KV, HEAD_DIM))
            w_bd = jax.scipy.linalg.block_diag(*[w_pool[i, g] for g in range(len(POOL_WINDOWS))]).astype(BF16)
            ya_p = _pool_mixer(a_p, w_bd, pscale, i, ctx_seq)
            ya_s = _pool_mixer(a_s, w_bd, pscale, i, lat_seq)
            yb_p = _ctx_even_attn(q_p, k_p, v_p, ctx_seq)
            yb_s = _flash_attn(q_s, k_s, v_s, cbk, cbv, i, n_lat)
            parts_p, parts_s = (ya_p, yb_p), (ya_s, yb_s)
        else:
            ckw, dw = C_KV * HEAD_DIM, D_HEADS * HEAD_DIM
            widths = (C_HEADS * LANES, ckw, ckw, D_HEADS * LANES, dw, dw)
            qc_p, kc_p, vc_p, qd_p, kd_p, vd_p = _proj_in(
                xp, mods, l, g_pre, w_odd, i, (), widths, (BF16, F32, F32, BF16, F32, F32),
                _proj_odd_kernel, None, None, "proj_odd_ctx")
            qc_s, kc_s, vc_s, qd_s, kd_s, vd_s = _proj_in(
                xs, mods, l, g_pre, w_odd, i, (), widths, (BF16,) * 6,
                _proj_odd_kernel, lat_blocks, rope_tabs, "proj_odd_lat")
            c_k.append(kc_p.reshape(n_ctx, ctx_seq, C_KV, HEAD_DIM))
            c_v.append(vc_p.reshape(n_ctx, ctx_seq, C_KV, HEAD_DIM))
            d_k.append(kd_p.reshape(n_ctx, ctx_seq, D_HEADS, HEAD_DIM))
            d_v.append(vd_p.reshape(n_ctx, ctx_seq, D_HEADS, HEAD_DIM))
            y_p = _ctx_odd_attn(sink_c, i, qc_p, kc_p, vc_p, qd_p, kd_p, vd_p, ctx_seq)
            yc_s = _window_attn(sink_c, qc_s, kc_s, vc_s, cck, ccv, i, n_lat)
            yd_s = _na_attn(qd_s, kd_s, vd_s, cdk, cdv, na_pairs, i, n_lat)
            parts_p, parts_s = (y_p,), (yc_s, yd_s)

        xp = _mixer_ffn(xp, parts_p, mods, l, g_post, w_out, gf_pre, gf_post, w_fi, w_ffn_conv, w_fo,
                        ctx_seq, None)
        xs = _mixer_ffn(xs, parts_s, mods, l, g_post, w_out, gf_pre, gf_post, w_fi, w_ffn_conv, w_fo,
                        lat_seq, lat_seq // TM_FFN)

    return (xp.reshape(n_ctx, ctx_seq, D_MODEL), xs.reshape(n_lat, lat_seq, D_MODEL),
            jnp.stack(b_k, axis=1), jnp.stack(b_v, axis=1),
            jnp.stack(c_k, axis=1), jnp.stack(c_v, axis=1),
            jnp.stack(d_k, axis=1), jnp.stack(d_v, axis=1))
```

```python
import functools

import jax
import jax.numpy as jnp
import numpy as np
from jax import lax
from jax.experimental import pallas as pl
from jax.experimental.pallas import tpu as pltpu

F32 = jnp.float32
BF16 = jnp.bfloat16

D_MODEL = 1024
DEPTH = 4
GRID_W = 64
HEAD_DIM = 64
EPS = 1e-6
ROPE_THETA = 10000.0
POOL_WINDOWS = (2, 4, 8, 16)
POOL_WIDTH = 256
B_HEADS, B_KV = 12, 4
C_HEADS, C_KV = 8, 2
D_HEADS = 8
C_WINDOW = 128
NA_ROWS, NA_COLS = 8, 16
D_FF = 2816
QK_SCALE = HEAD_DIM ** -0.5
LOG2E = 1.4426950408889634
NEG = -1e30

LANES = 128
HALO = 16
VMEM_LIMIT = 56 * 1024 * 1024

TM_PROJ = 1024
TM_FFN = 512
TM_POOL = 256
MXU_TILE = 256
FF_SPLIT = 2
_FF_BOUNDS = [round(i * (D_FF // MXU_TILE) / FF_SPLIT) * MXU_TILE for i in range(FF_SPLIT + 1)]
FF_CHUNKS = tuple((a, b - a) for a, b in zip(_FF_BOUNDS[:-1], _FF_BOUNDS[1:]))
assert D_FF % MXU_TILE == 0 and sum(w for _, w in FF_CHUNKS) == D_FF
TQ_FLASH = 256
TK_FLASH = 512
TQ_WIN = 256
CTX_SEQS = 2
NA_QROWS = 4
NA_WIN = NA_ROWS + NA_QROWS


def _cparams(sem):
    return pltpu.CompilerParams(dimension_semantics=sem, vmem_limit_bytes=VMEM_LIMIT)


def _norm_mod(x, g, scale, shift):
    ms = jnp.mean(x * x, axis=-1, keepdims=True)
    return (x * lax.rsqrt(ms + EPS) * g) * (1.0 + scale) + shift


def _rmsnorm(x, g):
    ms = jnp.mean(x * x, axis=-1, keepdims=True)
    return x * lax.rsqrt(ms + EPS) * g


def _dot(a, b):
    return jnp.dot(a, b, preferred_element_type=F32)


def _dot_nt(a, b):
    return lax.dot_general(a, b, (((1,), (1,)), ((), ())), preferred_element_type=F32)


def _low_half(rows):
    return lax.broadcasted_iota(jnp.int32, (rows, LANES), 1) < HEAD_DIM


def _layer_spec(shape, layer):
    zeros = (0,) * len(shape)
    return pl.BlockSpec((None,) + tuple(shape), lambda *_: (layer,) + zeros)


def _mod_spec(layer, seq_blocks):
    if seq_blocks is None:
        return pl.BlockSpec((None, None, 6, D_MODEL), lambda i: (layer, 0, 0, 0))
    return pl.BlockSpec((None, None, 6, D_MODEL), lambda i: (layer, 1 + i // seq_blocks, 0, 0))


def _mod_kernel(c_ref, w_ref, b_ref, o_ref):
    cv = c_ref[...]
    s = cv * (1.0 / (1.0 + jnp.exp(-cv)))
    o_ref[...] = _dot(s.astype(BF16), w_ref[...].astype(BF16)) + b_ref[...]


def _modulation(cvecs, w_mod, b_mod):
    tn = 1536
    return pl.pallas_call(
        _mod_kernel,
        out_shape=jax.ShapeDtypeStruct((DEPTH, 8, 6 * D_MODEL), F32),
        grid=(DEPTH, 6 * D_MODEL // tn),
        in_specs=[
            pl.BlockSpec((8, D_MODEL), lambda l, j: (0, 0)),
            pl.BlockSpec((None, D_MODEL, tn), lambda l, j: (l, 0, j)),
            pl.BlockSpec((None, 1, tn), lambda l, j: (l, 0, j)),
        ],
        out_specs=pl.BlockSpec((None, 8, tn), lambda l, j: (l, 0, j)),
        compiler_params=_cparams(("parallel", "parallel")),
        name="modulation",
    )(cvecs, w_mod, b_mod.reshape(DEPTH, 1, 6 * D_MODEL))


def _pair_swap(s):
    lane = lax.broadcasted_iota(jnp.int32, s.shape, 1)
    n = s.shape[1]
    return jnp.where(lane % 2 == 0, pltpu.roll(s, n - 1, 1), pltpu.roll(s, 1, 1))


def _store_expanded_q(q_ref, s, slab, kv_half_of, low):
    sr = pltpu.roll(s, HEAD_DIM, 1)
    for half in (0, 1):
        hd = 2 * slab + half
        dst = kv_half_of(hd)
        val = s if dst == half else sr
        keep = low if dst == 0 else jnp.logical_not(low)
        q_ref[:, hd * LANES:(hd + 1) * LANES] = jnp.where(keep, val, 0.0).astype(BF16)


def _proj_columns(hb, w_ref):
    full = _dot(hb, w_ref[...])
    return lambda c0, width: full[:, c0:c0 + width]


def _proj_even_kernel(*refs, rope, q_scale):
    if rope:
        (x_ref, mod_ref, g_ref, w_ref, gq_ref, gk_ref, e_ref, cos_ref, sin_ref,
         a_ref, q_ref, k_ref, v_ref) = refs
    else:
        (x_ref, mod_ref, g_ref, w_ref, gq_ref, gk_ref, e_ref,
         a_ref, q_ref, k_ref, v_ref) = refs
    rows = x_ref.shape[0]
    h = _norm_mod(x_ref[...], g_ref[...], mod_ref[1:2, :], mod_ref[0:1, :])
    u = _proj_columns(h.astype(BF16), w_ref)
    low = _low_half(rows)
    a_ref[...] = u(0, POOL_WIDTH)

    def headnorm(s, g):
        ms = _dot((s * s).astype(BF16), e_ref[...])
        return s * lax.rsqrt(ms + EPS) * g

    def rot(s):
        if not rope:
            return s
        return s * cos_ref[...] + _pair_swap(s) * sin_ref[...]

    q0 = POOL_WIDTH
    for j in range(B_HEADS // 2):
        s = rot(headnorm(u(q0 + LANES * j, LANES), gq_ref[...])) * q_scale
        _store_expanded_q(q_ref, s, j, lambda hd: (hd // (B_HEADS // B_KV)) % 2, low)
    k0 = q0 + B_HEADS * HEAD_DIM
    for j in range(B_KV // 2):
        s = headnorm(u(k0 + LANES * j, LANES), gk_ref[...])
        k_ref[:, LANES * j:LANES * (j + 1)] = rot(s).astype(k_ref.dtype)
    v0 = k0 + B_KV * HEAD_DIM
    v_ref[...] = u(v0, B_KV * HEAD_DIM).astype(v_ref.dtype)


def _proj_odd_kernel(*refs, rope):
    if rope:
        (x_ref, mod_ref, g_ref, w_ref, cos_ref, sin_ref,
         qc_ref, kc_ref, vc_ref, qd_ref, kd_ref, vd_ref) = refs
    else:
        (x_ref, mod_ref, g_ref, w_ref,
         qc_ref, kc_ref, vc_ref, qd_ref, kd_ref, vd_ref) = refs
    rows = x_ref.shape[0]
    h = _norm_mod(x_ref[...], g_ref[...], mod_ref[1:2, :], mod_ref[0:1, :])
    u = _proj_columns(h.astype(BF16), w_ref)
    low = _low_half(rows)

    def rot(s):
        if not rope:
            return s
        return s * cos_ref[...] + _pair_swap(s) * sin_ref[...]

    for j in range(C_HEADS // 2):
        s = rot(u(LANES * j, LANES)) * QK_SCALE
        _store_expanded_q(qc_ref, s, j, lambda hd: hd // (C_HEADS // C_KV), low)
    o = C_HEADS * HEAD_DIM
    kc_ref[...] = rot(u(o, LANES)).astype(kc_ref.dtype)
    o += C_KV * HEAD_DIM
    vc_ref[...] = u(o, LANES).astype(vc_ref.dtype)
    o += C_KV * HEAD_DIM
    for j in range(D_HEADS // 2):
        s = u(o + LANES * j, LANES) * QK_SCALE
        _store_expanded_q(qd_ref, s, j, lambda hd: hd % 2, low)
    o += D_HEADS * HEAD_DIM
    kd_ref[...] = u(o, D_HEADS * HEAD_DIM).astype(kd_ref.dtype)
    o += D_HEADS * HEAD_DIM
    vd_ref[...] = u(o, D_HEADS * HEAD_DIM).astype(vd_ref.dtype)


def _proj_in(x, mods, layer, g_all, w_all, w_layer, extras, out_widths, out_dtypes, kernel, seq_blocks,
             rope_tabs, name):
    rows = x.shape[0]
    tm = TM_PROJ
    const = lambda i: (0, 0)
    in_specs = [
        pl.BlockSpec((tm, D_MODEL), lambda i: (i, 0)),
        _mod_spec(layer, seq_blocks),
        _layer_spec((1, D_MODEL), layer),
        _layer_spec(w_all.shape[1:], w_layer),
    ]
    args = [x, mods, g_all, w_all]
    for e in extras:
        in_specs.append(pl.BlockSpec(e.shape, const))
        args.append(e)
    if rope_tabs is not None:
        for t in rope_tabs:
            in_specs.append(pl.BlockSpec((tm, LANES), lambda i: (i % seq_blocks, 0)))
            args.append(t)
    return pl.pallas_call(
        functools.partial(kernel, rope=rope_tabs is not None),
        out_shape=[jax.ShapeDtypeStruct((rows, wd), dt) for wd, dt in zip(out_widths, out_dtypes)],
        grid=(rows // tm,),
        in_specs=in_specs,
        out_specs=[pl.BlockSpec((tm, wd), lambda i: (i, 0)) for wd in out_widths],
        compiler_params=_cparams(("parallel",)),
        name=name,
    )(*args)


def _place_heads(out_ref, col0, heads, kv_half_of, low):
    for m in range(len(heads) // 2):
        a = heads[2 * m]
        if kv_half_of(2 * m) == 1:
            a = pltpu.roll(a, HEAD_DIM, 1)
        b = heads[2 * m + 1]
        if kv_half_of(2 * m + 1) == 0:
            b = pltpu.roll(b, HEAD_DIM, 1)
        out_ref[:, col0 + m * LANES:col0 + (m + 1) * LANES] = jnp.where(low, a, b).astype(out_ref.dtype)


def _stack_heads(q_ref, heads):
    return jnp.concatenate([q_ref[:, h * LANES:(h + 1) * LANES] for h in heads], axis=0)


def _softmax_pv(scores, values, sink=None):
    s = scores[0] if len(scores) == 1 else jnp.concatenate(scores, axis=1)
    v = values[0] if len(values) == 1 else jnp.concatenate(values, axis=0)
    m = jnp.broadcast_to(s.max(axis=-1, keepdims=True), (s.shape[0], LANES))
    if sink is not None:
        m = jnp.maximum(m, sink)
    e = jnp.exp(s - jnp.concatenate([m] * (s.shape[1] // LANES), axis=1))
    acc = _dot(e.astype(BF16), jnp.concatenate([v, jnp.ones_like(v)], axis=1))
    den = acc[:, LANES:]
    if sink is not None:
        den = den + jnp.exp(sink - m)
    return acc[:, :LANES] / den


def _sink_column(sink_ref, heads, rows_per_head):
    row = lax.broadcasted_iota(jnp.int32, (len(heads) * rows_per_head, LANES), 0)
    col = jnp.full(row.shape, sink_ref[heads[-1]], F32)
    for g in range(len(heads) - 2, -1, -1):
        col = jnp.where(row < (g + 1) * rows_per_head, sink_ref[heads[g]], col)
    return col


def _ctx_even_attn_seq(q_ref, k_ref, v_ref, y_ref):
    rows = q_ref.shape[0]
    low = _low_half(rows)
    grp = B_HEADS // B_KV
    outs = []
    for kv in range(B_KV):
        sl = slice((kv // 2) * LANES, (kv // 2 + 1) * LANES)
        ks = k_ref[:, sl].astype(BF16)
        vs = v_ref[:, sl].astype(BF16)
        heads = list(range(kv * grp, (kv + 1) * grp))
        o = _softmax_pv([_dot_nt(_stack_heads(q_ref, heads), ks)], [vs])
        outs += [o[g * rows:(g + 1) * rows] for g in range(grp)]
    _place_heads(y_ref, 0, outs, lambda hd: (hd // grp) % 2, low)


def _ctx_odd_attn_seq(sink_ref, qc_ref, kc_ref, vc_ref, qd_ref, kd_ref, vd_ref, y_ref):
    rows = qc_ref.shape[0]
    low = _low_half(rows)
    grp = C_HEADS // C_KV
    ks = kc_ref[...].astype(BF16)
    vs = vc_ref[...].astype(BF16)
    outs = []
    for kv in range(C_KV):
        heads = list(range(kv * grp, (kv + 1) * grp))
        sink = _sink_column(sink_ref, heads, rows)
        o = _softmax_pv([_dot_nt(_stack_heads(qc_ref, heads), ks)], [vs], sink)
        outs += [o[g * rows:(g + 1) * rows] for g in range(grp)]
    _place_heads(y_ref, 0, outs, lambda hd: hd // grp, low)
    outs = []
    for hd in range(D_HEADS):
        sl = slice((hd // 2) * LANES, (hd // 2 + 1) * LANES)
        s = _dot_nt(qd_ref[:, hd * LANES:(hd + 1) * LANES], kd_ref[:, sl].astype(BF16))
        outs.append(_softmax_pv([s], [vd_ref[:, sl].astype(BF16)]))
    _place_heads(y_ref, C_HEADS * HEAD_DIM, outs, lambda hd: hd % 2, low)


def _per_sequence(*refs, fn, seq, n_scalar=0):
    for i in range(refs[n_scalar].shape[0] // seq):
        rows = slice(i * seq, (i + 1) * seq)
        fn(*refs[:n_scalar], *[r.at[rows] for r in refs[n_scalar:]])


def _ctx_even_attn(q, k, v, seq):
    rows = q.shape[0]
    blk = lambda wd: pl.BlockSpec((CTX_SEQS * seq, wd), lambda b: (b, 0))
    return pl.pallas_call(
        functools.partial(_per_sequence, fn=_ctx_even_attn_seq, seq=seq),
        out_shape=jax.ShapeDtypeStruct((rows, B_HEADS * HEAD_DIM), BF16),
        grid=(rows // (CTX_SEQS * seq),),
        in_specs=[blk(q.shape[1]), blk(k.shape[1]), blk(v.shape[1])],
        out_specs=blk(B_HEADS * HEAD_DIM),
        compiler_params=_cparams(("parallel",)),
        name="ctx_even_attn",
    )(q, k, v)


def _ctx_odd_attn(sink_all, layer, qc, kc, vc, qd, kd, vd, seq):
    rows = qc.shape[0]
    blk = lambda wd: pl.BlockSpec((CTX_SEQS * seq, wd), lambda b: (b, 0))
    return pl.pallas_call(
        functools.partial(_per_sequence, fn=_ctx_odd_attn_seq, seq=seq, n_scalar=1),
        out_shape=jax.ShapeDtypeStruct((rows, D_MODEL), BF16),
        grid=(rows // (CTX_SEQS * seq),),
        in_specs=[pl.BlockSpec(memory_space=pltpu.SMEM)] + [blk(a.shape[1]) for a in (qc, kc, vc, qd, kd, vd)],
        out_specs=blk(D_MODEL),
        compiler_params=_cparams(("parallel",)),
        name="ctx_odd_attn",
    )(sink_all[layer], qc, kc, vc, qd, kd, vd)


def _pool_kernel(ap_ref, a_ref, an_ref, w_ref, ps_ref, y_ref, *, seq):
    tm = a_ref.shape[0]
    i = pl.program_id(0)
    pos0 = (i * tm) % seq
    a = a_ref[...]
    a_ext = jnp.concatenate([ap_ref[...], a, an_ref[...]], axis=0).astype(BF16)
    ext = tm + 2 * HALO
    r = lax.broadcasted_iota(jnp.int32, (tm, ext), 0)
    c = lax.broadcasted_iota(jnp.int32, (tm, ext), 1)
    pos_c = pos0 - HALO + c
    in_seq = (pos_c >= 0) & (pos_c < seq)
    d = c - HALO - r
    pos_r = pos0 + lax.broadcasted_iota(jnp.int32, (tm, POOL_WIDTH), 0)
    grp = lax.broadcasted_iota(jnp.int32, (tm, POOL_WIDTH), 1) // (POOL_WIDTH // len(POOL_WINDOWS))
    mean = jnp.zeros((tm, POOL_WIDTH), F32)
    for gi, wdw in enumerate(POOL_WINDOWS):
        band = jnp.where(in_seq & (d >= -(wdw // 2)) & (d < wdw - wdw // 2), 1.0, 0.0).astype(BF16)
        lo = jnp.maximum(pos_r - wdw // 2, 0)
        hi = jnp.minimum(pos_r - wdw // 2 + wdw, seq)
        win_mean = _dot(band, a_ext) / (hi - lo).astype(F32)
        mean = jnp.where(grp == gi, win_mean, mean)
    pooled = (mean - a).astype(BF16)
    y_ref[...] = (_dot(pooled, w_ref[...]) * ps_ref[...]).astype(y_ref.dtype)


def _halo_specs(tm, width, n_rows):
    per = tm // HALO
    last = n_rows // HALO - 1
    prev = pl.BlockSpec((HALO, width), lambda i: (jnp.maximum(i * per - 1, 0), 0))
    cur = pl.BlockSpec((tm, width), lambda i: (i, 0))
    nxt = pl.BlockSpec((HALO, width), lambda i: (jnp.minimum((i + 1) * per, last), 0))
    return [prev, cur, nxt]


def _pool_mixer(a, w_bd, pscale_all, layer, seq):
    rows = a.shape[0]
    tm = TM_POOL
    assert seq % tm == 0
    const = lambda i: (0, 0)
    return pl.pallas_call(
        functools.partial(_pool_kernel, seq=seq),
        out_shape=jax.ShapeDtypeStruct((rows, POOL_WIDTH), BF16),
        grid=(rows // tm,),
        in_specs=_halo_specs(tm, POOL_WIDTH, rows) + [
            pl.BlockSpec((POOL_WIDTH, POOL_WIDTH), const),
            _layer_spec((1, POOL_WIDTH), layer),
        ],
        out_specs=pl.BlockSpec((tm, POOL_WIDTH), lambda i: (i, 0)),
        compiler_params=_cparams(("parallel",)),
        name="pool_mixer",
    )(a, a, a, w_bd, pscale_all)


def _flash_kernel(q_ref, k_ref, v_ref, ck_ref, cv_ref, y_ref, m_ref, acc_ref):
    tq = q_ref.shape[0]
    low = _low_half(tq)
    grp = B_HEADS // B_KV
    n_chunks = k_ref.shape[0] // TK_FLASH
    qs = [_stack_heads(q_ref, list(range(j * grp, (j + 1) * grp))) for j in range(2)]
    def step(kc, vc, first=False):
        lowk = _low_half(vc.shape[0])
        for j in range(2):
            own = lowk if j == 0 else jnp.logical_not(lowk)
            vj = jnp.where(own, vc, jnp.ones_like(vc))
            s = _dot_nt(qs[j], kc)
            m_new = s.max(axis=-1, keepdims=True)
            m_new = jnp.broadcast_to(m_new, (s.shape[0], LANES)) if first else jnp.maximum(m_ref[j], m_new)
            p = jnp.exp2((s - jnp.concatenate([m_new] * (s.shape[1] // LANES), axis=1)).astype(BF16))
            pv = _dot(p, vj)
            acc_ref[j] = pv if first else jnp.exp2(m_ref[j] - m_new) * acc_ref[j] + pv
            m_ref[j] = m_new

    def body(c, carry):
        off = pl.multiple_of(c * TK_FLASH, TK_FLASH)
        step(k_ref[pl.ds(off, TK_FLASH), :], v_ref[pl.ds(off, TK_FLASH), :])
        return carry

    step(ck_ref[...], cv_ref[...], first=True)
    lax.fori_loop(0, n_chunks, body, 0, unroll=4)
    outs = []
    for j in range(2):
        acc = acc_ref[j]
        o = acc / pltpu.roll(acc, HEAD_DIM, 1)
        outs += [o[g * tq:(g + 1) * tq] for g in range(grp)]
    _place_heads(y_ref, 0, outs, lambda hd: hd // grp, low)


def _flash_attn(q, k, v, ck_all, cv_all, layer, n_batch):
    rows = q.shape[0]
    seq = rows // n_batch
    past = ck_all.shape[2]
    tq = TQ_FLASH
    nq = seq // tq
    grp = B_HEADS // B_KV
    qw = 2 * grp * LANES
    yw = 2 * grp * HEAD_DIM
    ctx = pl.BlockSpec((None, None, past, LANES), lambda b, p, i: (b, layer, 0, p))
    return pl.pallas_call(
        _flash_kernel,
        out_shape=jax.ShapeDtypeStruct((rows, B_HEADS * HEAD_DIM), BF16),
        grid=(n_batch, B_KV // 2, nq),
        in_specs=[
            pl.BlockSpec((tq, qw), lambda b, p, i: (b * nq + i, p)),
            pl.BlockSpec((seq, LANES), lambda b, p, i: (b, p)),
            pl.BlockSpec((seq, LANES), lambda b, p, i: (b, p)),
            ctx, ctx,
        ],
        out_specs=pl.BlockSpec((tq, yw), lambda b, p, i: (b * nq + i, p)),
        scratch_shapes=[
            pltpu.VMEM((2, grp * tq, LANES), F32),
            pltpu.VMEM((2, grp * tq, LANES), F32),
        ],
        compiler_params=_cparams(("parallel", "parallel", "arbitrary")),
        name="latent_flash_attn",
    )(q, k, v, ck_all, cv_all)


def _window_kernel(sink_ref, q_ref, kp_ref, kc_ref, kn_ref, vp_ref, vc_ref, vn_ref, ck_ref, cv_ref, y_ref,
                   *, n_blocks):
    tq = q_ref.shape[0]
    n = pl.program_id(1)
    low = _low_half(tq)
    grp = C_HEADS // C_KV
    n_loc = tq + 2 * C_WINDOW
    n_keys = n_loc + ck_ref.shape[0]
    r = lax.broadcasted_iota(jnp.int32, (tq, n_keys), 0)
    c = lax.broadcasted_iota(jnp.int32, (tq, n_keys), 1)
    ok = ((jnp.abs(r - (c - C_WINDOW)) <= C_WINDOW)
          & ((c >= C_WINDOW) | (n > 0)) & ((c < C_WINDOW + tq) | (n < n_blocks - 1)))
    mask = jnp.where(ok | (c >= n_loc), 0.0, NEG)
    mask = jnp.concatenate([mask] * grp, axis=0)
    k_all = jnp.concatenate([kp_ref[...], kc_ref[...], kn_ref[...], ck_ref[...]], axis=0)
    v_all = jnp.concatenate([vp_ref[...], vc_ref[...], vn_ref[...], cv_ref[...]], axis=0)
    outs = []
    for kv in range(C_KV):
        heads = list(range(kv * grp, (kv + 1) * grp))
        q = _stack_heads(q_ref, heads)
        sink = _sink_column(sink_ref, heads, tq)
        o = _softmax_pv([_dot_nt(q, k_all) + mask], [v_all], sink)
        outs += [o[g * tq:(g + 1) * tq] for g in range(grp)]
    _place_heads(y_ref, 0, outs, lambda hd: hd // grp, low)


def _window_attn(sink_all, q, k, v, ck_all, cv_all, layer, n_batch):
    rows = q.shape[0]
    seq = rows // n_batch
    past = ck_all.shape[2]
    tq = TQ_WIN
    assert tq % C_WINDOW == 0 and seq % tq == 0
    nb = seq // tq
    per = tq // C_WINDOW
    nw = seq // C_WINDOW
    prev = pl.BlockSpec((C_WINDOW, LANES), lambda b, n: (b * nw + jnp.maximum(n * per - 1, 0), 0))
    cur = pl.BlockSpec((tq, LANES), lambda b, n: (b * nb + n, 0))
    nxt = pl.BlockSpec((C_WINDOW, LANES), lambda b, n: (b * nw + jnp.minimum((n + 1) * per, nw - 1), 0))
    ctx = pl.BlockSpec((None, None, past, LANES), lambda b, n: (b, layer, 0, 0))
    return pl.pallas_call(
        functools.partial(_window_kernel, n_blocks=nb),
        out_shape=jax.ShapeDtypeStruct((rows, C_HEADS * HEAD_DIM), BF16),
        grid=(n_batch, nb),
        in_specs=[
            pl.BlockSpec(memory_space=pltpu.SMEM),
            pl.BlockSpec((tq, C_HEADS * LANES), lambda b, n: (b * nb + n, 0)),
            prev, cur, nxt, prev, cur, nxt, ctx, ctx,
        ],
        out_specs=pl.BlockSpec((tq, C_HEADS * HEAD_DIM), lambda b, n: (b * nb + n, 0)),
        compiler_params=_cparams(("parallel", "parallel")),
        name="latent_window_attn",
    )(sink_all[layer], q, k, k, k, v, v, v, ck_all, cv_all)


def _na_window_start(blk, n_rows):
    return jnp.clip(blk * NA_QROWS - NA_ROWS // 2, 0, n_rows - NA_WIN)


def _na_kernel(q_ref, k_ref, v_ref, ck_ref, cv_ref, pair_ref, y_ref, *, n_rows):
    blk = pl.program_id(1)
    tq = NA_QROWS * GRID_W
    win = NA_WIN * GRID_W
    low = _low_half(tq)
    ws = _na_window_start(blk, n_rows)
    start = pl.multiple_of(ws * GRID_W, GRID_W)
    q_row = blk * NA_QROWS + lax.broadcasted_iota(jnp.int32, (tq, win), 0) // GRID_W
    k_row = ws + lax.broadcasted_iota(jnp.int32, (tq, win), 1) // GRID_W
    rs = jnp.clip(q_row - NA_ROWS // 2, 0, n_rows - NA_ROWS)
    row_mask = jnp.where((k_row >= rs) & (k_row < rs + NA_ROWS), 0.0, NEG)
    outs = []
    for hd in range(D_HEADS):
        sl = slice((hd // 2) * LANES, (hd // 2 + 1) * LANES)
        q = q_ref[:, hd * LANES:(hd + 1) * LANES]
        kw = k_ref[pl.ds(start, win), sl]
        vw = v_ref[pl.ds(start, win), sl]
        bias = jnp.concatenate([
            jnp.concatenate([
                pair_ref[hd, jnp.clip(ws + 2 * jp - (blk * NA_QROWS + a) + NA_ROWS, 0, 2 * NA_ROWS)]
                for jp in range(NA_WIN // 2)], axis=1)
            for a in range(NA_QROWS)], axis=0)
        s_loc = _dot_nt(q, kw) + bias + row_mask
        s_ctx = _dot_nt(q, ck_ref[:, sl])
        outs.append(_softmax_pv([s_loc, s_ctx], [vw, cv_ref[:, sl]]))
    _place_heads(y_ref, 0, outs, lambda hd: hd % 2, low)


def _na_attn(q, k, v, ck_all, cv_all, pair_all, layer, n_batch):
    rows = q.shape[0]
    seq = rows // n_batch
    past = ck_all.shape[2]
    n_rows = seq // GRID_W
    n_blk = n_rows // NA_QROWS
    assert n_rows >= NA_WIN and NA_WIN % 2 == 0
    width = D_HEADS * HEAD_DIM
    tq = NA_QROWS * GRID_W
    full = pl.BlockSpec((seq, width), lambda b, r: (b, 0))
    ctx = pl.BlockSpec((None, None, past, width), lambda b, r: (b, layer, 0, 0))
    return pl.pallas_call(
        functools.partial(_na_kernel, n_rows=n_rows),
        out_shape=jax.ShapeDtypeStruct((rows, width), BF16),
        grid=(n_batch, n_blk),
        in_specs=[
            pl.BlockSpec((tq, D_HEADS * LANES), lambda b, r: (b * n_blk + r, 0)),
            full, full, ctx, ctx,
            _layer_spec(pair_all.shape[1:], layer),
        ],
        out_specs=pl.BlockSpec((tq, width), lambda b, r: (b * n_blk + r, 0)),
        compiler_params=_cparams(("parallel", "arbitrary")),
        name="latent_neighbourhood_attn",
    )(q, k, v, ck_all, cv_all, pair_all)


def _na_pair_tiles(rpb_all):
    n_l, n_h, n_y, n_x = rpb_all.shape
    n_tiles = 2 * NA_ROWS + 1
    zero = jnp.zeros((n_l, n_h, 1, n_x), F32)
    rows = jnp.concatenate([zero, rpb_all, zero, zero], axis=2)
    feats = jnp.concatenate([rows[:, :, :-1], rows[:, :, 1:]], axis=-1)
    feat = np.arange(2 * n_x)[:, None, None]
    qc = np.arange(GRID_W)[None, :, None]
    lane = np.arange(2 * GRID_W)[None, None, :]
    kc = lane % GRID_W
    cs = np.clip(qc - NA_COLS // 2, 0, GRID_W - NA_COLS)
    inside = (kc >= cs) & (kc < cs + NA_COLS)
    select = (lane // GRID_W == feat // n_x) & (feat % n_x == kc - qc + NA_COLS - 1) & inside
    iy = np.arange(n_tiles)[:, None, None] - 1 + lane // GRID_W
    valid = inside & (iy >= 0) & (iy < n_y)
    tiles = jnp.einsum('lhef,fqx->lheqx', feats, jnp.asarray(select, F32), precision=lax.Precision.HIGHEST)
    return jnp.where(valid, tiles, NEG)


def _mixer_ffn_kernel(*refs, seq, halo, widths):
    per = 3 if halo else 1
    n_in = per * (1 + len(widths))
    x_refs = refs[:per]
    part_refs = [refs[per * (1 + k):per * (2 + k)] for k in range(len(widths))]
    (mod_ref, gmix_ref, wmix_ref, gpre_ref, gpost_ref, wi_ref, wc_ref, wo_ref, o_ref, y_ref, h_ref) = refs[n_in:]
    tm = x_refs[per // 2].shape[0]
    pad = HALO if halo else 0
    ext = tm + 2 * pad
    pieces = [(pad, tm)] if not halo else [(0, pad), (pad, tm), (pad + tm, pad)]
    keep = [None]
    if halo:
        pos0 = (pl.program_id(0) * tm) % seq
        keep = [jnp.where(pos0 > 0, 1.0, 0.0), None, jnp.where(pos0 + tm < seq, 1.0, 0.0)]

    off = 0
    for p_refs, wd in zip(part_refs, widths):
        for (r0, n), p_ref in zip(pieces, p_refs):
            y_ref[r0:r0 + n, off:off + wd] = p_ref[...]
        off += wd
    t = _dot(y_ref[...], wmix_ref[...])
    scale, shift = mod_ref[4:5, :], mod_ref[3:4, :]
    for (r0, n), x_ref, kp in zip(pieces, x_refs, keep):
        x1 = x_ref[...] + mod_ref[2:3, :] * _rmsnorm(t[r0:r0 + n], gmix_ref[...])
        h = _norm_mod(x1, gpre_ref[...], scale, shift)
        if kp is None:
            o_ref[...] = x1
        else:
            h = h * kp
        h_ref[r0:r0 + n, :] = h.astype(BF16)

    def conv_up(c0, width):
        u = _dot(h_ref[...], wi_ref[:, c0:c0 + width])
        up = pltpu.roll(u, 1, 0)[pad:pad + tm]
        un = pltpu.roll(u, ext - 1, 0)[pad:pad + tm]
        if not halo:
            sub = lax.broadcasted_iota(jnp.int32, (8, width), 0)
            ups, uns = [], []
            for s0 in range(0, tm, seq):
                ups += [jnp.where(sub == 0, 0.0, up[s0:s0 + 8]), up[s0 + 8:s0 + seq]]
                uns += [un[s0:s0 + seq - 8], jnp.where(sub == 7, 0.0, un[s0 + seq - 8:s0 + seq])]
            up = jnp.concatenate(ups, axis=0)
            un = jnp.concatenate(uns, axis=0)
        return (u[pad:pad + tm] * wc_ref[1:2, c0:c0 + width] + up * wc_ref[0:1, c0:c0 + width]
                + un * wc_ref[2:3, c0:c0 + width])

    y = None
    for c0, width in FF_CHUNKS:
        gate = conv_up(c0, width)
        val = conv_up(D_FF + c0, width)
        act = gate * (1.0 / (1.0 + jnp.exp(-gate))) * val
        contrib = _dot(act.astype(BF16), wo_ref[c0:c0 + width, :])
        y = contrib if y is None else y + contrib
    o_ref[...] += mod_ref[5:6, :] * _rmsnorm(y, gpost_ref[...])


def _mixer_ffn(x, parts, mods, layer, g_mix_all, w_mix_all, g_pre_all, g_post_all, wi_all, wc_all, wo_all,
               seq, seq_blocks):
    rows = x.shape[0]
    tm = TM_FFN
    assert seq % tm == 0 or tm % seq == 0
    halo = tm < seq
    widths = tuple(p.shape[1] for p in parts)
    assert sum(widths) == D_MODEL
    resident = dict(pipeline_mode=pl.Buffered(1))
    zeros3 = lambda i: (layer, 0, 0)
    row_specs = lambda wd: (_halo_specs(tm, wd, rows) if halo else [pl.BlockSpec((tm, wd), lambda i: (i, 0))])
    in_specs, args = [], []
    for arr in (x,) + tuple(parts):
        specs = row_specs(arr.shape[1])
        in_specs += specs
        args += [arr] * len(specs)
    in_specs += [
        _mod_spec(layer, seq_blocks),
        _layer_spec((1, D_MODEL), layer),
        _layer_spec((D_MODEL, D_MODEL), layer),
        _layer_spec((1, D_MODEL), layer),
        _layer_spec((1, D_MODEL), layer),
        pl.BlockSpec((None, D_MODEL, 2 * D_FF), zeros3, **resident),
        _layer_spec((3, 2 * D_FF), layer),
        pl.BlockSpec((None, D_FF, D_MODEL), zeros3, **resident),
    ]
    args += [mods, g_mix_all, w_mix_all, g_pre_all, g_post_all, wi_all, wc_all, wo_all]
    ext = tm + (2 * HALO if halo else 0)
    return pl.pallas_call(
        functools.partial(_mixer_ffn_kernel, seq=seq, halo=halo, widths=widths),
        out_shape=jax.ShapeDtypeStruct((rows, D_MODEL), F32),
        grid=(rows // tm,),
        in_specs=in_specs,
        out_specs=pl.BlockSpec((tm, D_MODEL), lambda i: (i, 0)),
        scratch_shapes=[pltpu.VMEM((ext, D_MODEL), BF16), pltpu.VMEM((ext, D_MODEL), BF16)],
        compiler_params=_cparams(("parallel",)),
        name="mixer_ffn",
    )(*args)


def _rope_tables(n_tokens):
    t = jnp.arange(n_tokens)
    row = (t // GRID_W).astype(F32)
    col = (t % GRID_W).astype(F32)
    axis_dim = HEAD_DIM // 2
    inv_freq = ROPE_THETA ** (-jnp.arange(0, axis_dim, 2, dtype=F32) / axis_dim)
    ang = jnp.concatenate([row[:, None] * inv_freq, col[:, None] * inv_freq], axis=-1)
    cos = jnp.repeat(jnp.cos(ang), 2, axis=-1)
    sin = jnp.sin(ang)
    sin = jnp.stack([-sin, sin], axis=-1).reshape(n_tokens, HEAD_DIM)
    return jnp.tile(cos, (1, 2)), jnp.tile(sin, (1, 2))


def _rows3(p):
    return p.reshape(p.shape[0], 1, p.shape[1])


def kernel(x_prompt, x_sample, cache_b_k, cache_b_v, cache_c_k, cache_c_v, cache_d_k, cache_d_v, c, c_ctx,
           w_mod, b_mod, g_mix_pre, g_mix_post, g_ffn_pre, g_ffn_post, w_in_even, w_pool, pool_scale,
           g_q_b, g_k_b, w_in_odd, sink_c, rpb_d, w_mix_out, w_ffn_in, w_ffn_conv, w_ffn_out):
    n_ctx, ctx_seq, _ = x_prompt.shape
    n_lat, lat_seq, _ = x_sample.shape
    past = cache_b_k.shape[2]
    xp = x_prompt.reshape(n_ctx * ctx_seq, D_MODEL)
    xs = x_sample.reshape(n_lat * lat_seq, D_MODEL)
    lat_blocks = lat_seq // TM_PROJ

    cvecs = jnp.concatenate([c_ctx[None, :], c, jnp.zeros((8 - 1 - n_lat, D_MODEL), F32)], axis=0)
    mods = _modulation(cvecs, w_mod, b_mod).reshape(DEPTH, 8, 6, D_MODEL)

    rope_tabs = _rope_tables(lat_seq)
    blk = jnp.arange(LANES) // HEAD_DIM
    e_mat = jnp.where(blk[:, None] == blk[None, :], 1.0 / HEAD_DIM, 0.0).astype(BF16)

    w_even = w_in_even.astype(BF16)
    w_odd = w_in_odd.astype(BF16)
    w_out = w_mix_out.astype(BF16)
    w_fi = w_ffn_in.astype(BF16)
    w_fo = w_ffn_out.astype(BF16)
    slabs = lambda t: t.astype(BF16).reshape(t.shape[0], t.shape[1], past, t.shape[3] * HEAD_DIM)
    cbk, cbv, cck, ccv, cdk, cdv = map(slabs, (cache_b_k, cache_b_v, cache_c_k, cache_c_v, cache_d_k, cache_d_v))
    g_pre, g_post = _rows3(g_mix_pre), _rows3(g_mix_post)
    gf_pre, gf_post = _rows3(g_ffn_pre), _rows3(g_ffn_post)
    pscale = _rows3(pool_scale)
    na_pairs = _na_pair_tiles(rpb_d)

    b_k, b_v, c_k, c_v, d_k, d_v = [], [], [], [], [], []
    for l in range(DEPTH):
        i = l // 2
        if l % 2 == 0:
            gq = jnp.tile(g_q_b[i], 2)[None, :]
            gk = jnp.tile(g_k_b[i], 2)[None, :]
            kvw = B_KV * HEAD_DIM
            widths = (POOL_WIDTH, B_HEADS * LANES, kvw, kvw)
            a_p, q_p, k_p, v_p = _proj_in(xp, mods, l, g_pre, w_even, i, (gq, gk, e_mat), widths,
                                          (F32, BF16, F32, F32),
                                          functools.partial(_proj_even_kernel, q_scale=QK_SCALE), None, None,
                                          "proj_even_ctx")
            a_s, q_s, k_s, v_s = _proj_in(xs, mods, l, g_pre, w_even, i, (gq, gk, e_mat), widths,
                                          (F32, BF16, BF16, BF16),
                                          functools.partial(_proj_even_kernel, q_scale=QK_SCALE * LOG2E),
                                          lat_blocks, rope_tabs,
                                          "proj_even_lat")
            b_k.append(k_p)
            b_v.append(v_p)
            w_bd = jax.scipy.linalg.block_diag(*[w_pool[i, g] for g in range(len(POOL_WINDOWS))]).astype(BF16)
            ya_p = _pool_mixer(a_p, w_bd, pscale, i, ctx_seq)
            ya_s = _pool_mixer(a_s, w_bd, pscale, i, lat_seq)
            yb_p = _ctx_even_attn(q_p, k_p, v_p, ctx_seq)
            yb_s = _flash_attn(q_s, k_s, v_s, cbk, cbv, i, n_lat)
            parts_p, parts_s = (ya_p, yb_p), (ya_s, yb_s)
        else:
            ckw, dw = C_KV * HEAD_DIM, D_HEADS * HEAD_DIM
            widths = (C_HEADS * LANES, ckw, ckw, D_HEADS * LANES, dw, dw)
            qc_p, kc_p, vc_p, qd_p, kd_p, vd_p = _proj_in(
                xp, mods, l, g_pre, w_odd, i, (), widths, (BF16, F32, F32, BF16, F32, F32),
                _proj_odd_kernel, None, None, "proj_odd_ctx")
            qc_s, kc_s, vc_s, qd_s, kd_s, vd_s = _proj_in(
                xs, mods, l, g_pre, w_odd, i, (), widths, (BF16,) * 6,
                _proj_odd_kernel, lat_blocks, rope_tabs, "proj_odd_lat")
            c_k.append(kc_p)
            c_v.append(vc_p)
            d_k.append(kd_p)
            d_v.append(vd_p)
            y_p = _ctx_odd_attn(sink_c, i, qc_p, kc_p, vc_p, qd_p, kd_p, vd_p, ctx_seq)
            yc_s = _window_attn(sink_c, qc_s, kc_s, vc_s, cck, ccv, i, n_lat)
            yd_s = _na_attn(qd_s, kd_s, vd_s, cdk, cdv, na_pairs, i, n_lat)
            parts_p, parts_s = (y_p,), (yc_s, yd_s)

        xp = _mixer_ffn(xp, parts_p, mods, l, g_post, w_out, gf_pre, gf_post, w_fi, w_ffn_conv, w_fo,
                        ctx_seq, None)
        xs = _mixer_ffn(xs, parts_s, mods, l, g_post, w_out, gf_pre, gf_post, w_fi, w_ffn_conv, w_fo,
                        lat_seq, lat_seq // TM_FFN)

    def cache(per_layer):
        t = jnp.stack([a.reshape(n_ctx, ctx_seq, a.shape[1]) for a in per_layer], axis=1)
        return t.reshape(n_ctx, len(per_layer), ctx_seq, t.shape[-1] // HEAD_DIM, HEAD_DIM)

    return (xp.reshape(n_ctx, ctx_seq, D_MODEL), xs.reshape(n_lat, lat_seq, D_MODEL),
            cache(b_k), cache(b_v), cache(c_k), cache(c_v), cache(d_k), cache(d_v))
```

```python
import functools

import jax
import jax.numpy as jnp
import numpy as np
from jax import lax
from jax.experimental import pallas as pl
from jax.experimental.pallas import tpu as pltpu

F32 = jnp.float32
BF16 = jnp.bfloat16

D_MODEL = 1024
DEPTH = 4
GRID_W = 64
HEAD_DIM = 64
EPS = 1e-6
ROPE_THETA = 10000.0
POOL_WINDOWS = (2, 4, 8, 16)
POOL_WIDTH = 256
B_HEADS, B_KV = 12, 4
C_HEADS, C_KV = 8, 2
D_HEADS = 8
C_WINDOW = 128
NA_ROWS, NA_COLS = 8, 16
D_FF = 2816
QK_SCALE = HEAD_DIM ** -0.5
LOG2E = 1.4426950408889634
NEG = -1e30

LANES = 128
HALO = 16
VMEM_LIMIT = 56 * 1024 * 1024

TM_PROJ = 1024
TM_FFN = 512
TM_POOL = 256
MXU_TILE = 256
FF_SPLIT = 2
_FF_BOUNDS = [round(i * (D_FF // MXU_TILE) / FF_SPLIT) * MXU_TILE for i in range(FF_SPLIT + 1)]
FF_CHUNKS = tuple((a, b - a) for a, b in zip(_FF_BOUNDS[:-1], _FF_BOUNDS[1:]))
assert D_FF % MXU_TILE == 0 and sum(w for _, w in FF_CHUNKS) == D_FF
TQ_FLASH = 256
TK_FLASH = 512
TQ_WIN = 256
CTX_SEQS = 2
NA_QROWS = 4
NA_WIN = NA_ROWS + NA_QROWS


def _cparams(sem):
    return pltpu.CompilerParams(dimension_semantics=sem, vmem_limit_bytes=VMEM_LIMIT)


def _norm_mod(x, g, scale, shift):
    ms = jnp.mean(x * x, axis=-1, keepdims=True)
    return (x * lax.rsqrt(ms + EPS) * g) * (1.0 + scale) + shift


def _rmsnorm(x, g):
    ms = jnp.mean(x * x, axis=-1, keepdims=True)
    return x * lax.rsqrt(ms + EPS) * g


def _dot(a, b):
    return jnp.dot(a, b, preferred_element_type=F32)


def _dot_nt(a, b):
    return lax.dot_general(a, b, (((1,), (1,)), ((), ())), preferred_element_type=F32)


def _low_half(rows):
    return lax.broadcasted_iota(jnp.int32, (rows, LANES), 1) < HEAD_DIM


def _layer_spec(shape, layer):
    zeros = (0,) * len(shape)
    return pl.BlockSpec((None,) + tuple(shape), lambda *_: (layer,) + zeros)


def _mod_spec(layer, seq_blocks):
    if seq_blocks is None:
        return pl.BlockSpec((None, None, 6, D_MODEL), lambda i: (layer, 0, 0, 0))
    return pl.BlockSpec((None, None, 6, D_MODEL), lambda i: (layer, 1 + i // seq_blocks, 0, 0))


def _mod_kernel(c_ref, w_ref, b_ref, o_ref):
    cv = c_ref[...]
    s = cv * (1.0 / (1.0 + jnp.exp(-cv)))
    o_ref[...] = _dot(s.astype(BF16), w_ref[...].astype(BF16)) + b_ref[...]


def _modulation(cvecs, w_mod, b_mod):
    tn = 1536
    return pl.pallas_call(
        _mod_kernel,
        out_shape=jax.ShapeDtypeStruct((DEPTH, 8, 6 * D_MODEL), F32),
        grid=(DEPTH, 6 * D_MODEL // tn),
        in_specs=[
            pl.BlockSpec((8, D_MODEL), lambda l, j: (0, 0)),
            pl.BlockSpec((None, D_MODEL, tn), lambda l, j: (l, 0, j)),
            pl.BlockSpec((None, 1, tn), lambda l, j: (l, 0, j)),
        ],
        out_specs=pl.BlockSpec((None, 8, tn), lambda l, j: (l, 0, j)),
        compiler_params=_cparams(("parallel", "parallel")),
        name="modulation",
    )(cvecs, w_mod, b_mod.reshape(DEPTH, 1, 6 * D_MODEL))


def _pair_swap(s):
    lane = lax.broadcasted_iota(jnp.int32, s.shape, 1)
    n = s.shape[1]
    return jnp.where(lane % 2 == 0, pltpu.roll(s, n - 1, 1), pltpu.roll(s, 1, 1))


def _store_expanded_q(q_ref, s, slab, kv_half_of, low):
    sr = pltpu.roll(s, HEAD_DIM, 1)
    for half in (0, 1):
        hd = 2 * slab + half
        dst = kv_half_of(hd)
        val = s if dst == half else sr
        keep = low if dst == 0 else jnp.logical_not(low)
        q_ref[:, hd * LANES:(hd + 1) * LANES] = jnp.where(keep, val, 0.0).astype(BF16)


def _proj_columns(hb, w_ref):
    full = _dot(hb, w_ref[...])
    return lambda c0, width: full[:, c0:c0 + width]


def _proj_even_kernel(*refs, rope, q_scale):
    if rope:
        (x_ref, mod_ref, g_ref, w_ref, gq_ref, gk_ref, e_ref, cos_ref, sin_ref,
         a_ref, q_ref, k_ref, v_ref) = refs
    else:
        (x_ref, mod_ref, g_ref, w_ref, gq_ref, gk_ref, e_ref,
         a_ref, q_ref, k_ref, v_ref) = refs
    rows = x_ref.shape[0]
    h = _norm_mod(x_ref[...], g_ref[...], mod_ref[1:2, :], mod_ref[0:1, :])
    u = _proj_columns(h.astype(BF16), w_ref)
    low = _low_half(rows)
    a_ref[...] = u(0, POOL_WIDTH)

    def headnorm(s, g):
        ms = _dot((s * s).astype(BF16), e_ref[...])
        return s * lax.rsqrt(ms + EPS) * g

    def rot(s):
        if not rope:
            return s
        return s * cos_ref[...] + _pair_swap(s) * sin_ref[...]

    q0 = POOL_WIDTH
    for j in range(B_HEADS // 2):
        s = rot(headnorm(u(q0 + LANES * j, LANES), gq_ref[...])) * q_scale
        _store_expanded_q(q_ref, s, j, lambda hd: (hd // (B_HEADS // B_KV)) % 2, low)
    k0 = q0 + B_HEADS * HEAD_DIM
    for j in range(B_KV // 2):
        s = headnorm(u(k0 + LANES * j, LANES), gk_ref[...])
        k_ref[:, LANES * j:LANES * (j + 1)] = rot(s).astype(k_ref.dtype)
    v0 = k0 + B_KV * HEAD_DIM
    v_ref[...] = u(v0, B_KV * HEAD_DIM).astype(v_ref.dtype)


def _proj_odd_kernel(*refs, rope):
    if rope:
        (x_ref, mod_ref, g_ref, w_ref, cos_ref, sin_ref,
         qc_ref, kc_ref, vc_ref, qd_ref, kd_ref, vd_ref) = refs
    else:
        (x_ref, mod_ref, g_ref, w_ref,
         qc_ref, kc_ref, vc_ref, qd_ref, kd_ref, vd_ref) = refs
    rows = x_ref.shape[0]
    h = _norm_mod(x_ref[...], g_ref[...], mod_ref[1:2, :], mod_ref[0:1, :])
    u = _proj_columns(h.astype(BF16), w_ref)
    low = _low_half(rows)

    def rot(s):
        if not rope:
            return s
        return s * cos_ref[...] + _pair_swap(s) * sin_ref[...]

    for j in range(C_HEADS // 2):
        s = rot(u(LANES * j, LANES)) * QK_SCALE
        _store_expanded_q(qc_ref, s, j, lambda hd: hd // (C_HEADS // C_KV), low)
    o = C_HEADS * HEAD_DIM
    kc_ref[...] = rot(u(o, LANES)).astype(kc_ref.dtype)
    o += C_KV * HEAD_DIM
    vc_ref[...] = u(o, LANES).astype(vc_ref.dtype)
    o += C_KV * HEAD_DIM
    for j in range(D_HEADS // 2):
        s = u(o + LANES * j, LANES) * QK_SCALE
        _store_expanded_q(qd_ref, s, j, lambda hd: hd % 2, low)
    o += D_HEADS * HEAD_DIM
    kd_ref[...] = u(o, D_HEADS * HEAD_DIM).astype(kd_ref.dtype)
    o += D_HEADS * HEAD_DIM
    vd_ref[...] = u(o, D_HEADS * HEAD_DIM).astype(vd_ref.dtype)


def _proj_in(x, mods, layer, g_all, w_all, w_layer, extras, out_widths, out_dtypes, kernel, seq_blocks,
             rope_tabs, name):
    rows = x.shape[0]
    tm = TM_PROJ
    const = lambda i: (0, 0)
    in_specs = [
        pl.BlockSpec((tm, D_MODEL), lambda i: (i, 0)),
        _mod_spec(layer, seq_blocks),
        _layer_spec((1, D_MODEL), layer),
        _layer_spec(w_all.shape[1:], w_layer),
    ]
    args = [x, mods, g_all, w_all]
    for e in extras:
        in_specs.append(pl.BlockSpec(e.shape, const))
        args.append(e)
    if rope_tabs is not None:
        for t in rope_tabs:
            in_specs.append(pl.BlockSpec((tm, LANES), lambda i: (i % seq_blocks, 0)))
            args.append(t)
    return pl.pallas_call(
        functools.partial(kernel, rope=rope_tabs is not None),
        out_shape=[jax.ShapeDtypeStruct((rows, wd), dt) for wd, dt in zip(out_widths, out_dtypes)],
        grid=(rows // tm,),
        in_specs=in_specs,
        out_specs=[pl.BlockSpec((tm, wd), lambda i: (i, 0)) for wd in out_widths],
        compiler_params=_cparams(("parallel",)),
        name=name,
    )(*args)


def _place_heads(out_ref, col0, heads, kv_half_of, low):
    for m in range(len(heads) // 2):
        a = heads[2 * m]
        if kv_half_of(2 * m) == 1:
            a = pltpu.roll(a, HEAD_DIM, 1)
        b = heads[2 * m + 1]
        if kv_half_of(2 * m + 1) == 0:
            b = pltpu.roll(b, HEAD_DIM, 1)
        out_ref[:, col0 + m * LANES:col0 + (m + 1) * LANES] = jnp.where(low, a, b).astype(out_ref.dtype)


def _stack_heads(q_ref, heads):
    return jnp.concatenate([q_ref[:, h * LANES:(h + 1) * LANES] for h in heads], axis=0)


def _softmax_pv(scores, values, sink=None):
    s = scores[0] if len(scores) == 1 else jnp.concatenate(scores, axis=1)
    v = values[0] if len(values) == 1 else jnp.concatenate(values, axis=0)
    m = jnp.broadcast_to(s.max(axis=-1, keepdims=True), (s.shape[0], LANES))
    if sink is not None:
        m = jnp.maximum(m, sink)
    e = jnp.exp(s - jnp.concatenate([m] * (s.shape[1] // LANES), axis=1))
    acc = _dot(e.astype(BF16), jnp.concatenate([v, jnp.ones_like(v)], axis=1))
    den = acc[:, LANES:]
    if sink is not None:
        den = den + jnp.exp(sink - m)
    return acc[:, :LANES] / den


def _sink_column(sink_ref, heads, rows_per_head):
    row = lax.broadcasted_iota(jnp.int32, (len(heads) * rows_per_head, LANES), 0)
    col = jnp.full(row.shape, sink_ref[heads[-1]], F32)
    for g in range(len(heads) - 2, -1, -1):
        col = jnp.where(row < (g + 1) * rows_per_head, sink_ref[heads[g]], col)
    return col


def _ctx_even_attn_seq(q_ref, k_ref, v_ref, y_ref):
    rows = q_ref.shape[0]
    low = _low_half(rows)
    grp = B_HEADS // B_KV
    outs = []
    for kv in range(B_KV):
        sl = slice((kv // 2) * LANES, (kv // 2 + 1) * LANES)
        ks = k_ref[:, sl].astype(BF16)
        vs = v_ref[:, sl].astype(BF16)
        heads = list(range(kv * grp, (kv + 1) * grp))
        o = _softmax_pv([_dot_nt(_stack_heads(q_ref, heads), ks)], [vs])
        outs += [o[g * rows:(g + 1) * rows] for g in range(grp)]
    _place_heads(y_ref, 0, outs, lambda hd: (hd // grp) % 2, low)


def _ctx_odd_attn_seq(sink_ref, qc_ref, kc_ref, vc_ref, qd_ref, kd_ref, vd_ref, y_ref):
    rows = qc_ref.shape[0]
    low = _low_half(rows)
    grp = C_HEADS // C_KV
    ks = kc_ref[...].astype(BF16)
    vs = vc_ref[...].astype(BF16)
    outs = []
    for kv in range(C_KV):
        heads = list(range(kv * grp, (kv + 1) * grp))
        sink = _sink_column(sink_ref, heads, rows)
        o = _softmax_pv([_dot_nt(_stack_heads(qc_ref, heads), ks)], [vs], sink)
        outs += [o[g * rows:(g + 1) * rows] for g in range(grp)]
    _place_heads(y_ref, 0, outs, lambda hd: hd // grp, low)
    outs = []
    for hd in range(D_HEADS):
        sl = slice((hd // 2) * LANES, (hd // 2 + 1) * LANES)
        s = _dot_nt(qd_ref[:, hd * LANES:(hd + 1) * LANES], kd_ref[:, sl].astype(BF16))
        outs.append(_softmax_pv([s], [vd_ref[:, sl].astype(BF16)]))
    _place_heads(y_ref, C_HEADS * HEAD_DIM, outs, lambda hd: hd % 2, low)


def _per_sequence(*refs, fn, seq, n_scalar=0):
    for i in range(refs[n_scalar].shape[0] // seq):
        rows = slice(i * seq, (i + 1) * seq)
        fn(*refs[:n_scalar], *[r.at[rows] for r in refs[n_scalar:]])


def _ctx_even_attn(q, k, v, seq):
    rows = q.shape[0]
    blk = lambda wd: pl.BlockSpec((CTX_SEQS * seq, wd), lambda b: (b, 0))
    return pl.pallas_call(
        functools.partial(_per_sequence, fn=_ctx_even_attn_seq, seq=seq),
        out_shape=jax.ShapeDtypeStruct((rows, B_HEADS * HEAD_DIM), BF16),
        grid=(rows // (CTX_SEQS * seq),),
        in_specs=[blk(q.shape[1]), blk(k.shape[1]), blk(v.shape[1])],
        out_specs=blk(B_HEADS * HEAD_DIM),
        compiler_params=_cparams(("parallel",)),
        name="ctx_even_attn",
    )(q, k, v)


def _ctx_odd_attn(sink_all, layer, qc, kc, vc, qd, kd, vd, seq):
    rows = qc.shape[0]
    blk = lambda wd: pl.BlockSpec((CTX_SEQS * seq, wd), lambda b: (b, 0))
    return pl.pallas_call(
        functools.partial(_per_sequence, fn=_ctx_odd_attn_seq, seq=seq, n_scalar=1),
        out_shape=jax.ShapeDtypeStruct((rows, D_MODEL), BF16),
        grid=(rows // (CTX_SEQS * seq),),
        in_specs=[pl.BlockSpec(memory_space=pltpu.SMEM)] + [blk(a.shape[1]) for a in (qc, kc, vc, qd, kd, vd)],
        out_specs=blk(D_MODEL),
        compiler_params=_cparams(("parallel",)),
        name="ctx_odd_attn",
    )(sink_all[layer], qc, kc, vc, qd, kd, vd)


def _pool_kernel(ap_ref, a_ref, an_ref, w_ref, ps_ref, y_ref, *, seq):
    tm = a_ref.shape[0]
    i = pl.program_id(0)
    pos0 = (i * tm) % seq
    a = a_ref[...]
    a_ext = jnp.concatenate([ap_ref[...], a, an_ref[...]], axis=0).astype(BF16)
    ext = tm + 2 * HALO
    r = lax.broadcasted_iota(jnp.int32, (tm, ext), 0)
    c = lax.broadcasted_iota(jnp.int32, (tm, ext), 1)
    pos_c = pos0 - HALO + c
    in_seq = (pos_c >= 0) & (pos_c < seq)
    d = c - HALO - r
    pos_r = pos0 + lax.broadcasted_iota(jnp.int32, (tm, POOL_WIDTH), 0)
    grp = lax.broadcasted_iota(jnp.int32, (tm, POOL_WIDTH), 1) // (POOL_WIDTH // len(POOL_WINDOWS))
    mean = jnp.zeros((tm, POOL_WIDTH), F32)
    for gi, wdw in enumerate(POOL_WINDOWS):
        band = jnp.where(in_seq & (d >= -(wdw // 2)) & (d < wdw - wdw // 2), 1.0, 0.0).astype(BF16)
        lo = jnp.maximum(pos_r - wdw // 2, 0)
        hi = jnp.minimum(pos_r - wdw // 2 + wdw, seq)
        win_mean = _dot(band, a_ext) / (hi - lo).astype(F32)
        mean = jnp.where(grp == gi, win_mean, mean)
    pooled = (mean - a).astype(BF16)
    y_ref[...] = (_dot(pooled, w_ref[...]) * ps_ref[...]).astype(y_ref.dtype)


def _halo_specs(tm, width, n_rows):
    per = tm // HALO
    last = n_rows // HALO - 1
    prev = pl.BlockSpec((HALO, width), lambda i: (jnp.maximum(i * per - 1, 0), 0))
    cur = pl.BlockSpec((tm, width), lambda i: (i, 0))
    nxt = pl.BlockSpec((HALO, width), lambda i: (jnp.minimum((i + 1) * per, last), 0))
    return [prev, cur, nxt]


def _pool_mixer(a, w_bd, pscale_all, layer, seq):
    rows = a.shape[0]
    tm = TM_POOL
    assert seq % tm == 0
    const = lambda i: (0, 0)
    return pl.pallas_call(
        functools.partial(_pool_kernel, seq=seq),
        out_shape=jax.ShapeDtypeStruct((rows, POOL_WIDTH), BF16),
        grid=(rows // tm,),
        in_specs=_halo_specs(tm, POOL_WIDTH, rows) + [
            pl.BlockSpec((POOL_WIDTH, POOL_WIDTH), const),
            _layer_spec((1, POOL_WIDTH), layer),
        ],
        out_specs=pl.BlockSpec((tm, POOL_WIDTH), lambda i: (i, 0)),
        compiler_params=_cparams(("parallel",)),
        name="pool_mixer",
    )(a, a, a, w_bd, pscale_all)


def _flash_kernel(q_ref, k_ref, v_ref, ck_ref, cv_ref, y_ref, m_ref, acc_ref):
    tq = q_ref.shape[0]
    low = _low_half(tq)
    grp = B_HEADS // B_KV
    n_chunks = k_ref.shape[0] // TK_FLASH
    qs = [_stack_heads(q_ref, list(range(j * grp, (j + 1) * grp))) for j in range(2)]
    def step(kc, vc, first=False):
        lowk = _low_half(vc.shape[0])
        for j in range(2):
            own = lowk if j == 0 else jnp.logical_not(lowk)
            vj = jnp.where(own, vc, jnp.ones_like(vc))
            s = _dot_nt(qs[j], kc)
            m_new = s.max(axis=-1, keepdims=True)
            m_new = jnp.broadcast_to(m_new, (s.shape[0], LANES)) if first else jnp.maximum(m_ref[j], m_new)
            p = jnp.exp2((s - jnp.concatenate([m_new] * (s.shape[1] // LANES), axis=1)).astype(BF16))
            pv = _dot(p, vj)
            acc_ref[j] = pv if first else jnp.exp2(m_ref[j] - m_new) * acc_ref[j] + pv
            m_ref[j] = m_new

    def body(c, carry):
        off = pl.multiple_of(c * TK_FLASH, TK_FLASH)
        step(k_ref[pl.ds(off, TK_FLASH), :], v_ref[pl.ds(off, TK_FLASH), :])
        return carry

    step(ck_ref[...], cv_ref[...], first=True)
    lax.fori_loop(0, n_chunks, body, 0, unroll=4)
    outs = []
    for j in range(2):
        acc = acc_ref[j]
        o = acc / pltpu.roll(acc, HEAD_DIM, 1)
        outs += [o[g * tq:(g + 1) * tq] for g in range(grp)]
    _place_heads(y_ref, 0, outs, lambda hd: hd // grp, low)


def _flash_attn(q, k, v, ck_all, cv_all, layer, n_batch):
    rows = q.shape[0]
    seq = rows // n_batch
    past = ck_all.shape[2]
    tq = TQ_FLASH
    nq = seq // tq
    grp = B_HEADS // B_KV
    qw = 2 * grp * LANES
    yw = 2 * grp * HEAD_DIM
    ctx = pl.BlockSpec((None, None, past, LANES), lambda b, p, i: (b, layer, 0, p))
    return pl.pallas_call(
        _flash_kernel,
        out_shape=jax.ShapeDtypeStruct((rows, B_HEADS * HEAD_DIM), BF16),
        grid=(n_batch, B_KV // 2, nq),
        in_specs=[
            pl.BlockSpec((tq, qw), lambda b, p, i: (b * nq + i, p)),
            pl.BlockSpec((seq, LANES), lambda b, p, i: (b, p)),
            pl.BlockSpec((seq, LANES), lambda b, p, i: (b, p)),
            ctx, ctx,
        ],
        out_specs=pl.BlockSpec((tq, yw), lambda b, p, i: (b * nq + i, p)),
        scratch_shapes=[
            pltpu.VMEM((2, grp * tq, LANES), F32),
            pltpu.VMEM((2, grp * tq, LANES), F32),
        ],
        compiler_params=_cparams(("parallel", "parallel", "arbitrary")),
        name="latent_flash_attn",
    )(q, k, v, ck_all, cv_all)


def _window_kernel(sink_ref, q_ref, kp_ref, kc_ref, kn_ref, vp_ref, vc_ref, vn_ref, ck_ref, cv_ref, y_ref,
                   *, n_blocks):
    tq = q_ref.shape[0]
    n = pl.program_id(1)
    low = _low_half(tq)
    grp = C_HEADS // C_KV
    n_loc = tq + 2 * C_WINDOW
    n_keys = n_loc + ck_ref.shape[0]
    r = lax.broadcasted_iota(jnp.int32, (tq, n_keys), 0)
    c = lax.broadcasted_iota(jnp.int32, (tq, n_keys), 1)
    ok = ((jnp.abs(r - (c - C_WINDOW)) <= C_WINDOW)
          & ((c >= C_WINDOW) | (n > 0)) & ((c < C_WINDOW + tq) | (n < n_blocks - 1)))
    mask = jnp.where(ok | (c >= n_loc), 0.0, NEG)
    half = grp // 2
    mask = jnp.concatenate([mask] * half, axis=0)
    k_all = jnp.concatenate([kp_ref[...], kc_ref[...], kn_ref[...], ck_ref[...]], axis=0)
    v_all = jnp.concatenate([vp_ref[...], vc_ref[...], vn_ref[...], cv_ref[...]], axis=0)
    outs = []
    for h0 in range(0, C_HEADS, half):
        heads = list(range(h0, h0 + half))
        q = _stack_heads(q_ref, heads)
        sink = _sink_column(sink_ref, heads, tq)
        o = _softmax_pv([_dot_nt(q, k_all) + mask], [v_all], sink)
        outs += [o[g * tq:(g + 1) * tq] for g in range(half)]
    _place_heads(y_ref, 0, outs, lambda hd: hd // grp, low)


def _window_attn(sink_all, q, k, v, ck_all, cv_all, layer, n_batch):
    rows = q.shape[0]
    seq = rows // n_batch
    past = ck_all.shape[2]
    tq = TQ_WIN
    assert tq % C_WINDOW == 0 and seq % tq == 0
    nb = seq // tq
    per = tq // C_WINDOW
    nw = seq // C_WINDOW
    prev = pl.BlockSpec((C_WINDOW, LANES), lambda b, n: (b * nw + jnp.maximum(n * per - 1, 0), 0))
    cur = pl.BlockSpec((tq, LANES), lambda b, n: (b * nb + n, 0))
    nxt = pl.BlockSpec((C_WINDOW, LANES), lambda b, n: (b * nw + jnp.minimum((n + 1) * per, nw - 1), 0))
    ctx = pl.BlockSpec((None, None, past, LANES), lambda b, n: (b, layer, 0, 0))
    return pl.pallas_call(
        functools.partial(_window_kernel, n_blocks=nb),
        out_shape=jax.ShapeDtypeStruct((rows, C_HEADS * HEAD_DIM), BF16),
        grid=(n_batch, nb),
        in_specs=[
            pl.BlockSpec(memory_space=pltpu.SMEM),
            pl.BlockSpec((tq, C_HEADS * LANES), lambda b, n: (b * nb + n, 0)),
            prev, cur, nxt, prev, cur, nxt, ctx, ctx,
        ],
        out_specs=pl.BlockSpec((tq, C_HEADS * HEAD_DIM), lambda b, n: (b * nb + n, 0)),
        compiler_params=_cparams(("parallel", "parallel")),
        name="latent_window_attn",
    )(sink_all[layer], q, k, k, k, v, v, v, ck_all, cv_all)


def _na_window_start(blk, n_rows):
    return jnp.clip(blk * NA_QROWS - NA_ROWS // 2, 0, n_rows - NA_WIN)


def _na_kernel(q_ref, k_ref, v_ref, ck_ref, cv_ref, pair_ref, y_ref, *, n_rows):
    blk = pl.program_id(1)
    tq = NA_QROWS * GRID_W
    win = NA_WIN * GRID_W
    low = _low_half(tq)
    ws = _na_window_start(blk, n_rows)
    start = pl.multiple_of(ws * GRID_W, GRID_W)
    q_row = blk * NA_QROWS + lax.broadcasted_iota(jnp.int32, (tq, win), 0) // GRID_W
    k_row = ws + lax.broadcasted_iota(jnp.int32, (tq, win), 1) // GRID_W
    rs = jnp.clip(q_row - NA_ROWS // 2, 0, n_rows - NA_ROWS)
    row_mask = jnp.where((k_row >= rs) & (k_row < rs + NA_ROWS), 0.0, NEG)
    outs = []
    for hd in range(D_HEADS):
        sl = slice((hd // 2) * LANES, (hd // 2 + 1) * LANES)
        q = q_ref[:, hd * LANES:(hd + 1) * LANES]
        kw = k_ref[pl.ds(start, win), sl]
        vw = v_ref[pl.ds(start, win), sl]
        bias = jnp.concatenate([
            jnp.concatenate([
                pair_ref[hd, jnp.clip(ws + 2 * jp - (blk * NA_QROWS + a) + NA_ROWS, 0, 2 * NA_ROWS)]
                for jp in range(NA_WIN // 2)], axis=1)
            for a in range(NA_QROWS)], axis=0)
        s_loc = _dot_nt(q, kw) + bias + row_mask
        s_ctx = _dot_nt(q, ck_ref[:, sl])
        outs.append(_softmax_pv([s_loc, s_ctx], [vw, cv_ref[:, sl]]))
    _place_heads(y_ref, 0, outs, lambda hd: hd % 2, low)


def _na_attn(q, k, v, ck_all, cv_all, pair_all, layer, n_batch):
    rows = q.shape[0]
    seq = rows // n_batch
    past = ck_all.shape[2]
    n_rows = seq // GRID_W
    n_blk = n_rows // NA_QROWS
    assert n_rows >= NA_WIN and NA_WIN % 2 == 0
    width = D_HEADS * HEAD_DIM
    tq = NA_QROWS * GRID_W
    full = pl.BlockSpec((seq, width), lambda b, r: (b, 0))
    ctx = pl.BlockSpec((None, None, past, width), lambda b, r: (b, layer, 0, 0))
    return pl.pallas_call(
        functools.partial(_na_kernel, n_rows=n_rows),
        out_shape=jax.ShapeDtypeStruct((rows, width), BF16),
        grid=(n_batch, n_blk),
        in_specs=[
            pl.BlockSpec((tq, D_HEADS * LANES), lambda b, r: (b * n_blk + r, 0)),
            full, full, ctx, ctx,
            _layer_spec(pair_all.shape[1:], layer),
        ],
        out_specs=pl.BlockSpec((tq, width), lambda b, r: (b * n_blk + r, 0)),
        compiler_params=_cparams(("parallel", "arbitrary")),
        name="latent_neighbourhood_attn",
    )(q, k, v, ck_all, cv_all, pair_all)


def _na_pair_tiles(rpb_all):
    n_l, n_h, n_y, n_x = rpb_all.shape
    n_tiles = 2 * NA_ROWS + 1
    zero = jnp.zeros((n_l, n_h, 1, n_x), F32)
    rows = jnp.concatenate([zero, rpb_all, zero, zero], axis=2)
    feats = jnp.concatenate([rows[:, :, :-1], rows[:, :, 1:]], axis=-1)
    feat = np.arange(2 * n_x)[:, None, None]
    qc = np.arange(GRID_W)[None, :, None]
    lane = np.arange(2 * GRID_W)[None, None, :]
    kc = lane % GRID_W
    cs = np.clip(qc - NA_COLS // 2, 0, GRID_W - NA_COLS)
    inside = (kc >= cs) & (kc < cs + NA_COLS)
    select = (lane // GRID_W == feat // n_x) & (feat % n_x == kc - qc + NA_COLS - 1) & inside
    iy = np.arange(n_tiles)[:, None, None] - 1 + lane // GRID_W
    valid = inside & (iy >= 0) & (iy < n_y)
    tiles = jnp.einsum('lhef,fqx->lheqx', feats, jnp.asarray(select, F32), precision=lax.Precision.HIGHEST)
    return jnp.where(valid, tiles, NEG)


def _mixer_ffn_kernel(*refs, seq, halo, widths):
    per = 3 if halo else 1
    n_in = per * (1 + len(widths))
    x_refs = refs[:per]
    part_refs = [refs[per * (1 + k):per * (2 + k)] for k in range(len(widths))]
    (mod_ref, gmix_ref, wmix_ref, gpre_ref, gpost_ref, wi_ref, wc_ref, wo_ref, o_ref, y_ref, h_ref) = refs[n_in:]
    tm = x_refs[per // 2].shape[0]
    pad = HALO if halo else 0
    ext = tm + 2 * pad
    pieces = [(pad, tm)] if not halo else [(0, pad), (pad, tm), (pad + tm, pad)]
    keep = [None]
    if halo:
        pos0 = (pl.program_id(0) * tm) % seq
        keep = [jnp.where(pos0 > 0, 1.0, 0.0), None, jnp.where(pos0 + tm < seq, 1.0, 0.0)]

    off = 0
    for p_refs, wd in zip(part_refs, widths):
        for (r0, n), p_ref in zip(pieces, p_refs):
            y_ref[r0:r0 + n, off:off + wd] = p_ref[...]
        off += wd
    t = _dot(y_ref[...], wmix_ref[...])
    scale, shift = mod_ref[4:5, :], mod_ref[3:4, :]
    for (r0, n), x_ref, kp in zip(pieces, x_refs, keep):
        x1 = x_ref[...] + mod_ref[2:3, :] * _rmsnorm(t[r0:r0 + n], gmix_ref[...])
        h = _norm_mod(x1, gpre_ref[...], scale, shift)
        if kp is None:
            o_ref[...] = x1
        else:
            h = h * kp
        h_ref[r0:r0 + n, :] = h.astype(BF16)

    def conv_up(c0, width):
        u = _dot(h_ref[...], wi_ref[:, c0:c0 + width])
        up = pltpu.roll(u, 1, 0)[pad:pad + tm]
        un = pltpu.roll(u, ext - 1, 0)[pad:pad + tm]
        if not halo:
            sub = lax.broadcasted_iota(jnp.int32, (8, width), 0)
            ups, uns = [], []
            for s0 in range(0, tm, seq):
                ups += [jnp.where(sub == 0, 0.0, up[s0:s0 + 8]), up[s0 + 8:s0 + seq]]
                uns += [un[s0:s0 + seq - 8], jnp.where(sub == 7, 0.0, un[s0 + seq - 8:s0 + seq])]
            up = jnp.concatenate(ups, axis=0)
            un = jnp.concatenate(uns, axis=0)
        return (u[pad:pad + tm] * wc_ref[1:2, c0:c0 + width] + up * wc_ref[0:1, c0:c0 + width]
                + un * wc_ref[2:3, c0:c0 + width])

    y = None
    for c0, width in FF_CHUNKS:
        gate = conv_up(c0, width)
        val = conv_up(D_FF + c0, width)
        act = gate * (1.0 / (1.0 + jnp.exp(-gate))) * val
        contrib = _dot(act.astype(BF16), wo_ref[c0:c0 + width, :])
        y = contrib if y is None else y + contrib
    o_ref[...] += mod_ref[5:6, :] * _rmsnorm(y, gpost_ref[...])


def _mixer_ffn(x, parts, mods, layer, g_mix_all, w_mix_all, g_pre_all, g_post_all, wi_all, wc_all, wo_all,
               seq, seq_blocks):
    rows = x.shape[0]
    tm = TM_FFN
    assert seq % tm == 0 or tm % seq == 0
    halo = tm < seq
    widths = tuple(p.shape[1] for p in parts)
    assert sum(widths) == D_MODEL
    resident = dict(pipeline_mode=pl.Buffered(1))
    zeros3 = lambda i: (layer, 0, 0)
    row_specs = lambda wd: (_halo_specs(tm, wd, rows) if halo else [pl.BlockSpec((tm, wd), lambda i: (i, 0))])
    in_specs, args = [], []
    for arr in (x,) + tuple(parts):
        specs = row_specs(arr.shape[1])
        in_specs += specs
        args += [arr] * len(specs)
    in_specs += [
        _mod_spec(layer, seq_blocks),
        _layer_spec((1, D_MODEL), layer),
        _layer_spec((D_MODEL, D_MODEL), layer),
        _layer_spec((1, D_MODEL), layer),
        _layer_spec((1, D_MODEL), layer),
        pl.BlockSpec((None, D_MODEL, 2 * D_FF), zeros3, **resident),
        _layer_spec((3, 2 * D_FF), layer),
        pl.BlockSpec((None, D_FF, D_MODEL), zeros3, **resident),
    ]
    args += [mods, g_mix_all, w_mix_all, g_pre_all, g_post_all, wi_all, wc_all, wo_all]
    ext = tm + (2 * HALO if halo else 0)
    return pl.pallas_call(
        functools.partial(_mixer_ffn_kernel, seq=seq, halo=halo, widths=widths),
        out_shape=jax.ShapeDtypeStruct((rows, D_MODEL), F32),
        grid=(rows // tm,),
        in_specs=in_specs,
        out_specs=pl.BlockSpec((tm, D_MODEL), lambda i: (i, 0)),
        scratch_shapes=[pltpu.VMEM((ext, D_MODEL), BF16), pltpu.VMEM((ext, D_MODEL), BF16)],
        compiler_params=_cparams(("parallel",)),
        name="mixer_ffn",
    )(*args)


def _rope_tables(n_tokens):
    t = jnp.arange(n_tokens)
    row = (t // GRID_W).astype(F32)
    col = (t % GRID_W).astype(F32)
    axis_dim = HEAD_DIM // 2
    inv_freq = ROPE_THETA ** (-jnp.arange(0, axis_dim, 2, dtype=F32) / axis_dim)
    ang = jnp.concatenate([row[:, None] * inv_freq, col[:, None] * inv_freq], axis=-1)
    cos = jnp.repeat(jnp.cos(ang), 2, axis=-1)
    sin = jnp.sin(ang)
    sin = jnp.stack([-sin, sin], axis=-1).reshape(n_tokens, HEAD_DIM)
    return jnp.tile(cos, (1, 2)), jnp.tile(sin, (1, 2))


def _rows3(p):
    return p.reshape(p.shape[0], 1, p.shape[1])


def kernel(x_prompt, x_sample, cache_b_k, cache_b_v, cache_c_k, cache_c_v, cache_d_k, cache_d_v, c, c_ctx,
           w_mod, b_mod, g_mix_pre, g_mix_post, g_ffn_pre, g_ffn_post, w_in_even, w_pool, pool_scale,
           g_q_b, g_k_b, w_in_odd, sink_c, rpb_d, w_mix_out, w_ffn_in, w_ffn_conv, w_ffn_out):
    n_ctx, ctx_seq, _ = x_prompt.shape
    n_lat, lat_seq, _ = x_sample.shape
    past = cache_b_k.shape[2]
    xp = x_prompt.reshape(n_ctx * ctx_seq, D_MODEL)
    xs = x_sample.reshape(n_lat * lat_seq, D_MODEL)
    lat_blocks = lat_seq // TM_PROJ

    cvecs = jnp.concatenate([c_ctx[None, :], c, jnp.zeros((8 - 1 - n_lat, D_MODEL), F32)], axis=0)
    mods = _modulation(cvecs, w_mod, b_mod).reshape(DEPTH, 8, 6, D_MODEL)

    rope_tabs = _rope_tables(lat_seq)
    blk = jnp.arange(LANES) // HEAD_DIM
    e_mat = jnp.where(blk[:, None] == blk[None, :], 1.0 / HEAD_DIM, 0.0).astype(BF16)

    w_even = w_in_even.astype(BF16)
    w_odd = w_in_odd.astype(BF16)
    w_out = w_mix_out.astype(BF16)
    w_fi = w_ffn_in.astype(BF16)
    w_fo = w_ffn_out.astype(BF16)
    slabs = lambda t: t.astype(BF16).reshape(t.shape[0], t.shape[1], past, t.shape[3] * HEAD_DIM)
    cbk, cbv, cck, ccv, cdk, cdv = map(slabs, (cache_b_k, cache_b_v, cache_c_k, cache_c_v, cache_d_k, cache_d_v))
    g_pre, g_post = _rows3(g_mix_pre), _rows3(g_mix_post)
    gf_pre, gf_post = _rows3(g_ffn_pre), _rows3(g_ffn_post)
    pscale = _rows3(pool_scale)
    na_pairs = _na_pair_tiles(rpb_d)

    b_k, b_v, c_k, c_v, d_k, d_v = [], [], [], [], [], []
    for l in range(DEPTH):
        i = l // 2
        if l % 2 == 0:
            gq = jnp.tile(g_q_b[i], 2)[None, :]
            gk = jnp.tile(g_k_b[i], 2)[None, :]
            kvw = B_KV * HEAD_DIM
            widths = (POOL_WIDTH, B_HEADS * LANES, kvw, kvw)
            a_p, q_p, k_p, v_p = _proj_in(xp, mods, l, g_pre, w_even, i, (gq, gk, e_mat), widths,
                                          (F32, BF16, F32, F32),
                                          functools.partial(_proj_even_kernel, q_scale=QK_SCALE), None, None,
                                          "proj_even_ctx")
            a_s, q_s, k_s, v_s = _proj_in(xs, mods, l, g_pre, w_even, i, (gq, gk, e_mat), widths,
                                          (F32, BF16, BF16, BF16),
                                          functools.partial(_proj_even_kernel, q_scale=QK_SCALE * LOG2E),
                                          lat_blocks, rope_tabs,
                                          "proj_even_lat")
            b_k.append(k_p)
            b_v.append(v_p)
            w_bd = jax.scipy.linalg.block_diag(*[w_pool[i, g] for g in range(len(POOL_WINDOWS))]).astype(BF16)
            ya_p = _pool_mixer(a_p, w_bd, pscale, i, ctx_seq)
            ya_s = _pool_mixer(a_s, w_bd, pscale, i, lat_seq)
            yb_p = _ctx_even_attn(q_p, k_p, v_p, ctx_seq)
            yb_s = _flash_attn(q_s, k_s, v_s, cbk, cbv, i, n_lat)
            parts_p, parts_s = (ya_p, yb_p), (ya_s, yb_s)
        else:
            ckw, dw = C_KV * HEAD_DIM, D_HEADS * HEAD_DIM
            widths = (C_HEADS * LANES, ckw, ckw, D_HEADS * LANES, dw, dw)
            qc_p, kc_p, vc_p, qd_p, kd_p, vd_p = _proj_in(
                xp, mods, l, g_pre, w_odd, i, (), widths, (BF16, F32, F32, BF16, F32, F32),
                _proj_odd_kernel, None, None, "proj_odd_ctx")
            qc_s, kc_s, vc_s, qd_s, kd_s, vd_s = _proj_in(
                xs, mods, l, g_pre, w_odd, i, (), widths, (BF16,) * 6,
                _proj_odd_kernel, lat_blocks, rope_tabs, "proj_odd_lat")
            c_k.append(kc_p)
            c_v.append(vc_p)
            d_k.append(kd_p)
            d_v.append(vd_p)
            y_p = _ctx_odd_attn(sink_c, i, qc_p, kc_p, vc_p, qd_p, kd_p, vd_p, ctx_seq)
            yc_s = _window_attn(sink_c, qc_s, kc_s, vc_s, cck, ccv, i, n_lat)
            yd_s = _na_attn(qd_s, kd_s, vd_s, cdk, cdv, na_pairs, i, n_lat)
            parts_p, parts_s = (y_p,), (yc_s, yd_s)

        xp = _mixer_ffn(xp, parts_p, mods, l, g_post, w_out, gf_pre, gf_post, w_fi, w_ffn_conv, w_fo,
                        ctx_seq, None)
        xs = _mixer_ffn(xs, parts_s, mods, l, g_post, w_out, gf_pre, gf_post, w_fi, w_ffn_conv, w_fo,
                        lat_seq, lat_seq // TM_FFN)

    def cache(per_layer):
        t = jnp.stack([a.reshape(n_ctx, ctx_seq, a.shape[1]) for a in per_layer], axis=1)
        return t.reshape(n_ctx, len(per_layer), ctx_seq, t.shape[-1] // HEAD_DIM, HEAD_DIM)

    return (xp.reshape(n_ctx, ctx_seq, D_MODEL), xs.reshape(n_lat, lat_seq, D_MODEL),
            cache(b_k), cache(b_v), cache(c_k), cache(c_v), cache(d_k), cache(d_v))
```
